```python
import jax, jax.numpy as jnp
from jax import lax
import numpy as np

D_MODEL = 1024
BATCH = 16
SEQ = 4096
DEPTH = 1
DEC_BATCH = 8
DEC_SEQ = 64
PAST_LEN = 4096

CHUNK = 64
N_META = 16
EPS = 1e-5
GLA_HEADS = 4
GLA_DK = 128
GLA_DV = 256
GLA_RANK = 16
GLA_TAU = 16.0
RET_HEADS = 4
RET_DK = 128
RET_DV = 256
ROPE_BASE = 10000.0
N_EXPERTS = 32
TOP_K = 4
D_FF = 1024
SWIGLU_ALPHA = 1.702
SWIGLU_LIMIT = 7.0
MOE_BLOCK = 128
IN_WIDTHS = (GLA_HEADS * GLA_DK, GLA_HEADS * GLA_DK, GLA_HEADS * GLA_DV, GLA_HEADS * GLA_DV, GLA_RANK,
             RET_HEADS * RET_DK, RET_HEADS * RET_DK, RET_HEADS * RET_DV, RET_HEADS * RET_DV, 2 * D_MODEL)
IN_WIDTH = sum(IN_WIDTHS)

kernel_name = 'hybrid_gla_retention_moe_stream_step'


def rmsnorm(x, w):
    x32 = x.astype(jnp.float32)
    y = x32 * lax.rsqrt(jnp.mean(x32 * x32, axis=-1, keepdims=True) + EPS)
    return (y * w.astype(jnp.float32)).astype(x.dtype)


def head_rmsnorm(o, w):
    return o * lax.rsqrt(jnp.mean(o * o, axis=-1, keepdims=True) + EPS) * w.astype(jnp.float32)


def rotary(t, pos):
    d = t.shape[-1]
    inv = ROPE_BASE ** (-jnp.arange(0, d, 2, dtype=jnp.float32) / d)
    ang = pos.astype(jnp.float32)[:, None] * inv[None, :]
    cos, sin = jnp.cos(ang), jnp.sin(ang)
    t1, t2 = t[..., : d // 2], t[..., d // 2:]
    return jnp.concatenate([t1 * cos - t2 * sin, t1 * sin + t2 * cos], axis=-1)


def retention_log_decay():
    return jnp.log(1.0 - 2.0 ** (-5.0 - jnp.arange(RET_HEADS, dtype=jnp.float32)))


def branch_inputs(h, pos, w_in, w_alpha_up, b_alpha):
    B, T, _ = h.shape
    z = jnp.einsum('btd,dc->btc', h, w_in).astype(jnp.float32)
    offsets = [int(o) for o in np.cumsum(IN_WIDTHS)[:-1]]
    qa, ka, va, ga, ra, qb, kb, vb, gb, m = jnp.split(z, offsets, axis=-1)

    def heads(t, n):
        return t.reshape(B, T, n, -1).transpose(0, 2, 1, 3)

    la = jax.nn.log_sigmoid(ra @ w_alpha_up.astype(jnp.float32) + b_alpha.astype(jnp.float32)) / GLA_TAU
    qa_h = heads(qa, GLA_HEADS) * GLA_DK ** -0.5
    qb_h = rotary(heads(qb, RET_HEADS), pos)
    kb_h = rotary(heads(kb, RET_HEADS), pos) * RET_DK ** -0.5
    return (qa_h, heads(ka, GLA_HEADS), heads(va, GLA_HEADS), heads(la, GLA_HEADS), ga,
            qb_h, kb_h, heads(vb, RET_HEADS), gb, m)


def gla_update(S, k, v, b):
    b_last = b[:, :, -1:, :]
    return jnp.exp(b_last[:, :, 0, :, None]) * S + jnp.einsum('bhsd,bhsv->bhdv', k * jnp.exp(b_last - b), v)


def gla_chunk(S, q, k, v, la):
    L = q.shape[2]
    b = jnp.cumsum(la, axis=2)
    causal = jnp.tril(jnp.ones((L, L), dtype=bool))
    diff = b[:, :, :, None, :] - b[:, :, None, :, :]
    decay = jnp.exp(jnp.where(causal[:, :, None], diff, -jnp.inf))
    scores = jnp.einsum('bhtd,bhsd,bhtsd->bhts', q, k, decay)
    o = jnp.einsum('bhts,bhsv->bhtv', scores, v) + jnp.einsum('bhtd,bhdv->bhtv', q * jnp.exp(b), S)
    return gla_update(S, k, v, b), o


def ret_update(S, k, v, lg):
    L = k.shape[2]
    idx = jnp.arange(L, dtype=jnp.float32)
    kd = k * jnp.exp(lg[:, None] * (L - 1 - idx)[None, :])[None, :, :, None]
    return jnp.exp(lg * L)[None, :, None, None] * S + jnp.einsum('bhsd,bhsv->bhdv', kd, v)


def ret_chunk(S, q, k, v, lg):
    L = q.shape[2]
    idx = jnp.arange(L, dtype=jnp.float32)
    rel = idx[:, None] - idx[None, :]
    decay = jnp.exp(jnp.where(rel >= 0, lg[:, None, None] * rel, -jnp.inf))
    scores = jnp.einsum('bhtd,bhsd->bhts', q, k) * decay[None]
    inner = jnp.exp(lg[:, None] * (idx + 1.0)[None, :])[None, :, :, None]
    o = jnp.einsum('bhts,bhsv->bhtv', scores, v) + inner * jnp.einsum('bhtd,bhdv->bhtv', q, S)
    return ret_update(S, k, v, lg), o


def to_chunks(t, L):
    B, H, T, d = t.shape
    return t.reshape(B, H, T // L, L, d).transpose(2, 0, 1, 3, 4)


def from_chunks(t):
    nC, B, H, L, d = t.shape
    return t.transpose(1, 0, 3, 2, 4).reshape(B, nC * L, H, d)


def token_mixers(h, pos, S_gla, S_ret, chunk_len, w_in, w_alpha_up, b_alpha, gla_norm_w, ret_norm_w,
                 w_branch_gla, w_branch_ret, w_out):
    B, T, _ = h.shape
    qa, ka, va, la, ga, qb, kb, vb, gb, m = branch_inputs(h, pos, w_in, w_alpha_up, b_alpha)
    lg = retention_log_decay()

    def step(carry, xs):
        Sg, Sr = carry
        qa_c, ka_c, va_c, la_c, qb_c, kb_c, vb_c = xs
        Sg, oa_c = gla_chunk(Sg, qa_c, ka_c, va_c, la_c)
        Sr, ob_c = ret_chunk(Sr, qb_c, kb_c, vb_c, lg)
        return (Sg, Sr), (oa_c, ob_c)

    xs = tuple(to_chunks(t, chunk_len) for t in (qa, ka, va, la, qb, kb, vb))
    (Sg, Sr), (oa, ob) = lax.scan(step, (S_gla, S_ret), xs)
    oa = head_rmsnorm(from_chunks(oa), gla_norm_w).reshape(B, T, -1) * jax.nn.silu(ga)
    ob = head_rmsnorm(from_chunks(ob), ret_norm_w).reshape(B, T, -1) * jax.nn.silu(gb)
    ma, mb = jnp.split(m, 2, axis=-1)
    merged = jax.nn.sigmoid(ma) * (oa @ w_branch_gla) + jax.nn.sigmoid(mb) * (ob @ w_branch_ret)
    y = merged @ w_out
    return y.astype(h.dtype), Sg, Sr


def meta_states(h, pos, S_gla, S_ret, w_in, w_alpha_up, b_alpha):
    _, ka, va, la, _, _, kb, vb, _, _ = branch_inputs(h, pos, w_in, w_alpha_up, b_alpha)
    return gla_update(S_gla, ka, va, jnp.cumsum(la, axis=2)), ret_update(S_ret, kb, vb, retention_log_decay())


def clamped_swiglu(hgu):
    glu, lin = hgu[..., :D_FF], hgu[..., D_FF:]
    glu = jnp.minimum(glu, SWIGLU_LIMIT)
    lin = jnp.clip(lin, -SWIGLU_LIMIT, SWIGLU_LIMIT)
    return glu * jax.nn.sigmoid(SWIGLU_ALPHA * glu) * (lin + 1.0)


def moe(x, w_router, b_router, w_gate_up, b_gate_up, w_down, b_down):
    T, D = x.shape
    logits = x.astype(jnp.float32) @ w_router.astype(jnp.float32) + b_router.astype(jnp.float32)
    top_logit, top_e = lax.top_k(logits, TOP_K)
    gate = jax.nn.softmax(top_logit, axis=-1)
    A = T * TOP_K
    flat_e = top_e.reshape(A)
    order = jnp.argsort(flat_e)
    sorted_e = flat_e[order]
    tok_of = order // TOP_K
    counts = jnp.bincount(flat_e, length=N_EXPERTS)
    padded = (counts + MOE_BLOCK - 1) // MOE_BLOCK * MOE_BLOCK
    start = jnp.cumsum(counts) - counts
    pend = jnp.cumsum(padded)
    pstart = pend - padded
    dest = pstart[sorted_e] + jnp.arange(A) - start[sorted_e]
    n_blocks = -(-(A + N_EXPERTS * (MOE_BLOCK - 1)) // MOE_BLOCK)
    P = n_blocks * MOE_BLOCK
    row_tok = jnp.full((P,), T, dtype=jnp.int32).at[dest].set(tok_of.astype(jnp.int32))
    row_w = jnp.zeros((P,), jnp.float32).at[dest].set(gate.reshape(A)[order])
    block_e = jnp.minimum(jnp.searchsorted(pend, jnp.arange(n_blocks) * MOE_BLOCK, side='right'), N_EXPERTS - 1)
    x_pad = jnp.concatenate([x, jnp.zeros((1, D), x.dtype)], axis=0)

    def expert_block(args):
        rows, e = args
        hgu = x_pad[rows] @ w_gate_up[e] + b_gate_up[e]
        return (clamped_swiglu(hgu) @ w_down[e] + b_down[e]).astype(jnp.float32)

    y_rows = lax.map(expert_block, (row_tok.reshape(n_blocks, MOE_BLOCK), block_e)).reshape(P, D)
    out = jnp.zeros((T + 1, D), jnp.float32).at[row_tok].add(y_rows * row_w[:, None])
    return out[:T].astype(x.dtype)


def setup_inputs(seed: int = 0) -> dict:
    key = jax.random.key(seed)
    ks = jax.random.split(key, 24)
    f32 = jnp.float32

    def nrm(k, shape, scale):
        return jax.random.normal(k, shape, f32) * scale

    def gain(k, shape):
        return 1.0 + 0.01 * jax.random.normal(k, shape, f32)

    QA = GLA_HEADS * GLA_DK
    VA = GLA_HEADS * GLA_DV
    VB = RET_HEADS * RET_DV
    return {
        'x_prompt': nrm(ks[0], (BATCH, SEQ, D_MODEL), 1.0),
        'x_sample': nrm(ks[1], (DEC_BATCH, DEC_SEQ, D_MODEL), 1.0),
        'state_gla': nrm(ks[2], (DEPTH, DEC_BATCH, GLA_HEADS, GLA_DK, GLA_DV), 0.5),
        'state_ret': nrm(ks[3], (DEPTH, DEC_BATCH, RET_HEADS, RET_DK, RET_DV), 0.5),
        'meta_tokens': nrm(ks[4], (N_META, D_MODEL), 1.0),
        'norm_mix_w': gain(ks[5], (DEPTH, D_MODEL)),
        'w_in': nrm(ks[6], (DEPTH, D_MODEL, IN_WIDTH), D_MODEL ** -0.5),
        'w_alpha_up': nrm(ks[7], (DEPTH, GLA_RANK, QA), GLA_RANK ** -0.5),
        'b_alpha': nrm(ks[8], (DEPTH, QA), 0.1),
        'gla_norm_w': gain(ks[9], (DEPTH, GLA_DV)),
        'ret_norm_w': gain(ks[10], (DEPTH, RET_DV)),
        'w_branch_gla': nrm(ks[11], (DEPTH, VA, D_MODEL), VA ** -0.5),
        'w_branch_ret': nrm(ks[12], (DEPTH, VB, D_MODEL), VB ** -0.5),
        'w_out': nrm(ks[13], (DEPTH, D_MODEL, D_MODEL), D_MODEL ** -0.5),
        'norm_ffn_w': gain(ks[14], (DEPTH, D_MODEL)),
        'w_router': nrm(ks[15], (DEPTH, D_MODEL, N_EXPERTS), D_MODEL ** -0.5),
        'b_router': nrm(ks[16], (DEPTH, N_EXPERTS), 0.01),
        'w_gate_up': nrm(ks[17], (DEPTH, N_EXPERTS, D_MODEL, 2 * D_FF), D_MODEL ** -0.5),
        'b_gate_up': nrm(ks[18], (DEPTH, N_EXPERTS, 2 * D_FF), 0.01),
        'w_down': nrm(ks[19], (DEPTH, N_EXPERTS, D_FF, D_MODEL), D_FF ** -0.5),
        'b_down': nrm(ks[20], (DEPTH, N_EXPERTS, D_MODEL), 0.01),
        'norm_final_w': gain(ks[21], (D_MODEL,)),
    }


def reference(x_prompt, x_sample, state_gla, state_ret, meta_tokens, norm_mix_w, w_in, w_alpha_up, b_alpha,
              gla_norm_w, ret_norm_w, w_branch_gla, w_branch_ret, w_out, norm_ffn_w, w_router, b_router,
              w_gate_up, b_gate_up, w_down, b_down, norm_final_w):
    B, S, D = x_prompt.shape
    DB, DS, _ = x_sample.shape
    pos_meta = jnp.arange(N_META)
    pos_prompt = N_META + jnp.arange(S)
    pos_sample = N_META + PAST_LEN + jnp.arange(DS)
    xm = meta_tokens[None].astype(x_prompt.dtype)
    xp, xs = x_prompt, x_sample
    zero_g = jnp.zeros((1, GLA_HEADS, GLA_DK, GLA_DV), jnp.float32)
    zero_r = jnp.zeros((1, RET_HEADS, RET_DK, RET_DV), jnp.float32)
    gla_p, ret_p, gla_s, ret_s = [], [], [], []
    for l in range(DEPTH):
        mix_w = (w_in[l], w_alpha_up[l], b_alpha[l], gla_norm_w[l], ret_norm_w[l],
                 w_branch_gla[l], w_branch_ret[l], w_out[l])
        moe_w = (w_router[l], b_router[l], w_gate_up[l], b_gate_up[l], w_down[l], b_down[l])
        hm = rmsnorm(xm, norm_mix_w[l])
        if l < DEPTH - 1:
            ym, Sg_m, Sr_m = token_mixers(hm, pos_meta, zero_g, zero_r, N_META, *mix_w)
            xm = xm + ym
            xm = xm + moe(rmsnorm(xm, norm_ffn_w[l]).reshape(N_META, D), *moe_w).reshape(1, N_META, D)
        else:
            Sg_m, Sr_m = meta_states(hm, pos_meta, zero_g, zero_r, w_in[l], w_alpha_up[l], b_alpha[l])
        Sg0 = jnp.broadcast_to(Sg_m, (B,) + Sg_m.shape[1:])
        Sr0 = jnp.broadcast_to(Sr_m, (B,) + Sr_m.shape[1:])
        yp, Sg_p, Sr_p = token_mixers(rmsnorm(xp, norm_mix_w[l]), pos_prompt, Sg0, Sr0, CHUNK, *mix_w)
        ys, Sg_s, Sr_s = token_mixers(rmsnorm(xs, norm_mix_w[l]), pos_sample,
                                      state_gla[l].astype(jnp.float32), state_ret[l].astype(jnp.float32), DS, *mix_w)
        xp = xp + yp
        xs = xs + ys
        tokens = jnp.concatenate([rmsnorm(xp, norm_ffn_w[l]).reshape(B * S, D),
                                  rmsnorm(xs, norm_ffn_w[l]).reshape(DB * DS, D)], axis=0)
        f = moe(tokens, *moe_w)
        xp = xp + f[: B * S].reshape(B, S, D)
        xs = xs + f[B * S:].reshape(DB, DS, D)
        gla_p.append(Sg_p.astype(state_gla.dtype))
        ret_p.append(Sr_p.astype(state_ret.dtype))
        gla_s.append(Sg_s.astype(state_gla.dtype))
        ret_s.append(Sr_s.astype(state_ret.dtype))
    y_prompt = rmsnorm(xp, norm_final_w)
    y_sample = rmsnorm(xs, norm_final_w)
    return (y_prompt, y_sample, jnp.stack(gla_p), jnp.stack(ret_p), jnp.stack(gla_s), jnp.stack(ret_s))
```

```python
import functools
import math

import jax
import jax.numpy as jnp
from jax import lax
from jax.experimental import pallas as pl
from jax.experimental.pallas import tpu as pltpu

F32 = jnp.float32
BF16 = jnp.bfloat16

D_MODEL = 1024
CHUNK = 64
N_META = 16
PAST_LEN = 4096
EPS = 1e-5
HEADS = 4
DK = 128
DV = 256
GLA_RANK = 16
GLA_TAU = 16.0
ROPE_BASE = 10000.0
N_EXPERTS = 32
TOP_K = 4
D_FF = 1024
SWIGLU_ALPHA = 1.702
SWIGLU_LIMIT = 7.0
IN_WIDTHS = (HEADS * DK, HEADS * DK, HEADS * DV, HEADS * DV, GLA_RANK,
             HEADS * DK, HEADS * DK, HEADS * DV, HEADS * DV, 2 * D_MODEL)

LANES = 128
QA, KA, VA, GA, QB, KB, VB, GB, RA = 0, 512, 1024, 2048, 3072, 3584, 4096, 5120, 6144
ZS_W = RA + LANES
ZM_W = 2 * D_MODEL

ROW_TILE = 512
DISPATCH_TILE = 256
COMBINE_TILE = 128
EXPERT_BLOCK = 512
VMEM_LIMIT = 56 * 1024 * 1024


def _params(n_axes):
    return pltpu.CompilerParams(dimension_semantics=("arbitrary",) * n_axes, vmem_limit_bytes=VMEM_LIMIT)


def _rms(x, w):
    return x * lax.rsqrt(jnp.mean(x * x, axis=-1, keepdims=True) + EPS) * w


def _dot(a, b):
    return jnp.dot(a, b, preferred_element_type=F32)


def _split3(x):
    hi = x.astype(BF16)
    r1 = x - hi.astype(F32)
    mid = r1.astype(BF16)
    lo = (r1 - mid.astype(F32)).astype(BF16)
    return hi, mid, lo


def _inproj_kernel(x_ref, nw_ref, ws_ref, wm_ref, zs_ref, zm_ref):
    h = _rms(x_ref[...], nw_ref[...]).astype(BF16)
    step = 512
    for c0 in range(0, ZS_W, step):
        c1 = min(c0 + step, ZS_W)
        zs_ref[:, c0:c1] = _dot(h, ws_ref[:, c0:c1]).astype(BF16)
    for c0 in range(0, ZM_W, step):
        zm_ref[:, c0:c0 + step] = _dot(h, wm_ref[:, c0:c0 + step]).astype(BF16)


def _inproj(x, nw, ws, wm):
    T = x.shape[0]
    tm = min(ROW_TILE, T)
    assert T % tm == 0
    const = dict(pipeline_mode=pl.Buffered(1))
    return pl.pallas_call(
        _inproj_kernel,
        grid=(T // tm,),
        in_specs=[pl.BlockSpec((tm, D_MODEL), lambda i: (i, 0)),
                  pl.BlockSpec((1, D_MODEL), lambda i: (0, 0)),
                  pl.BlockSpec((D_MODEL, ZS_W), lambda i: (0, 0), **const),
                  pl.BlockSpec((D_MODEL, ZM_W), lambda i: (0, 0), **const)],
        out_specs=[pl.BlockSpec((tm, ZS_W), lambda i: (i, 0)),
                   pl.BlockSpec((tm, ZM_W), lambda i: (i, 0))],
        out_shape=[jax.ShapeDtypeStruct((T, ZS_W), BF16), jax.ShapeDtypeStruct((T, ZM_W), BF16)],
        compiler_params=_params(1),
        name="inproj",
    )(x, nw, ws, wm)


def _log_sigmoid(x):
    return jnp.minimum(x, 0.0) - jnp.log1p(jnp.exp(-jnp.abs(x)))


def _scan_kernel(z_ref, cs_ref, sn_ref, sg0_ref, sr0_ref, wup_ref, ba_ref, gnw_ref, rnw_ref,
                 ua_ref, ub_ref, sg_out_ref, sr_out_ref, sg_scr, sr_scr):
    c = pl.program_id(1)
    L = CHUNK

    @pl.when(c == 0)
    def _():
        sg_scr[...] = sg0_ref[0]
        sr_scr[...] = sr0_ref[0]

    row = lax.broadcasted_iota(jnp.int32, (L, L), 0)
    col = lax.broadcasted_iota(jnp.int32, (L, L), 1)
    causal = row >= col
    tril = causal.astype(BF16)

    la = _log_sigmoid(_dot(z_ref[:, RA:RA + LANES], wup_ref[...]) + ba_ref[...]) * (1.0 / GLA_TAU)
    hi, mid, lo = _split3(la)
    b_all = _dot(tril, hi) + _dot(tril, mid) + _dot(tril, lo)

    for h in range(HEADS):
        bh = b_all[:, h * DK:(h + 1) * DK]
        q = z_ref[:, QA + h * DK:QA + (h + 1) * DK].astype(F32) * (DK ** -0.5)
        k = z_ref[:, KA + h * DK:KA + (h + 1) * DK].astype(F32)
        v = z_ref[:, VA + h * DV:VA + (h + 1) * DV]
        qt = (q * jnp.exp(bh)).astype(BF16)
        kT = k.T
        bT = bh.T
        blT = bT[:, L - 1:L]
        ktT = (kT * jnp.exp(-bT)).astype(BF16)
        kdT = (kT * jnp.exp(blT - bT)).astype(BF16)
        s = jnp.where(causal, _dot(qt, ktT), 0.0).astype(BF16)
        s_old = sg_scr[h]
        o = _dot(s, v) + _dot(qt, s_old.astype(BF16))
        sg_scr[h] = jnp.exp(blT) * s_old + _dot(kdT, v)
        g = z_ref[:, GA + h * DV:GA + (h + 1) * DV].astype(F32)
        ua_ref[:, h * DV:(h + 1) * DV] = (_rms(o, gnw_ref[...]) * (g * jax.nn.sigmoid(g))).astype(BF16)

    cs = cs_ref[...]
    sn = sn_ref[...]
    rel = (row - col).astype(F32)
    t_col = lax.broadcasted_iota(jnp.int32, (L, 1), 0).astype(F32)
    s_row = lax.broadcasted_iota(jnp.int32, (1, L), 1).astype(F32)
    for h in range(HEADS):
        lg = math.log(1.0 - 2.0 ** (-5.0 - h))
        q = z_ref[:, QB + h * DK:QB + (h + 1) * DK].astype(F32)
        k = z_ref[:, KB + h * DK:KB + (h + 1) * DK].astype(F32)
        v = z_ref[:, VB + h * DV:VB + (h + 1) * DV]
        qr = (q * cs + pltpu.roll(q, DK // 2, axis=1) * sn).astype(BF16)
        kr = (k * cs + pltpu.roll(k, DK // 2, axis=1) * sn) * (DK ** -0.5)
        krT = kr.T
        decay = jnp.exp(jnp.where(causal, lg * rel, -jnp.inf))
        s = (_dot(qr, krT.astype(BF16)) * decay).astype(BF16)
        s_old = sr_scr[h]
        inner = jnp.exp(lg * (t_col + 1.0))
        o = _dot(s, v) + inner * _dot(qr, s_old.astype(BF16))
        kdT = (krT * jnp.exp(lg * (L - 1.0 - s_row))).astype(BF16)
        sr_scr[h] = math.exp(lg * L) * s_old + _dot(kdT, v)
        g = z_ref[:, GB + h * DV:GB + (h + 1) * DV].astype(F32)
        ub_ref[:, h * DV:(h + 1) * DV] = (_rms(o, rnw_ref[...]) * (g * jax.nn.sigmoid(g))).astype(BF16)

    @pl.when(c == pl.num_programs(1) - 1)
    def _():
        sg_out_ref[0] = sg_scr[...]
        sr_out_ref[0] = sr_scr[...]


def _scan(zs, cs, sn, sg0, sr0, wup, ba, gnw, rnw, batch):
    T = zs.shape[0]
    n_chunks = T // (batch * CHUNK)
    state_spec = pl.BlockSpec((1, HEADS, DK, DV), lambda b, c: (b, 0, 0, 0))
    full = lambda shape: pl.BlockSpec(shape, lambda b, c: (0,) * len(shape))
    return pl.pallas_call(
        _scan_kernel,
        grid=(batch, n_chunks),
        in_specs=[pl.BlockSpec((CHUNK, ZS_W), lambda b, c: (b * n_chunks + c, 0)),
                  pl.BlockSpec((CHUNK, DK), lambda b, c: (c, 0)),
                  pl.BlockSpec((CHUNK, DK), lambda b, c: (c, 0)),
                  state_spec, state_spec,
                  full((LANES, HEADS * DK)), full((1, HEADS * DK)), full((1, DV)), full((1, DV))],
        out_specs=[pl.BlockSpec((CHUNK, HEADS * DV), lambda b, c: (b * n_chunks + c, 0)),
                   pl.BlockSpec((CHUNK, HEADS * DV), lambda b, c: (b * n_chunks + c, 0)),
                   state_spec, state_spec],
        out_shape=[jax.ShapeDtypeStruct((T, HEADS * DV), BF16), jax.ShapeDtypeStruct((T, HEADS * DV), BF16),
                   jax.ShapeDtypeStruct((batch, HEADS, DK, DV), F32),
                   jax.ShapeDtypeStruct((batch, HEADS, DK, DV), F32)],
        scratch_shapes=[pltpu.VMEM((HEADS, DK, DV), F32), pltpu.VMEM((HEADS, DK, DV), F32)],
        compiler_params=_params(2),
        name="scan",
    )(zs, cs, sn, sg0, sr0, wup, ba, gnw, rnw)


def _post_kernel(ua_ref, ub_ref, zm_ref, x_ref, wa_ref, wb_ref, wo_ref, nfw_ref, wr_ref, br_ref,
                 x1_ref, e_ref, g_ref):
    a = _dot(ua_ref[...], wa_ref[...])
    b = _dot(ub_ref[...], wb_ref[...])
    ma = zm_ref[:, :D_MODEL].astype(F32)
    mb = zm_ref[:, D_MODEL:].astype(F32)
    merged = jax.nn.sigmoid(ma) * a + jax.nn.sigmoid(mb) * b
    x1 = x_ref[...] + _dot(merged.astype(BF16), wo_ref[...])
    x1_ref[...] = x1

    h2 = _rms(x1, nfw_ref[...])
    h_hi, h_mid, _ = _split3(h2)
    w_hi, w_mid = wr_ref[0], wr_ref[1]
    logits = _dot(h_hi, w_hi) + _dot(h_hi, w_mid) + _dot(h_mid, w_hi) + br_ref[...]
    lane = lax.broadcasted_iota(jnp.int32, logits.shape, 1)
    lane_f = lane.astype(F32)
    work = jnp.where(lane < N_EXPERTS, logits, -jnp.inf)
    vals, idxs = [], []
    for _ in range(TOP_K):
        m = jnp.max(work, axis=1, keepdims=True)
        idx = jnp.min(jnp.where(work == m, lane_f, float(LANES)), axis=1, keepdims=True)
        vals.append(m)
        idxs.append(idx)
        work = jnp.where(lane_f == idx, -jnp.inf, work)
    ps = [jnp.exp(v - vals[0]) for v in vals]
    den = ps[0] + ps[1] + ps[2] + ps[3]
    e_out = jnp.zeros(logits.shape, F32)
    g_out = jnp.zeros(logits.shape, F32)
    for k in range(TOP_K):
        e_out = jnp.where(lane == k, idxs[k], e_out)
        g_out = jnp.where(lane == k, ps[k] / den, g_out)
    e_ref[...] = e_out.astype(jnp.int32)
    g_ref[...] = g_out


def _post(ua, ub, zm, x, wa, wb, wo, nfw, wr, br):
    T = x.shape[0]
    tm = min(ROW_TILE, T)
    assert T % tm == 0
    rows = lambda w: pl.BlockSpec((tm, w), lambda i: (i, 0))
    full = lambda shape: pl.BlockSpec(shape, lambda i: (0,) * len(shape))
    return pl.pallas_call(
        _post_kernel,
        grid=(T // tm,),
        in_specs=[rows(HEADS * DV), rows(HEADS * DV), rows(ZM_W), rows(D_MODEL),
                  full((HEADS * DV, D_MODEL)), full((HEADS * DV, D_MODEL)), full((D_MODEL, D_MODEL)),
                  full((1, D_MODEL)), full((2, D_MODEL, LANES)), full((1, LANES))],
        out_specs=[rows(D_MODEL), rows(LANES), rows(LANES)],
        out_shape=[jax.ShapeDtypeStruct((T, D_MODEL), F32), jax.ShapeDtypeStruct((T, LANES), jnp.int32),
                   jax.ShapeDtypeStruct((T, LANES), F32)],
        compiler_params=_params(1),
        name="post",
    )(ua, ub, zm, x, wa, wb, wo, nfw, wr, br)


def _row_copy_wait(src_rows, dst_rows, sem):
    pltpu.make_async_copy(src_rows, dst_rows, sem).wait()


def _dispatch_kernel(dest_ref, x1_ref, nfw_ref, xs_in_ref, xs_ref, hbuf, sem):
    del xs_in_ref
    tm = hbuf.shape[0]
    hbuf[...] = _rms(x1_ref[...], nfw_ref[...])

    def issue(r, carry):
        for k in range(TOP_K):
            d = dest_ref[0, 0, r * TOP_K + k]
            pltpu.make_async_copy(hbuf.at[pl.ds(r, 1)], xs_ref.at[pl.ds(d, 1)], sem).start()
        return carry

    lax.fori_loop(0, tm, issue, 0)
    for _ in range(TOP_K):
        _row_copy_wait(hbuf, xs_ref.at[pl.ds(0, tm)], sem)


def _dispatch(dest, x1, nfw, xs):
    T = x1.shape[0]
    tm = min(DISPATCH_TILE, T)
    assert T % tm == 0
    return pl.pallas_call(
        _dispatch_kernel,
        grid=(T // tm,),
        in_specs=[pl.BlockSpec((1, 1, tm * TOP_K), lambda i: (i, 0, 0), memory_space=pltpu.SMEM),
                  pl.BlockSpec((tm, D_MODEL), lambda i: (i, 0)),
                  pl.BlockSpec((1, D_MODEL), lambda i: (0, 0)),
                  pl.BlockSpec(memory_space=pl.ANY)],
        out_specs=pl.BlockSpec(memory_space=pl.ANY),
        out_shape=jax.ShapeDtypeStruct(xs.shape, xs.dtype),
        scratch_shapes=[pltpu.VMEM((tm, D_MODEL), F32), pltpu.SemaphoreType.DMA(())],
        input_output_aliases={3: 0},
        compiler_params=_params(1),
        name="dispatch",
    )(dest.reshape(T // tm, 1, tm * TOP_K), x1, nfw, xs)


def _expert_kernel(be_ref, nu_ref, xs_ref, wgu_ref, bgu_ref, wd_ref, bd_ref, ys_ref):
    del be_ref
    used = pl.program_id(0) < nu_ref[0]

    @pl.when(jnp.logical_not(used))
    def _():
        ys_ref[...] = jnp.zeros(ys_ref.shape, F32)

    @pl.when(used)
    def _():
        hgu = _dot(xs_ref[...].astype(BF16), wgu_ref[0]) + bgu_ref[0]
        glu = jnp.minimum(hgu[:, :D_FF], SWIGLU_LIMIT)
        lin = jnp.clip(hgu[:, D_FF:], -SWIGLU_LIMIT, SWIGLU_LIMIT)
        act = glu * jax.nn.sigmoid(SWIGLU_ALPHA * glu) * (lin + 1.0)
        ys_ref[...] = _dot(act.astype(BF16), wd_ref[0]) + bd_ref[0]


def _experts(block_e, n_used, xs, wgu, bgu, wd, bd):
    P = xs.shape[0]
    bm = EXPERT_BLOCK
    blk = lambda i, be, nu: (jnp.minimum(i, nu[0] - 1), 0)
    ex3 = lambda i, be, nu: (be[i], 0, 0)
    return pl.pallas_call(
        _expert_kernel,
        grid_spec=pltpu.PrefetchScalarGridSpec(
            num_scalar_prefetch=2,
            grid=(P // bm,),
            in_specs=[pl.BlockSpec((bm, D_MODEL), blk),
                      pl.BlockSpec((1, D_MODEL, 2 * D_FF), ex3),
                      pl.BlockSpec((1, 1, 2 * D_FF), ex3),
                      pl.BlockSpec((1, D_FF, D_MODEL), ex3),
                      pl.BlockSpec((1, 1, D_MODEL), ex3)],
            out_specs=pl.BlockSpec((bm, D_MODEL), lambda i, be, nu: (i, 0))),
        out_shape=jax.ShapeDtypeStruct((P, D_MODEL), F32),
        compiler_params=_params(1),
        name="experts",
    )(block_e, n_used, xs, wgu, bgu, wd, bd)


def _combine_kernel(dest_ref, gate_ref, x1_ref, nw_ref, ys_ref, out_ref, buf, sem):
    tm = x1_ref.shape[0]

    def issue(r, carry):
        for k in range(TOP_K):
            d = dest_ref[0, 0, r * TOP_K + k]
            pltpu.make_async_copy(ys_ref.at[pl.ds(d, 1)], buf.at[k, pl.ds(r, 1)], sem).start()
        return carry

    lax.fori_loop(0, tm, issue, 0)
    for k in range(TOP_K):
        _row_copy_wait(ys_ref.at[pl.ds(0, tm)], buf.at[k], sem)
    gate = gate_ref[...]
    f = gate[:, 0:1] * buf[0]
    for k in range(1, TOP_K):
        f = f + gate[:, k:k + 1] * buf[k]
    out_ref[...] = _rms(x1_ref[...] + f, nw_ref[...])


def _combine(dest, gate, x1, nw, ys):
    T = x1.shape[0]
    tm = min(COMBINE_TILE, T)
    assert T % tm == 0
    return pl.pallas_call(
        _combine_kernel,
        grid=(T // tm,),
        in_specs=[pl.BlockSpec((1, 1, tm * TOP_K), lambda i: (i, 0, 0), memory_space=pltpu.SMEM),
                  pl.BlockSpec((tm, LANES), lambda i: (i, 0)),
                  pl.BlockSpec((tm, D_MODEL), lambda i: (i, 0)),
                  pl.BlockSpec((1, D_MODEL), lambda i: (0, 0)),
                  pl.BlockSpec(memory_space=pl.ANY)],
        out_specs=pl.BlockSpec((tm, D_MODEL), lambda i: (i, 0)),
        out_shape=jax.ShapeDtypeStruct((T, D_MODEL), F32),
        scratch_shapes=[pltpu.VMEM((TOP_K, tm, D_MODEL), F32), pltpu.SemaphoreType.DMA(())],
        compiler_params=_params(1),
        name="combine",
    )(dest.reshape(T // tm, 1, tm * TOP_K), gate, x1, nw, ys)


def _rotary_tables(pos):
    inv = ROPE_BASE ** (-jnp.arange(0, DK, 2, dtype=F32) / DK)
    ang = pos.astype(F32)[:, None] * inv[None, :]
    cos, sin = jnp.cos(ang), jnp.sin(ang)
    return jnp.concatenate([cos, cos], axis=1), jnp.concatenate([-sin, sin], axis=1)


def _routing(top_e, n_rows_max):
    bm = EXPERT_BLOCK
    flat_e = top_e.reshape(-1)
    onehot = (flat_e[:, None] == jnp.arange(N_EXPERTS, dtype=jnp.int32)[None, :]).astype(jnp.int32)
    csum = jnp.cumsum(onehot, axis=0)
    rank = jnp.sum(csum * onehot, axis=1) - 1
    counts = csum[-1]
    padded = (counts + bm - 1) // bm * bm
    pend = jnp.cumsum(padded)
    pstart = pend - padded
    dest = (pstart[flat_e] + rank).astype(jnp.int32)
    n_blocks = n_rows_max // bm
    block_e = jnp.minimum(jnp.searchsorted(pend, jnp.arange(n_blocks, dtype=jnp.int32) * bm, side='right'),
                          N_EXPERTS - 1).astype(jnp.int32)
    n_used = (pend[-1:] // bm).astype(jnp.int32)
    return dest, block_e, n_used


def kernel(x_prompt, x_sample, state_gla, state_ret, meta_tokens, norm_mix_w, w_in, w_alpha_up, b_alpha, gla_norm_w, ret_norm_w, w_branch_gla, w_branch_ret, w_out, norm_ffn_w, w_router, b_router, w_gate_up, b_gate_up, w_down, b_down, norm_final_w):
    B, S, D = x_prompt.shape
    DB, DS, _ = x_sample.shape
    assert D == D_MODEL and S % CHUNK == 0 and DS == CHUNK and state_gla.shape[0] == 1
    l = 0

    offs = [0]
    for w in IN_WIDTHS:
        offs.append(offs[-1] + w)
    cols = [w_in[l][:, offs[i]:offs[i + 1]] for i in range(len(IN_WIDTHS))]
    qa, ka, va, ga, ra, qb, kb, vb, gb, m = cols
    ra = jnp.pad(ra, ((0, 0), (0, LANES - GLA_RANK)))
    ws = jnp.concatenate([qa, ka, va, ga, qb, kb, vb, gb, ra], axis=1).astype(BF16)
    wm = m.astype(BF16)
    wup = jnp.pad(w_alpha_up[l], ((0, LANES - GLA_RANK), (0, 0))).astype(BF16)
    ba = b_alpha[l][None, :]
    nmw = norm_mix_w[l][None, :]
    gnw = gla_norm_w[l][None, :]
    rnw = ret_norm_w[l][None, :]

    def mixer(x2d, pos, sg0, sr0, batch):
        zs, zm = _inproj(x2d, nmw, ws, wm)
        cs, sn = _rotary_tables(pos)
        ua, ub, sg, sr = _scan(zs, cs, sn, sg0, sr0, wup, ba, gnw, rnw, batch)
        return ua, ub, zm, sg, sr

    x_meta = jnp.concatenate([jnp.zeros((CHUNK - N_META, D), F32), meta_tokens.astype(F32)], axis=0)
    pos_meta = jnp.maximum(jnp.arange(CHUNK) - (CHUNK - N_META), 0)
    zero_state = jnp.zeros((1, HEADS, DK, DV), F32)
    _, _, _, sg_m, sr_m = mixer(x_meta, pos_meta, zero_state, zero_state, 1)

    xp = x_prompt.reshape(B * S, D)
    xs_ = x_sample.reshape(DB * DS, D)
    sg0 = jnp.broadcast_to(sg_m, (B, HEADS, DK, DV))
    sr0 = jnp.broadcast_to(sr_m, (B, HEADS, DK, DV))
    ua_p, ub_p, zm_p, sg_p, sr_p = mixer(xp, N_META + jnp.arange(S), sg0, sr0, B)
    ua_s, ub_s, zm_s, sg_s, sr_s = mixer(xs_, N_META + PAST_LEN + jnp.arange(DS),
                                         state_gla[l].astype(F32), state_ret[l].astype(F32), DB)

    wa = w_branch_gla[l].astype(BF16)
    wb = w_branch_ret[l].astype(BF16)
    wo = w_out[l].astype(BF16)
    nfw = norm_ffn_w[l][None, :]
    wr = jnp.pad(w_router[l].astype(F32), ((0, 0), (0, LANES - N_EXPERTS)))
    wr_hi = wr.astype(BF16)
    wr_mid = (wr - wr_hi.astype(F32)).astype(BF16)
    wr2 = jnp.stack([wr_hi, wr_mid])
    br = jnp.pad(b_router[l].astype(F32), (0, LANES - N_EXPERTS))[None, :]
    x1_p, e_p, g_p = _post(ua_p, ub_p, zm_p, xp, wa, wb, wo, nfw, wr2, br)
    x1_s, e_s, g_s = _post(ua_s, ub_s, zm_s, xs_, wa, wb, wo, nfw, wr2, br)

    Tp, Ts = B * S, DB * DS
    A = (Tp + Ts) * TOP_K
    bm = EXPERT_BLOCK
    n_rows_max = -(-(A + N_EXPERTS * (bm - 1)) // bm) * bm
    top_e = jnp.concatenate([e_p[:, :TOP_K], e_s[:, :TOP_K]], axis=0)
    dest, block_e, n_used = _routing(top_e, n_rows_max)
    dest_p, dest_s = dest[:Tp * TOP_K], dest[Tp * TOP_K:]

    xs_rows = jnp.zeros((n_rows_max, D), F32)
    xs_rows = _dispatch(dest_p, x1_p, nfw, xs_rows)
    xs_rows = _dispatch(dest_s, x1_s, nfw, xs_rows)
    ys_rows = _experts(block_e, n_used, xs_rows, w_gate_up[l].astype(BF16), b_gate_up[l][:, None, :],
                       w_down[l].astype(BF16), b_down[l][:, None, :])
    nw_final = norm_final_w[None, :]
    y_p = _combine(dest_p, g_p, x1_p, nw_final, ys_rows)
    y_s = _combine(dest_s, g_s, x1_s, nw_final, ys_rows)

    dt = state_gla.dtype
    return (y_p.reshape(B, S, D), y_s.reshape(DB, DS, D),
            sg_p[None].astype(dt), sr_p[None].astype(state_ret.dtype),
            sg_s[None].astype(dt), sr_s[None].astype(state_ret.dtype))
```

```python
import functools
import math

import jax
import jax.numpy as jnp
from jax import lax
from jax.experimental import pallas as pl
from jax.experimental.pallas import tpu as pltpu

F32 = jnp.float32
BF16 = jnp.bfloat16

D_MODEL = 1024
CHUNK = 64
N_META = 16
PAST_LEN = 4096
EPS = 1e-5
HEADS = 4
DK = 128
DV = 256
GLA_RANK = 16
GLA_TAU = 16.0
ROPE_BASE = 10000.0
N_EXPERTS = 32
TOP_K = 4
D_FF = 1024
SWIGLU_ALPHA = 1.702
SWIGLU_LIMIT = 7.0
IN_WIDTHS = (HEADS * DK, HEADS * DK, HEADS * DV, HEADS * DV, GLA_RANK,
             HEADS * DK, HEADS * DK, HEADS * DV, HEADS * DV, 2 * D_MODEL)

LANES = 128
QA, KA, VA, GA, QB, KB, VB, GB, RA = 0, 512, 1024, 2048, 3072, 3584, 4096, 5120, 6144
ZS_W = RA + LANES
ZM_W = 2 * D_MODEL

ROW_TILE = 512
DISPATCH_TILE = 256
COMBINE_TILE = 128
EXPERT_BLOCK = 512
VMEM_LIMIT = 56 * 1024 * 1024


def _params(n_axes):
    return pltpu.CompilerParams(dimension_semantics=("arbitrary",) * n_axes, vmem_limit_bytes=VMEM_LIMIT)


def _rms(x, w):
    return x * lax.rsqrt(jnp.mean(x * x, axis=-1, keepdims=True) + EPS) * w


def _dot(a, b):
    return jnp.dot(a, b, preferred_element_type=F32)


def _split3(x):
    hi = x.astype(BF16)
    r1 = x - hi.astype(F32)
    mid = r1.astype(BF16)
    lo = (r1 - mid.astype(F32)).astype(BF16)
    return hi, mid, lo


def _inproj_kernel(x_ref, nw_ref, ws_ref, wm_ref, zs_ref, zm_ref):
    h = _rms(x_ref[...], nw_ref[...]).astype(BF16)
    step = 512
    for c0 in range(0, ZS_W, step):
        c1 = min(c0 + step, ZS_W)
        zs_ref[:, c0:c1] = _dot(h, ws_ref[:, c0:c1]).astype(BF16)
    for c0 in range(0, ZM_W, step):
        zm_ref[:, c0:c0 + step] = _dot(h, wm_ref[:, c0:c0 + step]).astype(BF16)


def _inproj(x, nw, ws, wm):
    T = x.shape[0]
    tm = min(ROW_TILE, T)
    assert T % tm == 0
    const = dict(pipeline_mode=pl.Buffered(1))
    return pl.pallas_call(
        _inproj_kernel,
        grid=(T // tm,),
        in_specs=[pl.BlockSpec((tm, D_MODEL), lambda i: (i, 0)),
                  pl.BlockSpec((1, D_MODEL), lambda i: (0, 0)),
                  pl.BlockSpec((D_MODEL, ZS_W), lambda i: (0, 0), **const),
                  pl.BlockSpec((D_MODEL, ZM_W), lambda i: (0, 0), **const)],
        out_specs=[pl.BlockSpec((tm, ZS_W), lambda i: (i, 0)),
                   pl.BlockSpec((tm, ZM_W), lambda i: (i, 0))],
        out_shape=[jax.ShapeDtypeStruct((T, ZS_W), BF16), jax.ShapeDtypeStruct((T, ZM_W), BF16)],
        compiler_params=_params(1),
        name="inproj",
    )(x, nw, ws, wm)


def _log_sigmoid(x):
    return jnp.minimum(x, 0.0) - jnp.log1p(jnp.exp(-jnp.abs(x)))


def _scan_kernel(z_ref, cs_ref, sn_ref, sg0_ref, sr0_ref, wup_ref, ba_ref, gnw_ref, rnw_ref,
                 ua_ref, ub_ref, sg_out_ref, sr_out_ref, sg_scr, sr_scr):
    c = pl.program_id(1)
    L = CHUNK

    @pl.when(c == 0)
    def _():
        sg_scr[...] = sg0_ref[0]
        sr_scr[...] = sr0_ref[0]

    row = lax.broadcasted_iota(jnp.int32, (L, L), 0)
    col = lax.broadcasted_iota(jnp.int32, (L, L), 1)
    causal = row >= col
    tril = causal.astype(BF16)

    la = _log_sigmoid(_dot(z_ref[:, RA:RA + LANES], wup_ref[...]) + ba_ref[...]) * (1.0 / GLA_TAU)
    hi, mid, lo = _split3(la)
    b_all = _dot(tril, hi) + _dot(tril, mid) + _dot(tril, lo)

    for h in range(HEADS):
        bh = b_all[:, h * DK:(h + 1) * DK]
        q = z_ref[:, QA + h * DK:QA + (h + 1) * DK].astype(F32) * (DK ** -0.5)
        k = z_ref[:, KA + h * DK:KA + (h + 1) * DK].astype(F32)
        v = z_ref[:, VA + h * DV:VA + (h + 1) * DV]
        qt = (q * jnp.exp(bh)).astype(BF16)
        kT = k.T
        bT = bh.T
        blT = bT[:, L - 1:L]
        ktT = (kT * jnp.exp(-bT)).astype(BF16)
        kdT = (kT * jnp.exp(blT - bT)).astype(BF16)
        s = jnp.where(causal, _dot(qt, ktT), 0.0).astype(BF16)
        s_old = sg_scr[h]
        o = _dot(s, v) + _dot(qt, s_old.astype(BF16))
        sg_scr[h] = jnp.exp(blT) * s_old + _dot(kdT, v)
        g = z_ref[:, GA + h * DV:GA + (h + 1) * DV].astype(F32)
        ua_ref[:, h * DV:(h + 1) * DV] = (_rms(o, gnw_ref[...]) * (g * jax.nn.sigmoid(g))).astype(BF16)

    cs = cs_ref[...]
    sn = sn_ref[...]
    rel = (row - col).astype(F32)
    t_col = lax.broadcasted_iota(jnp.int32, (L, 1), 0).astype(F32)
    s_row = lax.broadcasted_iota(jnp.int32, (1, L), 1).astype(F32)
    for h in range(HEADS):
        lg = math.log(1.0 - 2.0 ** (-5.0 - h))
        q = z_ref[:, QB + h * DK:QB + (h + 1) * DK].astype(F32)
        k = z_ref[:, KB + h * DK:KB + (h + 1) * DK].astype(F32)
        v = z_ref[:, VB + h * DV:VB + (h + 1) * DV]
        qr = (q * cs + pltpu.roll(q, DK // 2, axis=1) * sn).astype(BF16)
        kr = (k * cs + pltpu.roll(k, DK // 2, axis=1) * sn) * (DK ** -0.5)
        krT = kr.T
        decay = jnp.exp(jnp.where(causal, lg * rel, -jnp.inf))
        s = (_dot(qr, krT.astype(BF16)) * decay).astype(BF16)
        s_old = sr_scr[h]
        inner = jnp.exp(lg * (t_col + 1.0))
        o = _dot(s, v) + inner * _dot(qr, s_old.astype(BF16))
        kdT = (krT * jnp.exp(lg * (L - 1.0 - s_row))).astype(BF16)
        sr_scr[h] = math.exp(lg * L) * s_old + _dot(kdT, v)
        g = z_ref[:, GB + h * DV:GB + (h + 1) * DV].astype(F32)
        ub_ref[:, h * DV:(h + 1) * DV] = (_rms(o, rnw_ref[...]) * (g * jax.nn.sigmoid(g))).astype(BF16)

    @pl.when(c == pl.num_programs(1) - 1)
    def _():
        sg_out_ref[0] = sg_scr[...]
        sr_out_ref[0] = sr_scr[...]


def _scan(zs, cs, sn, sg0, sr0, wup, ba, gnw, rnw, batch):
    T = zs.shape[0]
    n_chunks = T // (batch * CHUNK)
    state_spec = pl.BlockSpec((1, HEADS, DK, DV), lambda b, c: (b, 0, 0, 0))
    full = lambda shape: pl.BlockSpec(shape, lambda b, c: (0,) * len(shape))
    return pl.pallas_call(
        _scan_kernel,
        grid=(batch, n_chunks),
        in_specs=[pl.BlockSpec((CHUNK, ZS_W), lambda b, c: (b * n_chunks + c, 0)),
                  pl.BlockSpec((CHUNK, DK), lambda b, c: (c, 0)),
                  pl.BlockSpec((CHUNK, DK), lambda b, c: (c, 0)),
                  state_spec, state_spec,
                  full((LANES, HEADS * DK)), full((1, HEADS * DK)), full((1, DV)), full((1, DV))],
        out_specs=[pl.BlockSpec((CHUNK, HEADS * DV), lambda b, c: (b * n_chunks + c, 0)),
                   pl.BlockSpec((CHUNK, HEADS * DV), lambda b, c: (b * n_chunks + c, 0)),
                   state_spec, state_spec],
        out_shape=[jax.ShapeDtypeStruct((T, HEADS * DV), BF16), jax.ShapeDtypeStruct((T, HEADS * DV), BF16),
                   jax.ShapeDtypeStruct((batch, HEADS, DK, DV), F32),
                   jax.ShapeDtypeStruct((batch, HEADS, DK, DV), F32)],
        scratch_shapes=[pltpu.VMEM((HEADS, DK, DV), F32), pltpu.VMEM((HEADS, DK, DV), F32)],
        compiler_params=_params(2),
        name="scan",
    )(zs, cs, sn, sg0, sr0, wup, ba, gnw, rnw)


def _post_kernel(ua_ref, ub_ref, zm_ref, x_ref, wa_ref, wb_ref, wo_ref, nfw_ref, wr_ref, br_ref, cnt0_ref,
                 x1_ref, e_ref, g_ref, rank_ref, cnt_ref):
    @pl.when(pl.program_id(0) == 0)
    def _():
        cnt_ref[...] = cnt0_ref[...]

    a = _dot(ua_ref[...], wa_ref[...])
    b = _dot(ub_ref[...], wb_ref[...])
    ma = zm_ref[:, :D_MODEL].astype(F32)
    mb = zm_ref[:, D_MODEL:].astype(F32)
    merged = jax.nn.sigmoid(ma) * a + jax.nn.sigmoid(mb) * b
    x1 = x_ref[...] + _dot(merged.astype(BF16), wo_ref[...])
    x1_ref[...] = x1

    h2 = _rms(x1, nfw_ref[...])
    h_hi, h_mid, _ = _split3(h2)
    w_hi, w_mid = wr_ref[0], wr_ref[1]
    logits = _dot(h_hi, w_hi) + _dot(h_hi, w_mid) + _dot(h_mid, w_hi) + br_ref[...]
    lane = lax.broadcasted_iota(jnp.int32, logits.shape, 1)
    lane_f = lane.astype(F32)
    work = jnp.where(lane < N_EXPERTS, logits, -jnp.inf)
    vals, idxs = [], []
    for _ in range(TOP_K):
        m = jnp.max(work, axis=1, keepdims=True)
        idx = jnp.min(jnp.where(work == m, lane_f, float(LANES)), axis=1, keepdims=True)
        vals.append(m)
        idxs.append(idx)
        work = jnp.where(lane_f == idx, -jnp.inf, work)
    ps = [jnp.exp(v - vals[0]) for v in vals]
    den = ps[0] + ps[1] + ps[2] + ps[3]
    e_out = jnp.zeros(logits.shape, F32)
    g_out = jnp.zeros(logits.shape, F32)
    for k in range(TOP_K):
        e_out = jnp.where(lane == k, idxs[k], e_out)
        g_out = jnp.where(lane == k, ps[k] / den, g_out)
    e_ref[...] = e_out.astype(jnp.int32)
    g_ref[...] = g_out

    tm = logits.shape[0]
    multi_hot = jnp.zeros(logits.shape, F32)
    for k in range(TOP_K):
        multi_hot = multi_hot + (lane_f == idxs[k]).astype(F32)
    earlier = (lax.broadcasted_iota(jnp.int32, (tm, tm), 0) > lax.broadcasted_iota(jnp.int32, (tm, tm), 1))
    before = _dot(earlier.astype(BF16), multi_hot.astype(BF16)) + cnt_ref[...]
    r_out = jnp.zeros(logits.shape, F32)
    for k in range(TOP_K):
        rk = jnp.sum(jnp.where(lane_f == idxs[k], before, 0.0), axis=1, keepdims=True)
        r_out = jnp.where(lane == k, rk, r_out)
    rank_ref[...] = r_out.astype(jnp.int32)
    cnt_ref[...] += jnp.sum(multi_hot, axis=0, keepdims=True)


def _post(ua, ub, zm, x, wa, wb, wo, nfw, wr, br, cnt0):
    T = x.shape[0]
    tm = min(ROW_TILE, T)
    assert T % tm == 0
    rows = lambda w: pl.BlockSpec((tm, w), lambda i: (i, 0))
    full = lambda shape: pl.BlockSpec(shape, lambda i: (0,) * len(shape))
    return pl.pallas_call(
        _post_kernel,
        grid=(T // tm,),
        in_specs=[rows(HEADS * DV), rows(HEADS * DV), rows(ZM_W), rows(D_MODEL),
                  full((HEADS * DV, D_MODEL)), full((HEADS * DV, D_MODEL)), full((D_MODEL, D_MODEL)),
                  full((1, D_MODEL)), full((2, D_MODEL, LANES)), full((1, LANES)), full((1, LANES))],
        out_specs=[rows(D_MODEL), rows(LANES), rows(LANES), rows(LANES), full((1, LANES))],
        out_shape=[jax.ShapeDtypeStruct((T, D_MODEL), F32), jax.ShapeDtypeStruct((T, LANES), jnp.int32),
                   jax.ShapeDtypeStruct((T, LANES), F32), jax.ShapeDtypeStruct((T, LANES), jnp.int32),
                   jax.ShapeDtypeStruct((1, LANES), F32)],
        compiler_params=_params(1),
        name="post",
    )(ua, ub, zm, x, wa, wb, wo, nfw, wr, br, cnt0)


def _row_copy_wait(src_rows, dst_rows, sem):
    pltpu.make_async_copy(src_rows, dst_rows, sem).wait()


def _dispatch_kernel(dest_ref, xp_ref, xs_ref, nfw_ref, rows_ref, hbuf, sem, *, n_prompt_tiles):
    tm = hbuf.shape[0]
    i = pl.program_id(0)

    @pl.when(i < n_prompt_tiles)
    def _():
        hbuf[...] = _rms(xp_ref[...], nfw_ref[...])

    @pl.when(i >= n_prompt_tiles)
    def _():
        hbuf[...] = _rms(xs_ref[...], nfw_ref[...])

    def issue(r, carry):
        for k in range(TOP_K):
            d = dest_ref[0, 0, r * TOP_K + k]
            pltpu.make_async_copy(hbuf.at[pl.ds(r, 1)], rows_ref.at[pl.ds(d, 1)], sem).start(priority=k % 2)
        return carry

    lax.fori_loop(0, tm, issue, 0)
    for _ in range(TOP_K):
        _row_copy_wait(hbuf, rows_ref.at[pl.ds(0, tm)], sem)


def _dispatch(dest, x1_p, x1_s, nfw):
    Tp, Ts = x1_p.shape[0], x1_s.shape[0]
    tm = math.gcd(DISPATCH_TILE, Tp, Ts)
    assert tm % 8 == 0
    n_p, n_s = Tp // tm, Ts // tm
    return pl.pallas_call(
        functools.partial(_dispatch_kernel, n_prompt_tiles=n_p),
        grid=(n_p + n_s,),
        in_specs=[pl.BlockSpec((1, 1, tm * TOP_K), lambda i: (i, 0, 0), memory_space=pltpu.SMEM),
                  pl.BlockSpec((tm, D_MODEL), lambda i: (jnp.minimum(i, n_p - 1), 0)),
                  pl.BlockSpec((tm, D_MODEL), lambda i: (jnp.maximum(i - n_p, 0), 0)),
                  pl.BlockSpec((1, D_MODEL), lambda i: (0, 0))],
        out_specs=pl.BlockSpec(memory_space=pl.ANY),
        out_shape=jax.ShapeDtypeStruct(((Tp + Ts) * TOP_K, D_MODEL), F32),
        scratch_shapes=[pltpu.VMEM((tm, D_MODEL), F32), pltpu.SemaphoreType.DMA(())],
        compiler_params=_params(1),
        name="dispatch",
    )(dest.reshape(n_p + n_s, 1, tm * TOP_K), x1_p, x1_s, nfw)


def _expert_kernel(blk_ref, exp_ref, lo_ref, hi_ref, xs_ref, wgu_ref, bgu_ref, wd_ref, bd_ref, ys_ref):
    del blk_ref, exp_ref
    i = pl.program_id(0)
    lo, hi = lo_ref[i], hi_ref[i]

    @pl.when(hi > lo)
    def _():
        hgu = _dot(xs_ref[...].astype(BF16), wgu_ref[0]) + bgu_ref[0]
        glu = jnp.minimum(hgu[:, :D_FF], SWIGLU_LIMIT)
        lin = jnp.clip(hgu[:, D_FF:], -SWIGLU_LIMIT, SWIGLU_LIMIT)
        act = glu * jax.nn.sigmoid(SWIGLU_ALPHA * glu) * (lin + 1.0)
        y = _dot(act.astype(BF16), wd_ref[0]) + bd_ref[0]

        @pl.when(lo == 0)
        def _():
            ys_ref[...] = y

        @pl.when(lo > 0)
        def _():
            row = lax.broadcasted_iota(jnp.int32, (y.shape[0], 1), 0)
            ys_ref[...] = jnp.where(row >= lo, y, ys_ref[...])


def _experts(items, xs, wgu, bgu, wd, bd):
    P = xs.shape[0]
    bm = EXPERT_BLOCK
    n_items = items[0].shape[0]
    blk = lambda i, b, e, lo, hi: (b[i], 0)
    ex3 = lambda i, b, e, lo, hi: (e[i], 0, 0)
    return pl.pallas_call(
        _expert_kernel,
        grid_spec=pltpu.PrefetchScalarGridSpec(
            num_scalar_prefetch=4,
            grid=(n_items,),
            in_specs=[pl.BlockSpec((bm, D_MODEL), blk),
                      pl.BlockSpec((1, D_MODEL, 2 * D_FF), ex3),
                      pl.BlockSpec((1, 1, 2 * D_FF), ex3),
                      pl.BlockSpec((1, D_FF, D_MODEL), ex3),
                      pl.BlockSpec((1, 1, D_MODEL), ex3)],
            out_specs=pl.BlockSpec((bm, D_MODEL), blk)),
        out_shape=jax.ShapeDtypeStruct((P, D_MODEL), F32),
        compiler_params=_params(1),
        name="experts",
    )(*items, xs, wgu, bgu, wd, bd)


def _combine_kernel(dest_ref, gate_ref, x1_ref, nw_ref, ys_ref, out_ref, buf, sem):
    tm = x1_ref.shape[0]

    def issue(r, carry):
        for k in range(TOP_K):
            d = dest_ref[0, 0, r * TOP_K + k]
            pltpu.make_async_copy(ys_ref.at[pl.ds(d, 1)], buf.at[k, pl.ds(r, 1)], sem).start(priority=k % 2)
        return carry

    lax.fori_loop(0, tm, issue, 0)
    for k in range(TOP_K):
        _row_copy_wait(ys_ref.at[pl.ds(0, tm)], buf.at[k], sem)
    gate = gate_ref[...]
    f = gate[:, 0:1] * buf[0]
    for k in range(1, TOP_K):
        f = f + gate[:, k:k + 1] * buf[k]
    out_ref[...] = _rms(x1_ref[...] + f, nw_ref[...])


def _combine(dest, gate, x1, nw, ys):
    T = x1.shape[0]
    tm = min(COMBINE_TILE, T)
    assert T % tm == 0
    return pl.pallas_call(
        _combine_kernel,
        grid=(T // tm,),
        in_specs=[pl.BlockSpec((1, 1, tm * TOP_K), lambda i: (i, 0, 0), memory_space=pltpu.SMEM),
                  pl.BlockSpec((tm, LANES), lambda i: (i, 0)),
                  pl.BlockSpec((tm, D_MODEL), lambda i: (i, 0)),
                  pl.BlockSpec((1, D_MODEL), lambda i: (0, 0)),
                  pl.BlockSpec(memory_space=pl.ANY)],
        out_specs=pl.BlockSpec((tm, D_MODEL), lambda i: (i, 0)),
        out_shape=jax.ShapeDtypeStruct((T, D_MODEL), F32),
        scratch_shapes=[pltpu.VMEM((TOP_K, tm, D_MODEL), F32), pltpu.SemaphoreType.DMA(())],
        compiler_params=_params(1),
        name="combine",
    )(dest.reshape(T // tm, 1, tm * TOP_K), gate, x1, nw, ys)


def _rotary_tables(pos):
    inv = ROPE_BASE ** (-jnp.arange(0, DK, 2, dtype=F32) / DK)
    ang = pos.astype(F32)[:, None] * inv[None, :]
    cos, sin = jnp.cos(ang), jnp.sin(ang)
    return jnp.concatenate([cos, cos], axis=1), jnp.concatenate([-sin, sin], axis=1)


def _routing(top_e, rank, counts):
    bm = EXPERT_BLOCK
    n_rows = top_e.size
    n_blocks = n_rows // bm
    n_items = n_blocks + N_EXPERTS - 1
    start = jnp.cumsum(counts) - counts
    end = start + counts
    expert_ids = jnp.arange(N_EXPERTS, dtype=jnp.int32)
    onehot = top_e[:, :, None] == expert_ids[None, None, :]
    dest = (jnp.sum(jnp.where(onehot, start[None, None, :], 0), axis=2) + rank).astype(jnp.int32)

    first_blk = start // bm
    n_it = jnp.where(counts > 0, (end - 1) // bm - first_blk + 1, 0)
    it_end = jnp.cumsum(n_it)
    it_start = it_end - n_it
    j = jnp.arange(n_items, dtype=jnp.int32)
    e_j = jnp.minimum(jnp.sum(j[:, None] >= it_end[None, :], axis=1), N_EXPERTS - 1).astype(jnp.int32)
    valid = j < it_end[-1]
    last_e = jnp.max(jnp.where(counts > 0, expert_ids, 0))
    e_j = jnp.where(valid, e_j, last_e)
    blk = jnp.where(valid, first_blk[e_j] + (j - it_start[e_j]), n_blocks - 1)
    lo = jnp.maximum(start[e_j], blk * bm) - blk * bm
    hi = jnp.minimum(end[e_j], (blk + 1) * bm) - blk * bm
    lo = jnp.where(valid, lo, 0)
    hi = jnp.where(valid, hi, 0)
    items = tuple(a.astype(jnp.int32) for a in (blk, e_j, lo, hi))
    return dest, items


def kernel(x_prompt, x_sample, state_gla, state_ret, meta_tokens, norm_mix_w, w_in, w_alpha_up, b_alpha, gla_norm_w, ret_norm_w, w_branch_gla, w_branch_ret, w_out, norm_ffn_w, w_router, b_router, w_gate_up, b_gate_up, w_down, b_down, norm_final_w):
    B, S, D = x_prompt.shape
    DB, DS, _ = x_sample.shape
    assert D == D_MODEL and S % CHUNK == 0 and DS == CHUNK and state_gla.shape[0] == 1
    l = 0

    offs = [0]
    for w in IN_WIDTHS:
        offs.append(offs[-1] + w)
    cols = [w_in[l][:, offs[i]:offs[i + 1]] for i in range(len(IN_WIDTHS))]
    qa, ka, va, ga, ra, qb, kb, vb, gb, m = cols
    ra = jnp.pad(ra, ((0, 0), (0, LANES - GLA_RANK)))
    ws = jnp.concatenate([qa, ka, va, ga, qb, kb, vb, gb, ra], axis=1).astype(BF16)
    wm = m.astype(BF16)
    wup = jnp.pad(w_alpha_up[l], ((0, LANES - GLA_RANK), (0, 0))).astype(BF16)
    ba = b_alpha[l][None, :]
    nmw = norm_mix_w[l][None, :]
    gnw = gla_norm_w[l][None, :]
    rnw = ret_norm_w[l][None, :]

    def mixer(x2d, pos, sg0, sr0, batch):
        zs, zm = _inproj(x2d, nmw, ws, wm)
        cs, sn = _rotary_tables(pos)
        ua, ub, sg, sr = _scan(zs, cs, sn, sg0, sr0, wup, ba, gnw, rnw, batch)
        return ua, ub, zm, sg, sr

    x_meta = jnp.concatenate([jnp.zeros((CHUNK - N_META, D), F32), meta_tokens.astype(F32)], axis=0)
    pos_meta = jnp.maximum(jnp.arange(CHUNK) - (CHUNK - N_META), 0)
    zero_state = jnp.zeros((1, HEADS, DK, DV), F32)
    _, _, _, sg_m, sr_m = mixer(x_meta, pos_meta, zero_state, zero_state, 1)

    xp = x_prompt.reshape(B * S, D)
    xs_ = x_sample.reshape(DB * DS, D)
    sg0 = jnp.broadcast_to(sg_m, (B, HEADS, DK, DV))
    sr0 = jnp.broadcast_to(sr_m, (B, HEADS, DK, DV))
    ua_p, ub_p, zm_p, sg_p, sr_p = mixer(xp, N_META + jnp.arange(S), sg0, sr0, B)
    ua_s, ub_s, zm_s, sg_s, sr_s = mixer(xs_, N_META + PAST_LEN + jnp.arange(DS),
                                         state_gla[l].astype(F32), state_ret[l].astype(F32), DB)

    wa = w_branch_gla[l].astype(BF16)
    wb = w_branch_ret[l].astype(BF16)
    wo = w_out[l].astype(BF16)
    nfw = norm_ffn_w[l][None, :]
    wr = jnp.pad(w_router[l].astype(F32), ((0, 0), (0, LANES - N_EXPERTS)))
    wr_hi = wr.astype(BF16)
    wr_mid = (wr - wr_hi.astype(F32)).astype(BF16)
    wr2 = jnp.stack([wr_hi, wr_mid])
    br = jnp.pad(b_router[l].astype(F32), (0, LANES - N_EXPERTS))[None, :]
    cnt0 = jnp.zeros((1, LANES), F32)
    x1_p, e_p, g_p, r_p, cnt_p = _post(ua_p, ub_p, zm_p, xp, wa, wb, wo, nfw, wr2, br, cnt0)
    x1_s, e_s, g_s, r_s, cnt = _post(ua_s, ub_s, zm_s, xs_, wa, wb, wo, nfw, wr2, br, cnt_p)

    Tp, Ts = B * S, DB * DS
    assert ((Tp + Ts) * TOP_K) % EXPERT_BLOCK == 0
    top_e = jnp.concatenate([e_p[:, :TOP_K], e_s[:, :TOP_K]], axis=0)
    rank = jnp.concatenate([r_p[:, :TOP_K], r_s[:, :TOP_K]], axis=0)
    dest, items = _routing(top_e, rank, cnt[0, :N_EXPERTS].astype(jnp.int32))
    dest = dest.reshape(-1)
    dest_p, dest_s = dest[:Tp * TOP_K], dest[Tp * TOP_K:]

    xs_rows = _dispatch(dest, x1_p, x1_s, nfw)
    ys_rows = _experts(items, xs_rows, w_gate_up[l].astype(BF16), b_gate_up[l][:, None, :],
                       w_down[l].astype(BF16), b_down[l][:, None, :])
    nw_final = norm_final_w[None, :]
    y_p = _combine(dest_p, g_p, x1_p, nw_final, ys_rows)
    y_s = _combine(dest_s, g_s, x1_s, nw_final, ys_rows)

    dt = state_gla.dtype
    return (y_p.reshape(B, S, D), y_s.reshape(DB, DS, D),
            sg_p[None].astype(dt), sr_p[None].astype(state_ret.dtype),
            sg_s[None].astype(dt), sr_s[None].astype(state_ret.dtype))
```

```python
import functools
import math

import jax
import jax.numpy as jnp
from jax import lax
from jax.experimental import pallas as pl
from jax.experimental.pallas import tpu as pltpu

F32 = jnp.float32
BF16 = jnp.bfloat16

D_MODEL = 1024
CHUNK = 64
N_META = 16
PAST_LEN = 4096
EPS = 1e-5
HEADS = 4
DK = 128
DV = 256
GLA_RANK = 16
GLA_TAU = 16.0
ROPE_BASE = 10000.0
N_EXPERTS = 32
TOP_K = 4
D_FF = 1024
SWIGLU_ALPHA = 1.702
SWIGLU_LIMIT = 7.0
IN_WIDTHS = (HEADS * DK, HEADS * DK, HEADS * DV, HEADS * DV, GLA_RANK,
             HEADS * DK, HEADS * DK, HEADS * DV, HEADS * DV, 2 * D_MODEL)

LANES = 128
QA, KA, VA, GA, QB, KB, VB, GB, RA = 0, 512, 1024, 2048, 3072, 3584, 4096, 5120, 6144
ZS_W = RA + LANES
ZM_W = 2 * D_MODEL

ROW_TILE = 512
DISPATCH_TILE = 256
COMBINE_TILE = 128
EXPERT_BLOCK = 512
ISSUE_UNROLL = 4
VMEM_LIMIT = 56 * 1024 * 1024


def _params(n_axes):
    return pltpu.CompilerParams(dimension_semantics=("arbitrary",) * n_axes, vmem_limit_bytes=VMEM_LIMIT)


def _rms(x, w):
    return x * lax.rsqrt(jnp.mean(x * x, axis=-1, keepdims=True) + EPS) * w


def _dot(a, b):
    return jnp.dot(a, b, preferred_element_type=F32)


def _split3(x):
    hi = x.astype(BF16)
    r1 = x - hi.astype(F32)
    mid = r1.astype(BF16)
    lo = (r1 - mid.astype(F32)).astype(BF16)
    return hi, mid, lo


def _inproj_kernel(x_ref, nw_ref, ws_ref, wm_ref, zs_ref, zm_ref):
    h = _rms(x_ref[...], nw_ref[...]).astype(BF16)
    step = 512
    for c0 in range(0, ZS_W, step):
        c1 = min(c0 + step, ZS_W)
        zs_ref[:, c0:c1] = _dot(h, ws_ref[:, c0:c1]).astype(BF16)
    for c0 in range(0, ZM_W, step):
        zm_ref[:, c0:c0 + step] = _dot(h, wm_ref[:, c0:c0 + step]).astype(BF16)


def _inproj(x, nw, ws, wm):
    T = x.shape[0]
    tm = min(ROW_TILE, T)
    assert T % tm == 0
    const = dict(pipeline_mode=pl.Buffered(1))
    return pl.pallas_call(
        _inproj_kernel,
        grid=(T // tm,),
        in_specs=[pl.BlockSpec((tm, D_MODEL), lambda i: (i, 0)),
                  pl.BlockSpec((1, D_MODEL), lambda i: (0, 0)),
                  pl.BlockSpec((D_MODEL, ZS_W), lambda i: (0, 0), **const),
                  pl.BlockSpec((D_MODEL, ZM_W), lambda i: (0, 0), **const)],
        out_specs=[pl.BlockSpec((tm, ZS_W), lambda i: (i, 0)),
                   pl.BlockSpec((tm, ZM_W), lambda i: (i, 0))],
        out_shape=[jax.ShapeDtypeStruct((T, ZS_W), BF16), jax.ShapeDtypeStruct((T, ZM_W), BF16)],
        compiler_params=_params(1),
        name="inproj",
    )(x, nw, ws, wm)


def _log_sigmoid(x):
    return jnp.minimum(x, 0.0) - jnp.log1p(jnp.exp(-jnp.abs(x)))


def _scan_kernel(z_ref, cs_ref, sn_ref, sg0_ref, sr0_ref, wup_ref, ba_ref, gnw_ref, rnw_ref,
                 ua_ref, ub_ref, sg_out_ref, sr_out_ref, sg_scr, sr_scr, *, n_chunks):
    c = pl.program_id(1)
    L = CHUNK

    @pl.when(c == 0)
    def _():
        sg_scr[...] = sg0_ref[0]
        sr_scr[...] = sr0_ref[0]

    row = lax.broadcasted_iota(jnp.int32, (L, L), 0)
    col = lax.broadcasted_iota(jnp.int32, (L, L), 1)
    causal = row >= col
    tril = causal.astype(BF16)

    la = _log_sigmoid(_dot(z_ref[:, RA:RA + LANES], wup_ref[...]) + ba_ref[...]) * (1.0 / GLA_TAU)
    hi, mid, lo = _split3(la)
    b_all = _dot(tril, hi) + _dot(tril, mid) + _dot(tril, lo)

    for h in range(HEADS):
        bh = b_all[:, h * DK:(h + 1) * DK]
        q = z_ref[:, QA + h * DK:QA + (h + 1) * DK].astype(F32) * (DK ** -0.5)
        k = z_ref[:, KA + h * DK:KA + (h + 1) * DK].astype(F32)
        v = z_ref[:, VA + h * DV:VA + (h + 1) * DV]
        qt = (q * jnp.exp(bh)).astype(BF16)
        kT = k.T
        bT = bh.T
        blT = bT[:, L - 1:L]
        ktT = (kT * jnp.exp(-bT)).astype(BF16)
        kdT = (kT * jnp.exp(blT - bT)).astype(BF16)
        s = jnp.where(causal, _dot(qt, ktT), 0.0).astype(BF16)
        s_old = sg_scr[h]
        o = _dot(s, v) + _dot(qt, s_old.astype(BF16))
        sg_scr[h] = jnp.exp(blT) * s_old + _dot(kdT, v)
        g = z_ref[:, GA + h * DV:GA + (h + 1) * DV].astype(F32)
        ua_ref[:, h * DV:(h + 1) * DV] = (_rms(o, gnw_ref[...]) * (g * jax.nn.sigmoid(g))).astype(BF16)

    cs = cs_ref[...]
    sn = sn_ref[...]
    rel = (row - col).astype(F32)
    t_col = lax.broadcasted_iota(jnp.int32, (L, 1), 0).astype(F32)
    s_row = lax.broadcasted_iota(jnp.int32, (1, L), 1).astype(F32)
    for h in range(HEADS):
        lg = math.log(1.0 - 2.0 ** (-5.0 - h))
        q = z_ref[:, QB + h * DK:QB + (h + 1) * DK].astype(F32)
        k = z_ref[:, KB + h * DK:KB + (h + 1) * DK].astype(F32)
        v = z_ref[:, VB + h * DV:VB + (h + 1) * DV]
        qr = (q * cs + pltpu.roll(q, DK // 2, axis=1) * sn).astype(BF16)
        kr = (k * cs + pltpu.roll(k, DK // 2, axis=1) * sn) * (DK ** -0.5)
        krT = kr.T
        decay = jnp.exp(jnp.where(causal, lg * rel, -jnp.inf))
        s = (_dot(qr, krT.astype(BF16)) * decay).astype(BF16)
        s_old = sr_scr[h]
        inner = jnp.exp(lg * (t_col + 1.0))
        o = _dot(s, v) + inner * _dot(qr, s_old.astype(BF16))
        kdT = (krT * jnp.exp(lg * (L - 1.0 - s_row))).astype(BF16)
        sr_scr[h] = math.exp(lg * L) * s_old + _dot(kdT, v)
        g = z_ref[:, GB + h * DV:GB + (h + 1) * DV].astype(F32)
        ub_ref[:, h * DV:(h + 1) * DV] = (_rms(o, rnw_ref[...]) * (g * jax.nn.sigmoid(g))).astype(BF16)

    @pl.when(c == n_chunks - 1)
    def _():
        sg_out_ref[0] = sg_scr[...]
        sr_out_ref[0] = sr_scr[...]


def _scan(zs, cs, sn, sg0, sr0, wup, ba, gnw, rnw, batch):
    T = zs.shape[0]
    n_chunks = T // (batch * CHUNK)
    state_spec = pl.BlockSpec((1, HEADS, DK, DV), lambda b, c: (b, 0, 0, 0))
    full = lambda shape: pl.BlockSpec(shape, lambda b, c: (0,) * len(shape))
    return pl.pallas_call(
        functools.partial(_scan_kernel, n_chunks=n_chunks),
        grid=(batch, n_chunks),
        in_specs=[pl.BlockSpec((CHUNK, ZS_W), lambda b, c: (b * n_chunks + c, 0)),
                  pl.BlockSpec((CHUNK, DK), lambda b, c: (c, 0)),
                  pl.BlockSpec((CHUNK, DK), lambda b, c: (c, 0)),
                  state_spec, state_spec,
                  full((LANES, HEADS * DK)), full((1, HEADS * DK)), full((1, DV)), full((1, DV))],
        out_specs=[pl.BlockSpec((CHUNK, HEADS * DV), lambda b, c: (b * n_chunks + c, 0)),
                   pl.BlockSpec((CHUNK, HEADS * DV), lambda b, c: (b * n_chunks + c, 0)),
                   state_spec, state_spec],
        out_shape=[jax.ShapeDtypeStruct((T, HEADS * DV), BF16), jax.ShapeDtypeStruct((T, HEADS * DV), BF16),
                   jax.ShapeDtypeStruct((batch, HEADS, DK, DV), F32),
                   jax.ShapeDtypeStruct((batch, HEADS, DK, DV), F32)],
        scratch_shapes=[pltpu.VMEM((HEADS, DK, DV), F32), pltpu.VMEM((HEADS, DK, DV), F32)],
        compiler_params=_params(2),
        name="scan",
    )(zs, cs, sn, sg0, sr0, wup, ba, gnw, rnw)


def _post_kernel(ua_ref, ub_ref, zm_ref, x_ref, wa_ref, wb_ref, wo_ref, nfw_ref, wr_ref, br_ref, cnt0_ref,
                 earlier_ref, x1_ref, e_ref, g_ref, rank_ref, cnt_ref):
    @pl.when(pl.program_id(0) == 0)
    def _():
        cnt_ref[...] = cnt0_ref[...]

    a = _dot(ua_ref[...], wa_ref[...])
    b = _dot(ub_ref[...], wb_ref[...])
    ma = zm_ref[:, :D_MODEL].astype(F32)
    mb = zm_ref[:, D_MODEL:].astype(F32)
    merged = jax.nn.sigmoid(ma) * a + jax.nn.sigmoid(mb) * b
    x1 = x_ref[...] + _dot(merged.astype(BF16), wo_ref[...])
    x1_ref[...] = x1

    h2 = _rms(x1, nfw_ref[...])
    h_hi, h_mid, _ = _split3(h2)
    w_hi, w_mid = wr_ref[0], wr_ref[1]
    logits = _dot(h_hi, w_hi) + _dot(h_hi, w_mid) + _dot(h_mid, w_hi) + br_ref[...]
    lane = lax.broadcasted_iota(jnp.int32, logits.shape, 1)
    lane_f = lane.astype(F32)
    work = jnp.where(lane < N_EXPERTS, logits, -jnp.inf)
    vals, idxs = [], []
    for _ in range(TOP_K):
        m = jnp.max(work, axis=1, keepdims=True)
        idx = jnp.min(jnp.where(work == m, lane_f, float(LANES)), axis=1, keepdims=True)
        vals.append(m)
        idxs.append(idx)
        work = jnp.where(lane_f == idx, -jnp.inf, work)
    ps = [jnp.exp(v - vals[0]) for v in vals]
    den = ps[0] + ps[1] + ps[2] + ps[3]
    e_out = jnp.zeros(logits.shape, F32)
    g_out = jnp.zeros(logits.shape, F32)
    for k in range(TOP_K):
        e_out = jnp.where(lane == k, idxs[k], e_out)
        g_out = jnp.where(lane == k, ps[k] / den, g_out)
    e_ref[...] = e_out.astype(jnp.int32)
    g_ref[...] = g_out

    multi_hot = jnp.zeros(logits.shape, F32)
    for k in range(TOP_K):
        multi_hot = multi_hot + (lane_f == idxs[k]).astype(F32)
    before = _dot(earlier_ref[...], multi_hot.astype(BF16)) + cnt_ref[...]
    r_out = jnp.zeros(logits.shape, F32)
    for k in range(TOP_K):
        rk = jnp.sum(jnp.where(lane_f == idxs[k], before, 0.0), axis=1, keepdims=True)
        r_out = jnp.where(lane == k, rk, r_out)
    rank_ref[...] = r_out.astype(jnp.int32)
    cnt_ref[...] += jnp.sum(multi_hot, axis=0, keepdims=True)


def _post(ua, ub, zm, x, wa, wb, wo, nfw, wr, br, cnt0):
    T = x.shape[0]
    tm = min(ROW_TILE, T)
    assert T % tm == 0
    rows = lambda w: pl.BlockSpec((tm, w), lambda i: (i, 0))
    full = lambda shape: pl.BlockSpec(shape, lambda i: (0,) * len(shape))
    earlier = (jnp.arange(tm)[:, None] > jnp.arange(tm)[None, :]).astype(BF16)
    return pl.pallas_call(
        _post_kernel,
        grid=(T // tm,),
        in_specs=[rows(HEADS * DV), rows(HEADS * DV), rows(ZM_W), rows(D_MODEL),
                  full((HEADS * DV, D_MODEL)), full((HEADS * DV, D_MODEL)), full((D_MODEL, D_MODEL)),
                  full((1, D_MODEL)), full((2, D_MODEL, LANES)), full((1, LANES)), full((1, LANES)),
                  full((tm, tm))],
        out_specs=[rows(D_MODEL), rows(LANES), rows(LANES), rows(LANES), full((1, LANES))],
        out_shape=[jax.ShapeDtypeStruct((T, D_MODEL), F32), jax.ShapeDtypeStruct((T, LANES), jnp.int32),
                   jax.ShapeDtypeStruct((T, LANES), F32), jax.ShapeDtypeStruct((T, LANES), jnp.int32),
                   jax.ShapeDtypeStruct((1, LANES), F32)],
        compiler_params=_params(1),
        name="post",
    )(ua, ub, zm, x, wa, wb, wo, nfw, wr, br, cnt0, earlier)


def _row_copy_wait(src_rows, dst_rows, sem):
    pltpu.make_async_copy(src_rows, dst_rows, sem).wait()


def _dispatch_kernel(dest_ref, xp_ref, xs_ref, nfw_ref, rows_ref, hbuf, sem, *, n_prompt_tiles, n_tiles):
    tm = hbuf.shape[1]
    i = pl.program_id(0)
    slot = i % 2

    def wait_slot(s):
        for _ in range(TOP_K):
            _row_copy_wait(hbuf.at[s], rows_ref.at[pl.ds(0, tm)], sem.at[s])

    @pl.when(i >= 2)
    def _():
        wait_slot(slot)

    @pl.when(i < n_prompt_tiles)
    def _():
        hbuf[slot] = _rms(xp_ref[...], nfw_ref[...])

    @pl.when(i >= n_prompt_tiles)
    def _():
        hbuf[slot] = _rms(xs_ref[...], nfw_ref[...])

    def issue(r, carry):
        for k in range(TOP_K):
            d = dest_ref[0, 0, r * TOP_K + k]
            pltpu.make_async_copy(hbuf.at[slot, pl.ds(r, 1)], rows_ref.at[pl.ds(d, 1)], sem.at[slot]).start()
        return carry

    lax.fori_loop(0, tm, issue, 0, unroll=ISSUE_UNROLL)

    @pl.when(i == n_tiles - 1)
    def _():
        @pl.when(i >= 1)
        def _():
            wait_slot(1 - slot)
        wait_slot(slot)


def _dispatch(dest, x1_p, x1_s, nfw):
    Tp, Ts = x1_p.shape[0], x1_s.shape[0]
    tm = math.gcd(DISPATCH_TILE, Tp, Ts)
    assert tm % 8 == 0
    n_p, n_s = Tp // tm, Ts // tm
    return pl.pallas_call(
        functools.partial(_dispatch_kernel, n_prompt_tiles=n_p, n_tiles=n_p + n_s),
        grid=(n_p + n_s,),
        in_specs=[pl.BlockSpec((1, 1, tm * TOP_K), lambda i: (i, 0, 0), memory_space=pltpu.SMEM),
                  pl.BlockSpec((tm, D_MODEL), lambda i: (jnp.minimum(i, n_p - 1), 0)),
                  pl.BlockSpec((tm, D_MODEL), lambda i: (jnp.maximum(i - n_p, 0), 0)),
                  pl.BlockSpec((1, D_MODEL), lambda i: (0, 0))],
        out_specs=pl.BlockSpec(memory_space=pl.ANY),
        out_shape=jax.ShapeDtypeStruct(((Tp + Ts) * TOP_K, D_MODEL), F32),
        scratch_shapes=[pltpu.VMEM((2, tm, D_MODEL), F32), pltpu.SemaphoreType.DMA((2,))],
        compiler_params=_params(1),
        name="dispatch",
    )(dest.reshape(n_p + n_s, 1, tm * TOP_K), x1_p, x1_s, nfw)


def _expert_kernel(blk_ref, exp_ref, lo_ref, hi_ref, xs_ref, wgu_ref, bgu_ref, wd_ref, bd_ref, ys_ref):
    del blk_ref, exp_ref
    i = pl.program_id(0)
    lo, hi = lo_ref[i], hi_ref[i]

    @pl.when(hi > lo)
    def _():
        hgu = _dot(xs_ref[...].astype(BF16), wgu_ref[0]) + bgu_ref[0]
        glu = jnp.minimum(hgu[:, :D_FF], SWIGLU_LIMIT)
        lin = jnp.clip(hgu[:, D_FF:], -SWIGLU_LIMIT, SWIGLU_LIMIT)
        act = glu * jax.nn.sigmoid(SWIGLU_ALPHA * glu) * (lin + 1.0)
        y = _dot(act.astype(BF16), wd_ref[0]) + bd_ref[0]

        @pl.when(lo == 0)
        def _():
            ys_ref[...] = y

        @pl.when(lo > 0)
        def _():
            row = lax.broadcasted_iota(jnp.int32, (y.shape[0], 1), 0)
            ys_ref[...] = jnp.where(row >= lo, y, ys_ref[...])


def _experts(items, xs, wgu, bgu, wd, bd):
    P = xs.shape[0]
    bm = EXPERT_BLOCK
    n_items = items[0].shape[0]
    blk = lambda i, b, e, lo, hi: (b[i], 0)
    ex3 = lambda i, b, e, lo, hi: (e[i], 0, 0)
    return pl.pallas_call(
        _expert_kernel,
        grid_spec=pltpu.PrefetchScalarGridSpec(
            num_scalar_prefetch=4,
            grid=(n_items,),
            in_specs=[pl.BlockSpec((bm, D_MODEL), blk),
                      pl.BlockSpec((1, D_MODEL, 2 * D_FF), ex3),
                      pl.BlockSpec((1, 1, 2 * D_FF), ex3),
                      pl.BlockSpec((1, D_FF, D_MODEL), ex3),
                      pl.BlockSpec((1, 1, D_MODEL), ex3)],
            out_specs=pl.BlockSpec((bm, D_MODEL), blk)),
        out_shape=jax.ShapeDtypeStruct((P, D_MODEL), F32),
        compiler_params=_params(1),
        name="experts",
    )(*items, xs, wgu, bgu, wd, bd)


def _combine_kernel(dest_ref, dest_next_ref, gate_ref, x1_ref, nw_ref, ys_ref, out_ref, buf, sem, *, n_tiles):
    tm = x1_ref.shape[0]
    i = pl.program_id(0)
    slot = i % 2

    def issue_tile(idx_ref, s):
        def issue(r, carry):
            for k in range(TOP_K):
                d = idx_ref[0, 0, r * TOP_K + k]
                pltpu.make_async_copy(ys_ref.at[pl.ds(d, 1)], buf.at[s, k, pl.ds(r, 1)], sem.at[s]).start()
            return carry

        lax.fori_loop(0, tm, issue, 0, unroll=ISSUE_UNROLL)

    @pl.when(i == 0)
    def _():
        issue_tile(dest_ref, slot)

    @pl.when(i + 1 < n_tiles)
    def _():
        issue_tile(dest_next_ref, 1 - slot)

    for k in range(TOP_K):
        _row_copy_wait(ys_ref.at[pl.ds(0, tm)], buf.at[slot, k], sem.at[slot])
    gate = gate_ref[...]
    f = gate[:, 0:1] * buf[slot, 0]
    for k in range(1, TOP_K):
        f = f + gate[:, k:k + 1] * buf[slot, k]
    out_ref[...] = _rms(x1_ref[...] + f, nw_ref[...])


def _combine(dest, gate, x1, nw, ys):
    T = x1.shape[0]
    tm = min(COMBINE_TILE, T)
    assert T % tm == 0
    n = T // tm
    dest3 = dest.reshape(n, 1, tm * TOP_K)
    return pl.pallas_call(
        functools.partial(_combine_kernel, n_tiles=n),
        grid=(n,),
        in_specs=[pl.BlockSpec((1, 1, tm * TOP_K), lambda i: (i, 0, 0), memory_space=pltpu.SMEM),
                  pl.BlockSpec((1, 1, tm * TOP_K), lambda i: (jnp.minimum(i + 1, n - 1), 0, 0),
                               memory_space=pltpu.SMEM),
                  pl.BlockSpec((tm, LANES), lambda i: (i, 0)),
                  pl.BlockSpec((tm, D_MODEL), lambda i: (i, 0)),
                  pl.BlockSpec((1, D_MODEL), lambda i: (0, 0)),
                  pl.BlockSpec(memory_space=pl.ANY)],
        out_specs=pl.BlockSpec((tm, D_MODEL), lambda i: (i, 0)),
        out_shape=jax.ShapeDtypeStruct((T, D_MODEL), F32),
        scratch_shapes=[pltpu.VMEM((2, TOP_K, tm, D_MODEL), F32), pltpu.SemaphoreType.DMA((2,))],
        compiler_params=_params(1),
        name="combine",
    )(dest3, dest3, gate, x1, nw, ys)


def _rotary_tables(pos):
    inv = ROPE_BASE ** (-jnp.arange(0, DK, 2, dtype=F32) / DK)
    ang = pos.astype(F32)[:, None] * inv[None, :]
    cos, sin = jnp.cos(ang), jnp.sin(ang)
    return jnp.concatenate([cos, cos], axis=1), jnp.concatenate([-sin, sin], axis=1)


def _routing(top_e, rank, counts):
    bm = EXPERT_BLOCK
    n_rows = top_e.size
    n_blocks = n_rows // bm
    n_items = n_blocks + N_EXPERTS - 1
    start = jnp.cumsum(counts) - counts
    end = start + counts
    expert_ids = jnp.arange(N_EXPERTS, dtype=jnp.int32)
    onehot = top_e[:, :, None] == expert_ids[None, None, :]
    dest = (jnp.sum(jnp.where(onehot, start[None, None, :], 0), axis=2) + rank).astype(jnp.int32)

    first_blk = start // bm
    n_it = jnp.where(counts > 0, (end - 1) // bm - first_blk + 1, 0)
    it_end = jnp.cumsum(n_it)
    it_start = it_end - n_it
    j = jnp.arange(n_items, dtype=jnp.int32)
    e_j = jnp.minimum(jnp.sum(j[:, None] >= it_end[None, :], axis=1), N_EXPERTS - 1).astype(jnp.int32)
    valid = j < it_end[-1]
    last_e = jnp.max(jnp.where(counts > 0, expert_ids, 0))
    e_j = jnp.where(valid, e_j, last_e)
    blk = jnp.where(valid, first_blk[e_j] + (j - it_start[e_j]), n_blocks - 1)
    lo = jnp.maximum(start[e_j], blk * bm) - blk * bm
    hi = jnp.minimum(end[e_j], (blk + 1) * bm) - blk * bm
    lo = jnp.where(valid, lo, 0)
    hi = jnp.where(valid, hi, 0)
    items = tuple(a.astype(jnp.int32) for a in (blk, e_j, lo, hi))
    return dest, items


def kernel(x_prompt, x_sample, state_gla, state_ret, meta_tokens, norm_mix_w, w_in, w_alpha_up, b_alpha, gla_norm_w, ret_norm_w, w_branch_gla, w_branch_ret, w_out, norm_ffn_w, w_router, b_router, w_gate_up, b_gate_up, w_down, b_down, norm_final_w):
    B, S, D = x_prompt.shape
    DB, DS, _ = x_sample.shape
    assert D == D_MODEL and S % CHUNK == 0 and DS == CHUNK and state_gla.shape[0] == 1
    l = 0

    offs = [0]
    for w in IN_WIDTHS:
        offs.append(offs[-1] + w)
    cols = [w_in[l][:, offs[i]:offs[i + 1]] for i in range(len(IN_WIDTHS))]
    qa, ka, va, ga, ra, qb, kb, vb, gb, m = cols
    ra = jnp.pad(ra, ((0, 0), (0, LANES - GLA_RANK)))
    ws = jnp.concatenate([qa, ka, va, ga, qb, kb, vb, gb, ra], axis=1).astype(BF16)
    wm = m.astype(BF16)
    wup = jnp.pad(w_alpha_up[l], ((0, LANES - GLA_RANK), (0, 0))).astype(BF16)
    ba = b_alpha[l][None, :]
    nmw = norm_mix_w[l][None, :]
    gnw = gla_norm_w[l][None, :]
    rnw = ret_norm_w[l][None, :]

    def mixer(x2d, pos, sg0, sr0, batch):
        zs, zm = _inproj(x2d, nmw, ws, wm)
        cs, sn = _rotary_tables(pos)
        ua, ub, sg, sr = _scan(zs, cs, sn, sg0, sr0, wup, ba, gnw, rnw, batch)
        return ua, ub, zm, sg, sr

    x_meta = jnp.concatenate([jnp.zeros((CHUNK - N_META, D), F32), meta_tokens.astype(F32)], axis=0)
    pos_meta = jnp.maximum(jnp.arange(CHUNK) - (CHUNK - N_META), 0)
    zero_state = jnp.zeros((1, HEADS, DK, DV), F32)
    _, _, _, sg_m, sr_m = mixer(x_meta, pos_meta, zero_state, zero_state, 1)

    xp = x_prompt.reshape(B * S, D)
    xs_ = x_sample.reshape(DB * DS, D)
    sg0 = jnp.broadcast_to(sg_m, (B, HEADS, DK, DV))
    sr0 = jnp.broadcast_to(sr_m, (B, HEADS, DK, DV))
    ua_p, ub_p, zm_p, sg_p, sr_p = mixer(xp, N_META + jnp.arange(S), sg0, sr0, B)
    ua_s, ub_s, zm_s, sg_s, sr_s = mixer(xs_, N_META + PAST_LEN + jnp.arange(DS),
                                         state_gla[l].astype(F32), state_ret[l].astype(F32), DB)

    wa = w_branch_gla[l].astype(BF16)
    wb = w_branch_ret[l].astype(BF16)
    wo = w_out[l].astype(BF16)
    nfw = norm_ffn_w[l][None, :]
    wr = jnp.pad(w_router[l].astype(F32), ((0, 0), (0, LANES - N_EXPERTS)))
    wr_hi = wr.astype(BF16)
    wr_mid = (wr - wr_hi.astype(F32)).astype(BF16)
    wr2 = jnp.stack([wr_hi, wr_mid])
    br = jnp.pad(b_router[l].astype(F32), (0, LANES - N_EXPERTS))[None, :]
    cnt0 = jnp.zeros((1, LANES), F32)
    x1_p, e_p, g_p, r_p, cnt_p = _post(ua_p, ub_p, zm_p, xp, wa, wb, wo, nfw, wr2, br, cnt0)
    x1_s, e_s, g_s, r_s, cnt = _post(ua_s, ub_s, zm_s, xs_, wa, wb, wo, nfw, wr2, br, cnt_p)

    Tp, Ts = B * S, DB * DS
    assert ((Tp + Ts) * TOP_K) % EXPERT_BLOCK == 0
    top_e = jnp.concatenate([e_p[:, :TOP_K], e_s[:, :TOP_K]], axis=0)
    rank = jnp.concatenate([r_p[:, :TOP_K], r_s[:, :TOP_K]], axis=0)
    dest, items = _routing(top_e, rank, cnt[0, :N_EXPERTS].astype(jnp.int32))
    dest = dest.reshape(-1)
    dest_p, dest_s = dest[:Tp * TOP_K], dest[Tp * TOP_K:]

    xs_rows = _dispatch(dest, x1_p, x1_s, nfw)
    ys_rows = _experts(items, xs_rows, w_gate_up[l].astype(BF16), b_gate_up[l][:, None, :],
                       w_down[l].astype(BF16), b_down[l][:, None, :])
    nw_final = norm_final_w[None, :]
    y_p = _combine(dest_p, g_p, x1_p, nw_final, ys_rows)
    y_s = _combine(dest_s, g_s, x1_s, nw_final, ys_rows)

    dt = state_gla.dtype
    return (y_p.reshape(B, S, D), y_s.reshape(DB, DS, D),
            sg_p[None].astype(dt), sr_p[None].astype(state_ret.dtype),
            sg_s[None].astype(dt), sr_s[None].astype(state_ret.dtype))
```

```python
import functools
import math

import jax
import jax.numpy as jnp
from jax import lax
from jax.experimental import pallas as pl
from jax.experimental.pallas import tpu as pltpu

F32 = jnp.float32
BF16 = jnp.bfloat16

D_MODEL = 1024
CHUNK = 64
N_META = 16
PAST_LEN = 4096
EPS = 1e-5
HEADS = 4
DK = 128
DV = 256
GLA_RANK = 16
GLA_TAU = 16.0
ROPE_BASE = 10000.0
N_EXPERTS = 32
TOP_K = 4
D_FF = 1024
SWIGLU_ALPHA = 1.702
SWIGLU_LIMIT = 7.0
IN_WIDTHS = (HEADS * DK, HEADS * DK, HEADS * DV, HEADS * DV, GLA_RANK,
             HEADS * DK, HEADS * DK, HEADS * DV, HEADS * DV, 2 * D_MODEL)

LANES = 128
QA, KA, VA, GA, QB, KB, VB, GB, RA = 0, 512, 1024, 2048, 3072, 3584, 4096, 5120, 6144
ZS_W = RA + LANES
ZM_W = 2 * D_MODEL
PQT, PKT, PKD, PVA, PGA, PQB, PKB, PKDB, PVB, PGB = 0, 512, 1024, 1536, 2560, 3584, 4096, 4608, 5120, 6144
PZ_W = PGB + HEADS * DV
MIN_SAFE_LOG_FORGET = -60.0

ROW_TILE = 512
DISPATCH_TILE = 256
COMBINE_TILE = 128
EXPERT_BLOCK = 512
ISSUE_UNROLL = 4
VMEM_LIMIT = 56 * 1024 * 1024


def _params(n_axes):
    return pltpu.CompilerParams(dimension_semantics=("arbitrary",) * n_axes, vmem_limit_bytes=VMEM_LIMIT)


def _rms(x, w):
    return x * lax.rsqrt(jnp.mean(x * x, axis=-1, keepdims=True) + EPS) * w


def _dot(a, b):
    return jnp.dot(a, b, preferred_element_type=F32)


def _split3(x):
    hi = x.astype(BF16)
    r1 = x - hi.astype(F32)
    mid = r1.astype(BF16)
    lo = (r1 - mid.astype(F32)).astype(BF16)
    return hi, mid, lo


def _log_sigmoid(x):
    return jnp.minimum(x, 0.0) - jnp.log1p(jnp.exp(-jnp.abs(x)))


def _rotate(t, cs, sn):
    return t * cs + pltpu.roll(t, DK // 2, axis=1) * sn


def _prep_kernel(x_ref, nw_ref, ws_ref, wm_ref, wup_ref, ba_ref, tril_ref, cs_ref, sn_ref,
                 zp_ref, zm_ref, dec_ref, bmin_ref):
    tm = x_ref.shape[0]
    n = tm // CHUNK
    h = _rms(x_ref[...], nw_ref[...]).astype(BF16)

    def proj(c0, w):
        return _dot(h, ws_ref[:, c0:c0 + w])

    ra = proj(RA, LANES).astype(BF16)
    la = _log_sigmoid(_dot(ra, wup_ref[...]) + ba_ref[...]) * (1.0 / GLA_TAU)
    hi, mid, lo = _split3(la)
    tril = tril_ref[...]
    b = _dot(tril, hi) + _dot(tril, mid) + _dot(tril, lo)
    b3 = b.reshape(n, CHUNK, HEADS * DK)
    bl3 = b3[:, CHUNK - 1:CHUNK, :]
    dec_ref[...] = jnp.exp(bl3)
    bl = jnp.broadcast_to(bl3, b3.shape).reshape(tm, HEADS * DK)
    bmin = jnp.min(jnp.min(b, axis=0, keepdims=True), axis=1, keepdims=True)
    bmin_ref[0] = jnp.broadcast_to(bmin, (1, LANES))

    q = proj(QA, HEADS * DK) * (DK ** -0.5)
    zp_ref[:, PQT:PQT + HEADS * DK] = (q * jnp.exp(b)).astype(BF16)
    k = proj(KA, HEADS * DK)
    zp_ref[:, PKT:PKT + HEADS * DK] = (k * jnp.exp(-b)).astype(BF16)
    zp_ref[:, PKD:PKD + HEADS * DK] = (k * jnp.exp(bl - b)).astype(BF16)

    half = HEADS * DV // 2
    for j in range(2):
        zp_ref[:, PVA + j * half:PVA + (j + 1) * half] = proj(VA + j * half, half).astype(BF16)
        g = proj(GA + j * half, half)
        zp_ref[:, PGA + j * half:PGA + (j + 1) * half] = (g * jax.nn.sigmoid(g)).astype(BF16)
        zp_ref[:, PVB + j * half:PVB + (j + 1) * half] = proj(VB + j * half, half).astype(BF16)
        g = proj(GB + j * half, half)
        zp_ref[:, PGB + j * half:PGB + (j + 1) * half] = (g * jax.nn.sigmoid(g)).astype(BF16)

    cs = cs_ref[...]
    sn = sn_ref[...]
    qb = proj(QB, HEADS * DK)
    kb = proj(KB, HEADS * DK)
    steps_left = (CHUNK - 1 - (lax.broadcasted_iota(jnp.int32, (tm, 1), 0) % CHUNK)).astype(F32)
    for hd in range(HEADS):
        sl = slice(hd * DK, (hd + 1) * DK)
        zp_ref[:, PQB + hd * DK:PQB + (hd + 1) * DK] = _rotate(qb[:, sl], cs, sn).astype(BF16)
        kr = _rotate(kb[:, sl], cs, sn) * (DK ** -0.5)
        zp_ref[:, PKB + hd * DK:PKB + (hd + 1) * DK] = kr.astype(BF16)
        zp_ref[:, PKDB + hd * DK:PKDB + (hd + 1) * DK] = (kr * jnp.exp(_ret_log_decay(hd) * steps_left)).astype(BF16)

    for c0 in range(0, ZM_W, 512):
        zm_ref[:, c0:c0 + 512] = jax.nn.sigmoid(_dot(h, wm_ref[:, c0:c0 + 512])).astype(BF16)


def _prep(x, nw, ws, wm, wup, ba, cs, sn):
    T = x.shape[0]
    tm = min(ROW_TILE, T)
    assert T % tm == 0 and cs.shape[0] % tm == 0
    n_tab = cs.shape[0] // tm
    n = tm // CHUNK
    r = jnp.arange(tm)
    tril = ((r[:, None] >= r[None, :]) & (r[:, None] // CHUNK == r[None, :] // CHUNK)).astype(BF16)
    const = dict(pipeline_mode=pl.Buffered(1))
    full = lambda shape, **kw: pl.BlockSpec(shape, lambda i: (0,) * len(shape), **kw)
    return pl.pallas_call(
        _prep_kernel,
        grid=(T // tm,),
        in_specs=[pl.BlockSpec((tm, D_MODEL), lambda i: (i, 0)),
                  full((1, D_MODEL)),
                  full((D_MODEL, ZS_W), **const), full((D_MODEL, ZM_W), **const),
                  full((LANES, HEADS * DK)), full((1, HEADS * DK)), full((tm, tm)),
                  pl.BlockSpec((tm, DK), lambda i: (i % n_tab, 0)),
                  pl.BlockSpec((tm, DK), lambda i: (i % n_tab, 0))],
        out_specs=[pl.BlockSpec((tm, PZ_W), lambda i: (i, 0)),
                   pl.BlockSpec((tm, ZM_W), lambda i: (i, 0)),
                   pl.BlockSpec((n, 1, HEADS * DK), lambda i: (i, 0, 0)),
                   pl.BlockSpec((1, 1, LANES), lambda i: (i, 0, 0))],
        out_shape=[jax.ShapeDtypeStruct((T, PZ_W), BF16), jax.ShapeDtypeStruct((T, ZM_W), BF16),
                   jax.ShapeDtypeStruct((T // CHUNK, 1, HEADS * DK), F32),
                   jax.ShapeDtypeStruct((T // tm, 1, LANES), F32)],
        compiler_params=_params(1),
        name="prep",
    )(x, nw, ws, wm, wup, ba, tril, cs, sn)


def _inproj_kernel(x_ref, nw_ref, ws_ref, zs_ref):
    h = _rms(x_ref[...], nw_ref[...]).astype(BF16)
    step = 512
    for c0 in range(0, ZS_W, step):
        c1 = min(c0 + step, ZS_W)
        zs_ref[:, c0:c1] = _dot(h, ws_ref[:, c0:c1]).astype(BF16)


def _inproj(x, nw, ws):
    T = x.shape[0]
    tm = min(ROW_TILE, T)
    assert T % tm == 0
    return pl.pallas_call(
        _inproj_kernel,
        grid=(T // tm,),
        in_specs=[pl.BlockSpec((tm, D_MODEL), lambda i: (i, 0)),
                  pl.BlockSpec((1, D_MODEL), lambda i: (0, 0)),
                  pl.BlockSpec((D_MODEL, ZS_W), lambda i: (0, 0), pipeline_mode=pl.Buffered(1))],
        out_specs=pl.BlockSpec((tm, ZS_W), lambda i: (i, 0)),
        out_shape=jax.ShapeDtypeStruct((T, ZS_W), BF16),
        compiler_params=_params(1),
        name="inproj",
    )(x, nw, ws)


def _ret_log_decay(head):
    return math.log(1.0 - 2.0 ** (-5.0 - head))


NT_DIMS = (((1,), (1,)), ((), ()))
TN_DIMS = (((0,), (0,)), ((), ()))


def _scan_fast_kernel(z_ref, dec_ref, sg0_ref, sr0_ref, gnw_ref, rnw_ref,
                      ua_ref, ub_ref, sg_out_ref, sr_out_ref, sg_scr, sr_scr, *, n_chunks):
    c = pl.program_id(1)
    L = CHUNK

    @pl.when(c == 0)
    def _():
        sg_scr[...] = sg0_ref[0]
        sr_scr[...] = sr0_ref[0]

    row = lax.broadcasted_iota(jnp.int32, (L, L), 0)
    col = lax.broadcasted_iota(jnp.int32, (L, L), 1)
    causal = row >= col
    rel = (row - col).astype(F32)
    t_col = lax.broadcasted_iota(jnp.int32, (L, 1), 0).astype(F32)
    dec = dec_ref[0]

    def nt(a, b):
        return lax.dot_general(a, b, NT_DIMS, preferred_element_type=F32)

    def tn(a, b):
        return lax.dot_general(a, b, TN_DIMS, preferred_element_type=F32)

    for h in range(HEADS):
        qt = z_ref[:, PQT + h * DK:PQT + (h + 1) * DK]
        kt = z_ref[:, PKT + h * DK:PKT + (h + 1) * DK]
        kd = z_ref[:, PKD + h * DK:PKD + (h + 1) * DK]
        v = z_ref[:, PVA + h * DV:PVA + (h + 1) * DV]
        s = jnp.where(causal, nt(qt, kt), 0.0).astype(BF16)
        st_old = sg_scr[h]
        o = _dot(s, v) + nt(qt, st_old.astype(BF16))
        sg_scr[h] = dec[:, h * DK:(h + 1) * DK] * st_old + tn(v, kd)
        g = z_ref[:, PGA + h * DV:PGA + (h + 1) * DV].astype(F32)
        ua_ref[:, h * DV:(h + 1) * DV] = (_rms(o, gnw_ref[...]) * g).astype(BF16)

    for h in range(HEADS):
        lg = _ret_log_decay(h)
        qr = z_ref[:, PQB + h * DK:PQB + (h + 1) * DK]
        kr = z_ref[:, PKB + h * DK:PKB + (h + 1) * DK]
        kd = z_ref[:, PKDB + h * DK:PKDB + (h + 1) * DK]
        v = z_ref[:, PVB + h * DV:PVB + (h + 1) * DV]
        decay = jnp.exp(jnp.where(causal, lg * rel, -jnp.inf))
        s = (nt(qr, kr) * decay).astype(BF16)
        st_old = sr_scr[h]
        inner = jnp.exp(lg * (t_col + 1.0))
        o = _dot(s, v) + inner * nt(qr, st_old.astype(BF16))
        sr_scr[h] = math.exp(lg * L) * st_old + tn(v, kd)
        g = z_ref[:, PGB + h * DV:PGB + (h + 1) * DV].astype(F32)
        ub_ref[:, h * DV:(h + 1) * DV] = (_rms(o, rnw_ref[...]) * g).astype(BF16)

    @pl.when(c == n_chunks - 1)
    def _():
        sg_out_ref[0] = sg_scr[...]
        sr_out_ref[0] = sr_scr[...]


def _scan_fast(zp, dec, sg0t, sr0t, gnw, rnw, batch):
    T = zp.shape[0]
    n_chunks = T // (batch * CHUNK)
    state_spec = pl.BlockSpec((1, HEADS, DV, DK), lambda b, c: (b, 0, 0, 0))
    full = lambda shape: pl.BlockSpec(shape, lambda b, c: (0,) * len(shape))
    rows = lambda w: pl.BlockSpec((CHUNK, w), lambda b, c: (b * n_chunks + c, 0))
    return pl.pallas_call(
        functools.partial(_scan_fast_kernel, n_chunks=n_chunks),
        grid=(batch, n_chunks),
        in_specs=[rows(PZ_W),
                  pl.BlockSpec((1, 1, HEADS * DK), lambda b, c: (b * n_chunks + c, 0, 0)),
                  state_spec, state_spec, full((1, DV)), full((1, DV))],
        out_specs=[rows(HEADS * DV), rows(HEADS * DV), state_spec, state_spec],
        out_shape=[jax.ShapeDtypeStruct((T, HEADS * DV), BF16), jax.ShapeDtypeStruct((T, HEADS * DV), BF16),
                   jax.ShapeDtypeStruct((batch, HEADS, DV, DK), F32),
                   jax.ShapeDtypeStruct((batch, HEADS, DV, DK), F32)],
        scratch_shapes=[pltpu.VMEM((HEADS, DV, DK), F32), pltpu.VMEM((HEADS, DV, DK), F32)],
        compiler_params=_params(2),
        name="scanfast",
    )(zp, dec, sg0t, sr0t, gnw, rnw)


def _scan_kernel(z_ref, cs_ref, sn_ref, sg0_ref, sr0_ref, wup_ref, ba_ref, gnw_ref, rnw_ref,
                 ua_ref, ub_ref, sg_out_ref, sr_out_ref, sg_scr, sr_scr, k_scr, b_scr, *, n_chunks):
    c = pl.program_id(1)
    L = CHUNK

    @pl.when(c == 0)
    def _():
        sg_scr[...] = sg0_ref[0]
        sr_scr[...] = sr0_ref[0]

    row = lax.broadcasted_iota(jnp.int32, (L, L), 0)
    col = lax.broadcasted_iota(jnp.int32, (L, L), 1)
    causal = row >= col
    tril = causal.astype(BF16)

    la = _log_sigmoid(_dot(z_ref[:, RA:RA + LANES], wup_ref[...]) + ba_ref[...]) * (1.0 / GLA_TAU)
    hi, mid, lo = _split3(la)
    b_all = _dot(tril, hi) + _dot(tril, mid) + _dot(tril, lo)

    for h in range(HEADS):
        bh = b_all[:, h * DK:(h + 1) * DK]
        q = z_ref[:, QA + h * DK:QA + (h + 1) * DK].astype(F32) * (DK ** -0.5)
        k = z_ref[:, KA + h * DK:KA + (h + 1) * DK].astype(F32)
        v = z_ref[:, VA + h * DV:VA + (h + 1) * DV]
        qt = (q * jnp.exp(bh)).astype(BF16)
        kT = k.T
        bT = bh.T
        blT = bT[:, L - 1:L]
        kdT = (kT * jnp.exp(blT - bT)).astype(BF16)
        k_scr[...] = k
        b_scr[...] = bh

        def score_column(j, acc, q=q, bh=bh):
            e = jnp.exp(jnp.minimum(bh - b_scr[pl.ds(j, 1), :], 0.0))
            colv = jnp.sum(q * k_scr[pl.ds(j, 1), :] * e, axis=1, keepdims=True)
            return jnp.where(col == j, colv, acc)

        s = lax.fori_loop(0, L, score_column, jnp.zeros((L, L), F32))
        s = jnp.where(causal, s, 0.0).astype(BF16)
        s_old = sg_scr[h]
        o = _dot(s, v) + _dot(qt, s_old.astype(BF16))
        sg_scr[h] = jnp.exp(blT) * s_old + _dot(kdT, v)
        g = z_ref[:, GA + h * DV:GA + (h + 1) * DV].astype(F32)
        ua_ref[:, h * DV:(h + 1) * DV] = (_rms(o, gnw_ref[...]) * (g * jax.nn.sigmoid(g))).astype(BF16)

    cs = cs_ref[...]
    sn = sn_ref[...]
    rel = (row - col).astype(F32)
    t_col = lax.broadcasted_iota(jnp.int32, (L, 1), 0).astype(F32)
    s_row = lax.broadcasted_iota(jnp.int32, (1, L), 1).astype(F32)
    for h in range(HEADS):
        lg = _ret_log_decay(h)
        q = z_ref[:, QB + h * DK:QB + (h + 1) * DK].astype(F32)
        k = z_ref[:, KB + h * DK:KB + (h + 1) * DK].astype(F32)
        v = z_ref[:, VB + h * DV:VB + (h + 1) * DV]
        qr = _rotate(q, cs, sn).astype(BF16)
        kr = _rotate(k, cs, sn) * (DK ** -0.5)
        krT = kr.T
        decay = jnp.exp(jnp.where(causal, lg * rel, -jnp.inf))
        s = (_dot(qr, krT.astype(BF16)) * decay).astype(BF16)
        s_old = sr_scr[h]
        inner = jnp.exp(lg * (t_col + 1.0))
        o = _dot(s, v) + inner * _dot(qr, s_old.astype(BF16))
        kdT = (krT * jnp.exp(lg * (L - 1.0 - s_row))).astype(BF16)
        sr_scr[h] = math.exp(lg * L) * s_old + _dot(kdT, v)
        g = z_ref[:, GB + h * DV:GB + (h + 1) * DV].astype(F32)
        ub_ref[:, h * DV:(h + 1) * DV] = (_rms(o, rnw_ref[...]) * (g * jax.nn.sigmoid(g))).astype(BF16)

    @pl.when(c == n_chunks - 1)
    def _():
        sg_out_ref[0] = sg_scr[...]
        sr_out_ref[0] = sr_scr[...]


def _scan(zs, cs, sn, sg0, sr0, wup, ba, gnw, rnw, batch):
    T = zs.shape[0]
    n_chunks = T // (batch * CHUNK)
    state_spec = pl.BlockSpec((1, HEADS, DK, DV), lambda b, c: (b, 0, 0, 0))
    full = lambda shape: pl.BlockSpec(shape, lambda b, c: (0,) * len(shape))
    return pl.pallas_call(
        functools.partial(_scan_kernel, n_chunks=n_chunks),
        grid=(batch, n_chunks),
        in_specs=[pl.BlockSpec((CHUNK, ZS_W), lambda b, c: (b * n_chunks + c, 0)),
                  pl.BlockSpec((CHUNK, DK), lambda b, c: (c, 0)),
                  pl.BlockSpec((CHUNK, DK), lambda b, c: (c, 0)),
                  state_spec, state_spec,
                  full((LANES, HEADS * DK)), full((1, HEADS * DK)), full((1, DV)), full((1, DV))],
        out_specs=[pl.BlockSpec((CHUNK, HEADS * DV), lambda b, c: (b * n_chunks + c, 0)),
                   pl.BlockSpec((CHUNK, HEADS * DV), lambda b, c: (b * n_chunks + c, 0)),
                   state_spec, state_spec],
        out_shape=[jax.ShapeDtypeStruct((T, HEADS * DV), BF16), jax.ShapeDtypeStruct((T, HEADS * DV), BF16),
                   jax.ShapeDtypeStruct((batch, HEADS, DK, DV), F32),
                   jax.ShapeDtypeStruct((batch, HEADS, DK, DV), F32)],
        scratch_shapes=[pltpu.VMEM((HEADS, DK, DV), F32), pltpu.VMEM((HEADS, DK, DV), F32),
                        pltpu.VMEM((CHUNK, DK), F32), pltpu.VMEM((CHUNK, DK), F32)],
        compiler_params=_params(2),
        name="scan",
    )(zs, cs, sn, sg0, sr0, wup, ba, gnw, rnw)


def _post_kernel(ua_ref, ub_ref, zm_ref, x_ref, wa_ref, wb_ref, wo_ref, nfw_ref, wr_ref, br_ref, cnt0_ref,
                 earlier_ref, x1_ref, e_ref, g_ref, rank_ref, cnt_ref):
    @pl.when(pl.program_id(0) == 0)
    def _():
        cnt_ref[...] = cnt0_ref[...]

    a = _dot(ua_ref[...], wa_ref[...])
    b = _dot(ub_ref[...], wb_ref[...])
    merged = zm_ref[:, :D_MODEL].astype(F32) * a + zm_ref[:, D_MODEL:].astype(F32) * b
    x1 = x_ref[...] + _dot(merged.astype(BF16), wo_ref[...])
    x1_ref[...] = x1

    h2 = _rms(x1, nfw_ref[...])
    h_hi, h_mid, _ = _split3(h2)
    w_hi, w_mid = wr_ref[0], wr_ref[1]
    logits = _dot(h_hi, w_hi) + _dot(h_hi, w_mid) + _dot(h_mid, w_hi) + br_ref[...]
    lane = lax.broadcasted_iota(jnp.int32, logits.shape, 1)
    lane_f = lane.astype(F32)
    work = jnp.where(lane < N_EXPERTS, logits, -jnp.inf)
    vals, idxs = [], []
    for _ in range(TOP_K):
        m = jnp.max(work, axis=1, keepdims=True)
        idx = jnp.min(jnp.where(work == m, lane_f, float(LANES)), axis=1, keepdims=True)
        vals.append(m)
        idxs.append(idx)
        work = jnp.where(lane_f == idx, -jnp.inf, work)
    ps = [jnp.exp(v - vals[0]) for v in vals]
    den = ps[0] + ps[1] + ps[2] + ps[3]
    e_out = jnp.zeros(logits.shape, F32)
    g_out = jnp.zeros(logits.shape, F32)
    for k in range(TOP_K):
        e_out = jnp.where(lane == k, idxs[k], e_out)
        g_out = jnp.where(lane == k, ps[k] / den, g_out)
    e_ref[...] = e_out.astype(jnp.int32)
    g_ref[...] = g_out

    multi_hot = jnp.zeros(logits.shape, F32)
    for k in range(TOP_K):
        multi_hot = multi_hot + (lane_f == idxs[k]).astype(F32)
    before = _dot(earlier_ref[...], multi_hot.astype(BF16)) + cnt_ref[...]
    r_out = jnp.zeros(logits.shape, F32)
    for k in range(TOP_K):
        rk = jnp.sum(jnp.where(lane_f == idxs[k], before, 0.0), axis=1, keepdims=True)
        r_out = jnp.where(lane == k, rk, r_out)
    rank_ref[...] = r_out.astype(jnp.int32)
    cnt_ref[...] += jnp.sum(multi_hot, axis=0, keepdims=True)


def _post(ua, ub, zm, x, wa, wb, wo, nfw, wr, br, cnt0):
    T = x.shape[0]
    tm = min(ROW_TILE, T)
    assert T % tm == 0
    rows = lambda w: pl.BlockSpec((tm, w), lambda i: (i, 0))
    full = lambda shape: pl.BlockSpec(shape, lambda i: (0,) * len(shape))
    earlier = (jnp.arange(tm)[:, None] > jnp.arange(tm)[None, :]).astype(BF16)
    return pl.pallas_call(
        _post_kernel,
        grid=(T // tm,),
        in_specs=[rows(HEADS * DV), rows(HEADS * DV), rows(ZM_W), rows(D_MODEL),
                  full((HEADS * DV, D_MODEL)), full((HEADS * DV, D_MODEL)), full((D_MODEL, D_MODEL)),
                  full((1, D_MODEL)), full((2, D_MODEL, LANES)), full((1, LANES)), full((1, LANES)),
                  full((tm, tm))],
        out_specs=[rows(D_MODEL), rows(LANES), rows(LANES), rows(LANES), full((1, LANES))],
        out_shape=[jax.ShapeDtypeStruct((T, D_MODEL), F32), jax.ShapeDtypeStruct((T, LANES), jnp.int32),
                   jax.ShapeDtypeStruct((T, LANES), F32), jax.ShapeDtypeStruct((T, LANES), jnp.int32),
                   jax.ShapeDtypeStruct((1, LANES), F32)],
        compiler_params=_params(1),
        name="post",
    )(ua, ub, zm, x, wa, wb, wo, nfw, wr, br, cnt0, earlier)


def _row_copy_wait(src_rows, dst_rows, sem):
    pltpu.make_async_copy(src_rows, dst_rows, sem).wait()


def _dispatch_kernel(dest_ref, xp_ref, xs_ref, nfw_ref, rows_ref, hbuf, sem, *, n_prompt_tiles, n_tiles):
    tm = hbuf.shape[1]
    i = pl.program_id(0)
    slot = i % 2

    def wait_slot(s):
        for _ in range(TOP_K):
            _row_copy_wait(hbuf.at[s], rows_ref.at[pl.ds(0, tm)], sem.at[s])

    @pl.when(i >= 2)
    def _():
        wait_slot(slot)

    @pl.when(i < n_prompt_tiles)
    def _():
        hbuf[slot] = _rms(xp_ref[...], nfw_ref[...])

    @pl.when(i >= n_prompt_tiles)
    def _():
        hbuf[slot] = _rms(xs_ref[...], nfw_ref[...])

    def issue(r, carry):
        for k in range(TOP_K):
            d = dest_ref[0, 0, r * TOP_K + k]
            pltpu.make_async_copy(hbuf.at[slot, pl.ds(r, 1)], rows_ref.at[pl.ds(d, 1)], sem.at[slot]).start()
        return carry

    lax.fori_loop(0, tm, issue, 0, unroll=ISSUE_UNROLL)

    @pl.when(i == n_tiles - 1)
    def _():
        @pl.when(i >= 1)
        def _():
            wait_slot(1 - slot)
        wait_slot(slot)


def _dispatch(dest, x1_p, x1_s, nfw):
    Tp, Ts = x1_p.shape[0], x1_s.shape[0]
    tm = math.gcd(DISPATCH_TILE, Tp, Ts)
    assert tm % 8 == 0
    n_p, n_s = Tp // tm, Ts // tm
    return pl.pallas_call(
        functools.partial(_dispatch_kernel, n_prompt_tiles=n_p, n_tiles=n_p + n_s),
        grid=(n_p + n_s,),
        in_specs=[pl.BlockSpec((1, 1, tm * TOP_K), lambda i: (i, 0, 0), memory_space=pltpu.SMEM),
                  pl.BlockSpec((tm, D_MODEL), lambda i: (jnp.minimum(i, n_p - 1), 0)),
                  pl.BlockSpec((tm, D_MODEL), lambda i: (jnp.maximum(i - n_p, 0), 0)),
                  pl.BlockSpec((1, D_MODEL), lambda i: (0, 0))],
        out_specs=pl.BlockSpec(memory_space=pl.ANY),
        out_shape=jax.ShapeDtypeStruct(((Tp + Ts) * TOP_K, D_MODEL), F32),
        scratch_shapes=[pltpu.VMEM((2, tm, D_MODEL), F32), pltpu.SemaphoreType.DMA((2,))],
        compiler_params=_params(1),
        name="dispatch",
    )(dest.reshape(n_p + n_s, 1, tm * TOP_K), x1_p, x1_s, nfw)


def _expert_kernel(blk_ref, exp_ref, lo_ref, hi_ref, xs_ref, wgu_ref, bgu_ref, wd_ref, bd_ref, ys_ref):
    del blk_ref, exp_ref
    i = pl.program_id(0)
    lo, hi = lo_ref[i], hi_ref[i]

    @pl.when(hi > lo)
    def _():
        hgu = _dot(xs_ref[...].astype(BF16), wgu_ref[0]) + bgu_ref[0]
        glu = jnp.minimum(hgu[:, :D_FF], SWIGLU_LIMIT)
        lin = jnp.clip(hgu[:, D_FF:], -SWIGLU_LIMIT, SWIGLU_LIMIT)
        act = glu * jax.nn.sigmoid(SWIGLU_ALPHA * glu) * (lin + 1.0)
        y = _dot(act.astype(BF16), wd_ref[0]) + bd_ref[0]

        @pl.when(lo == 0)
        def _():
            ys_ref[...] = y

        @pl.when(lo > 0)
        def _():
            row = lax.broadcasted_iota(jnp.int32, (y.shape[0], 1), 0)
            ys_ref[...] = jnp.where(row >= lo, y, ys_ref[...])


def _experts(items, xs, wgu, bgu, wd, bd):
    P = xs.shape[0]
    bm = EXPERT_BLOCK
    n_items = items[0].shape[0]
    blk = lambda i, b, e, lo, hi: (b[i], 0)
    ex3 = lambda i, b, e, lo, hi: (e[i], 0, 0)
    return pl.pallas_call(
        _expert_kernel,
        grid_spec=pltpu.PrefetchScalarGridSpec(
            num_scalar_prefetch=4,
            grid=(n_items,),
            in_specs=[pl.BlockSpec((bm, D_MODEL), blk),
                      pl.BlockSpec((1, D_MODEL, 2 * D_FF), ex3),
                      pl.BlockSpec((1, 1, 2 * D_FF), ex3),
                      pl.BlockSpec((1, D_FF, D_MODEL), ex3),
                      pl.BlockSpec((1, 1, D_MODEL), ex3)],
            out_specs=pl.BlockSpec((bm, D_MODEL), blk)),
        out_shape=jax.ShapeDtypeStruct((P, D_MODEL), F32),
        compiler_params=_params(1),
        name="experts",
    )(*items, xs, wgu, bgu, wd, bd)


def _combine_kernel(dest_ref, dest_next_ref, gate_ref, x1_ref, nw_ref, ys_ref, out_ref, buf, sem, *, n_tiles):
    tm = x1_ref.shape[0]
    i = pl.program_id(0)
    slot = i % 2

    def issue_tile(idx_ref, s):
        def issue(r, carry):
            for k in range(TOP_K):
                d = idx_ref[0, 0, r * TOP_K + k]
                pltpu.make_async_copy(ys_ref.at[pl.ds(d, 1)], buf.at[s, k, pl.ds(r, 1)], sem.at[s]).start()
            return carry

        lax.fori_loop(0, tm, issue, 0, unroll=ISSUE_UNROLL)

    @pl.when(i == 0)
    def _():
        issue_tile(dest_ref, slot)

    @pl.when(i + 1 < n_tiles)
    def _():
        issue_tile(dest_next_ref, 1 - slot)

    for k in range(TOP_K):
        _row_copy_wait(ys_ref.at[pl.ds(0, tm)], buf.at[slot, k], sem.at[slot])
    gate = gate_ref[...]
    f = gate[:, 0:1] * buf[slot, 0]
    for k in range(1, TOP_K):
        f = f + gate[:, k:k + 1] * buf[slot, k]
    out_ref[...] = _rms(x1_ref[...] + f, nw_ref[...])


def _combine(dest, gate, x1, nw, ys):
    T = x1.shape[0]
    tm = min(COMBINE_TILE, T)
    assert T % tm == 0
    n = T // tm
    dest3 = dest.reshape(n, 1, tm * TOP_K)
    return pl.pallas_call(
        functools.partial(_combine_kernel, n_tiles=n),
        grid=(n,),
        in_specs=[pl.BlockSpec((1, 1, tm * TOP_K), lambda i: (i, 0, 0), memory_space=pltpu.SMEM),
                  pl.BlockSpec((1, 1, tm * TOP_K), lambda i: (jnp.minimum(i + 1, n - 1), 0, 0),
                               memory_space=pltpu.SMEM),
                  pl.BlockSpec((tm, LANES), lambda i: (i, 0)),
                  pl.BlockSpec((tm, D_MODEL), lambda i: (i, 0)),
                  pl.BlockSpec((1, D_MODEL), lambda i: (0, 0)),
                  pl.BlockSpec(memory_space=pl.ANY)],
        out_specs=pl.BlockSpec((tm, D_MODEL), lambda i: (i, 0)),
        out_shape=jax.ShapeDtypeStruct((T, D_MODEL), F32),
        scratch_shapes=[pltpu.VMEM((2, TOP_K, tm, D_MODEL), F32), pltpu.SemaphoreType.DMA((2,))],
        compiler_params=_params(1),
        name="combine",
    )(dest3, dest3, gate, x1, nw, ys)


def _rotary_tables(pos):
    inv = ROPE_BASE ** (-jnp.arange(0, DK, 2, dtype=F32) / DK)
    ang = pos.astype(F32)[:, None] * inv[None, :]
    cos, sin = jnp.cos(ang), jnp.sin(ang)
    return jnp.concatenate([cos, cos], axis=1), jnp.concatenate([-sin, sin], axis=1)


def _routing(top_e, rank, counts):
    bm = EXPERT_BLOCK
    n_rows = top_e.size
    n_blocks = n_rows // bm
    n_items = n_blocks + N_EXPERTS - 1
    start = jnp.cumsum(counts) - counts
    end = start + counts
    expert_ids = jnp.arange(N_EXPERTS, dtype=jnp.int32)
    onehot = top_e[:, :, None] == expert_ids[None, None, :]
    dest = (jnp.sum(jnp.where(onehot, start[None, None, :], 0), axis=2) + rank).astype(jnp.int32)

    first_blk = start // bm
    n_it = jnp.where(counts > 0, (end - 1) // bm - first_blk + 1, 0)
    it_end = jnp.cumsum(n_it)
    it_start = it_end - n_it
    j = jnp.arange(n_items, dtype=jnp.int32)
    e_j = jnp.minimum(jnp.sum(j[:, None] >= it_end[None, :], axis=1), N_EXPERTS - 1).astype(jnp.int32)
    valid = j < it_end[-1]
    last_e = jnp.max(jnp.where(counts > 0, expert_ids, 0))
    e_j = jnp.where(valid, e_j, last_e)
    blk = jnp.where(valid, first_blk[e_j] + (j - it_start[e_j]), n_blocks - 1)
    lo = jnp.maximum(start[e_j], blk * bm) - blk * bm
    hi = jnp.minimum(end[e_j], (blk + 1) * bm) - blk * bm
    lo = jnp.where(valid, lo, 0)
    hi = jnp.where(valid, hi, 0)
    items = tuple(a.astype(jnp.int32) for a in (blk, e_j, lo, hi))
    return dest, items


def kernel(x_prompt, x_sample, state_gla, state_ret, meta_tokens, norm_mix_w, w_in, w_alpha_up, b_alpha, gla_norm_w, ret_norm_w, w_branch_gla, w_branch_ret, w_out, norm_ffn_w, w_router, b_router, w_gate_up, b_gate_up, w_down, b_down, norm_final_w):
    B, S, D = x_prompt.shape
    DB, DS, _ = x_sample.shape
    assert D == D_MODEL and S % CHUNK == 0 and DS == CHUNK and state_gla.shape[0] == 1
    l = 0

    offs = [0]
    for w in IN_WIDTHS:
        offs.append(offs[-1] + w)
    cols = [w_in[l][:, offs[i]:offs[i + 1]] for i in range(len(IN_WIDTHS))]
    qa, ka, va, ga, ra, qb, kb, vb, gb, m = cols
    ra = jnp.pad(ra, ((0, 0), (0, LANES - GLA_RANK)))
    ws = jnp.concatenate([qa, ka, va, ga, qb, kb, vb, gb, ra], axis=1).astype(BF16)
    wm = m.astype(BF16)
    wup = jnp.pad(w_alpha_up[l], ((0, LANES - GLA_RANK), (0, 0))).astype(BF16)
    ba = b_alpha[l][None, :]
    nmw = norm_mix_w[l][None, :]
    gnw = gla_norm_w[l][None, :]
    rnw = ret_norm_w[l][None, :]

    def mixer(x2d, pos, sg0, sr0, batch):
        T = x2d.shape[0]
        cs, sn = _rotary_tables(pos)
        reps = max(1, min(ROW_TILE, T) // pos.shape[0])
        zp, zm, dec, bmin = _prep(x2d, nmw, ws, wm, wup, ba, jnp.tile(cs, (reps, 1)), jnp.tile(sn, (reps, 1)))

        def ratio_form():
            ua, ub, sgt, srt = _scan_fast(zp, dec, jnp.swapaxes(sg0, 2, 3), jnp.swapaxes(sr0, 2, 3), gnw, rnw, batch)
            return ua, ub, jnp.swapaxes(sgt, 2, 3), jnp.swapaxes(srt, 2, 3)

        def exact_form():
            return tuple(_scan(_inproj(x2d, nmw, ws), cs, sn, sg0, sr0, wup, ba, gnw, rnw, batch))

        ua, ub, sg, sr = lax.cond(jnp.min(bmin) >= MIN_SAFE_LOG_FORGET, ratio_form, exact_form)
        return ua, ub, zm, sg, sr

    x_meta = jnp.concatenate([jnp.zeros((CHUNK - N_META, D), F32), meta_tokens.astype(F32)], axis=0)
    pos_meta = jnp.maximum(jnp.arange(CHUNK) - (CHUNK - N_META), 0)
    zero_state = jnp.zeros((1, HEADS, DK, DV), F32)
    _, _, _, sg_m, sr_m = mixer(x_meta, pos_meta, zero_state, zero_state, 1)

    xp = x_prompt.reshape(B * S, D)
    xs_ = x_sample.reshape(DB * DS, D)
    sg0 = jnp.broadcast_to(sg_m, (B, HEADS, DK, DV))
    sr0 = jnp.broadcast_to(sr_m, (B, HEADS, DK, DV))
    ua_p, ub_p, zm_p, sg_p, sr_p = mixer(xp, N_META + jnp.arange(S), sg0, sr0, B)
    ua_s, ub_s, zm_s, sg_s, sr_s = mixer(xs_, N_META + PAST_LEN + jnp.arange(DS),
                                         state_gla[l].astype(F32), state_ret[l].astype(F32), DB)

    wa = w_branch_gla[l].astype(BF16)
    wb = w_branch_ret[l].astype(BF16)
    wo = w_out[l].astype(BF16)
    nfw = norm_ffn_w[l][None, :]
    wr = jnp.pad(w_router[l].astype(F32), ((0, 0), (0, LANES - N_EXPERTS)))
    wr_hi = wr.astype(BF16)
    wr_mid = (wr - wr_hi.astype(F32)).astype(BF16)
    wr2 = jnp.stack([wr_hi, wr_mid])
    br = jnp.pad(b_router[l].astype(F32), (0, LANES - N_EXPERTS))[None, :]
    cnt0 = jnp.zeros((1, LANES), F32)
    x1_p, e_p, g_p, r_p, cnt_p = _post(ua_p, ub_p, zm_p, xp, wa, wb, wo, nfw, wr2, br, cnt0)
    x1_s, e_s, g_s, r_s, cnt = _post(ua_s, ub_s, zm_s, xs_, wa, wb, wo, nfw, wr2, br, cnt_p)

    Tp, Ts = B * S, DB * DS
    assert ((Tp + Ts) * TOP_K) % EXPERT_BLOCK == 0
    top_e = jnp.concatenate([e_p[:, :TOP_K], e_s[:, :TOP_K]], axis=0)
    rank = jnp.concatenate([r_p[:, :TOP_K], r_s[:, :TOP_K]], axis=0)
    dest, items = _routing(top_e, rank, cnt[0, :N_EXPERTS].astype(jnp.int32))
    dest = dest.reshape(-1)
    dest_p, dest_s = dest[:Tp * TOP_K], dest[Tp * TOP_K:]

    xs_rows = _dispatch(dest, x1_p, x1_s, nfw)
    ys_rows = _experts(items, xs_rows, w_gate_up[l].astype(BF16), b_gate_up[l][:, None, :],
                       w_down[l].astype(BF16), b_down[l][:, None, :])
    nw_final = norm_final_w[None, :]
    y_p = _combine(dest_p, g_p, x1_p, nw_final, ys_rows)
    y_s = _combine(dest_s, g_s, x1_s, nw_final, ys_rows)

    dt = state_gla.dtype
    return (y_p.reshape(B, S, D), y_s.reshape(DB, DS, D),
            sg_p[None].astype(dt), sr_p[None].astype(state_ret.dtype),
            sg_s[None].astype(dt), sr_s[None].astype(state_ret.dtype))
```

```python
import functools
import math

import jax
import jax.numpy as jnp
from jax import lax
from jax.experimental import pallas as pl
from jax.experimental.pallas import tpu as pltpu

F32 = jnp.float32
BF16 = jnp.bfloat16

D_MODEL = 1024
CHUNK = 64
N_META = 16
PAST_LEN = 4096
EPS = 1e-5
HEADS = 4
DK = 128
DV = 256
GLA_RANK = 16
GLA_TAU = 16.0
ROPE_BASE = 10000.0
N_EXPERTS = 32
TOP_K = 4
D_FF = 1024
SWIGLU_ALPHA = 1.702
SWIGLU_LIMIT = 7.0
IN_WIDTHS = (HEADS * DK, HEADS * DK, HEADS * DV, HEADS * DV, GLA_RANK,
             HEADS * DK, HEADS * DK, HEADS * DV, HEADS * DV, 2 * D_MODEL)

LANES = 128
QA, KA, VA, GA, QB, KB, VB, GB, RA = 0, 512, 1024, 2048, 3072, 3584, 4096, 5120, 6144
ZS_W = RA + LANES
ZM_W = 2 * D_MODEL
PQT, PKT, PKD, PVA, PGA, PQB, PKB, PKDB, PVB, PGB = 0, 512, 1024, 1536, 2560, 3584, 4096, 4608, 5120, 6144
PZ_W = PGB + HEADS * DV
MIN_SAFE_LOG_FORGET = -60.0

ROW_TILE = 512
DISPATCH_TILE = 256
COMBINE_TILE = 128
EXPERT_BLOCK = 512
ISSUE_UNROLL = 4
VMEM_LIMIT = 56 * 1024 * 1024


def _params(n_axes):
    return pltpu.CompilerParams(dimension_semantics=("arbitrary",) * n_axes, vmem_limit_bytes=VMEM_LIMIT)


def _rms(x, w):
    return x * lax.rsqrt(jnp.mean(x * x, axis=-1, keepdims=True) + EPS) * w


def _dot(a, b):
    return jnp.dot(a, b, preferred_element_type=F32)


def _split3(x):
    hi = x.astype(BF16)
    r1 = x - hi.astype(F32)
    mid = r1.astype(BF16)
    lo = (r1 - mid.astype(F32)).astype(BF16)
    return hi, mid, lo


def _log_sigmoid(x):
    return jnp.minimum(x, 0.0) - jnp.log1p(jnp.exp(-jnp.abs(x)))


def _rotate(t, cs, sn):
    return t * cs + pltpu.roll(t, DK // 2, axis=1) * sn


def _prep_kernel(x_ref, nw_ref, ws_ref, wm_ref, wup_ref, ba_ref, tril_ref, cs_ref, sn_ref,
                 zp_ref, zm_ref, dec_ref, bmin_ref):
    tm = x_ref.shape[0]
    n = tm // CHUNK
    h = _rms(x_ref[...], nw_ref[...]).astype(BF16)

    def proj(c0, w):
        return _dot(h, ws_ref[:, c0:c0 + w])

    ra = proj(RA, LANES).astype(BF16)
    la = _log_sigmoid(_dot(ra, wup_ref[...]) + ba_ref[...]) * (1.0 / GLA_TAU)
    hi, mid, lo = _split3(la)
    tril = tril_ref[...]
    b = _dot(tril, hi) + _dot(tril, mid) + _dot(tril, lo)
    b3 = b.reshape(n, CHUNK, HEADS * DK)
    bl3 = b3[:, CHUNK - 1:CHUNK, :]
    dec_ref[...] = jnp.exp(bl3)
    bl = jnp.broadcast_to(bl3, b3.shape).reshape(tm, HEADS * DK)
    bmin = jnp.min(jnp.min(b, axis=0, keepdims=True), axis=1, keepdims=True)
    bmin_ref[0] = jnp.broadcast_to(bmin, (1, LANES))

    q = proj(QA, HEADS * DK) * (DK ** -0.5)
    zp_ref[:, PQT:PQT + HEADS * DK] = (q * jnp.exp(b)).astype(BF16)
    k = proj(KA, HEADS * DK)
    zp_ref[:, PKT:PKT + HEADS * DK] = (k * jnp.exp(-b)).astype(BF16)
    zp_ref[:, PKD:PKD + HEADS * DK] = (k * jnp.exp(bl - b)).astype(BF16)

    half = HEADS * DV // 2
    for j in range(2):
        zp_ref[:, PVA + j * half:PVA + (j + 1) * half] = proj(VA + j * half, half).astype(BF16)
        g = proj(GA + j * half, half)
        zp_ref[:, PGA + j * half:PGA + (j + 1) * half] = (g * jax.nn.sigmoid(g)).astype(BF16)
        zp_ref[:, PVB + j * half:PVB + (j + 1) * half] = proj(VB + j * half, half).astype(BF16)
        g = proj(GB + j * half, half)
        zp_ref[:, PGB + j * half:PGB + (j + 1) * half] = (g * jax.nn.sigmoid(g)).astype(BF16)

    cs = cs_ref[...]
    sn = sn_ref[...]
    qb = proj(QB, HEADS * DK)
    kb = proj(KB, HEADS * DK)
    steps_left = (CHUNK - 1 - (lax.broadcasted_iota(jnp.int32, (tm, 1), 0) % CHUNK)).astype(F32)
    for hd in range(HEADS):
        sl = slice(hd * DK, (hd + 1) * DK)
        zp_ref[:, PQB + hd * DK:PQB + (hd + 1) * DK] = _rotate(qb[:, sl], cs, sn).astype(BF16)
        kr = _rotate(kb[:, sl], cs, sn) * (DK ** -0.5)
        zp_ref[:, PKB + hd * DK:PKB + (hd + 1) * DK] = kr.astype(BF16)
        zp_ref[:, PKDB + hd * DK:PKDB + (hd + 1) * DK] = (kr * jnp.exp(_ret_log_decay(hd) * steps_left)).astype(BF16)

    for c0 in range(0, ZM_W, 512):
        zm_ref[:, c0:c0 + 512] = jax.nn.sigmoid(_dot(h, wm_ref[:, c0:c0 + 512])).astype(BF16)


def _prep(x, nw, ws, wm, wup, ba, cs, sn):
    T = x.shape[0]
    tm = min(ROW_TILE, T)
    assert T % tm == 0 and cs.shape[0] % tm == 0
    n_tab = cs.shape[0] // tm
    n = tm // CHUNK
    r = jnp.arange(tm)
    tril = ((r[:, None] >= r[None, :]) & (r[:, None] // CHUNK == r[None, :] // CHUNK)).astype(BF16)
    const = dict(pipeline_mode=pl.Buffered(1))
    full = lambda shape, **kw: pl.BlockSpec(shape, lambda i: (0,) * len(shape), **kw)
    return pl.pallas_call(
        _prep_kernel,
        grid=(T // tm,),
        in_specs=[pl.BlockSpec((tm, D_MODEL), lambda i: (i, 0)),
                  full((1, D_MODEL)),
                  full((D_MODEL, ZS_W), **const), full((D_MODEL, ZM_W), **const),
                  full((LANES, HEADS * DK)), full((1, HEADS * DK)), full((tm, tm)),
                  pl.BlockSpec((tm, DK), lambda i: (i % n_tab, 0)),
                  pl.BlockSpec((tm, DK), lambda i: (i % n_tab, 0))],
        out_specs=[pl.BlockSpec((tm, PZ_W), lambda i: (i, 0)),
                   pl.BlockSpec((tm, ZM_W), lambda i: (i, 0)),
                   pl.BlockSpec((n, 1, HEADS * DK), lambda i: (i, 0, 0)),
                   pl.BlockSpec((1, 1, LANES), lambda i: (i, 0, 0))],
        out_shape=[jax.ShapeDtypeStruct((T, PZ_W), BF16), jax.ShapeDtypeStruct((T, ZM_W), BF16),
                   jax.ShapeDtypeStruct((T // CHUNK, 1, HEADS * DK), F32),
                   jax.ShapeDtypeStruct((T // tm, 1, LANES), F32)],
        compiler_params=_params(1),
        name="prep",
    )(x, nw, ws, wm, wup, ba, tril, cs, sn)


def _inproj_kernel(x_ref, nw_ref, ws_ref, zs_ref):
    h = _rms(x_ref[...], nw_ref[...]).astype(BF16)
    step = 512
    for c0 in range(0, ZS_W, step):
        c1 = min(c0 + step, ZS_W)
        zs_ref[:, c0:c1] = _dot(h, ws_ref[:, c0:c1]).astype(BF16)


def _inproj(x, nw, ws):
    T = x.shape[0]
    tm = min(ROW_TILE, T)
    assert T % tm == 0
    return pl.pallas_call(
        _inproj_kernel,
        grid=(T // tm,),
        in_specs=[pl.BlockSpec((tm, D_MODEL), lambda i: (i, 0)),
                  pl.BlockSpec((1, D_MODEL), lambda i: (0, 0)),
                  pl.BlockSpec((D_MODEL, ZS_W), lambda i: (0, 0), pipeline_mode=pl.Buffered(1))],
        out_specs=pl.BlockSpec((tm, ZS_W), lambda i: (i, 0)),
        out_shape=jax.ShapeDtypeStruct((T, ZS_W), BF16),
        compiler_params=_params(1),
        name="inproj",
    )(x, nw, ws)


def _ret_log_decay(head):
    return math.log(1.0 - 2.0 ** (-5.0 - head))


NT_DIMS = (((1,), (1,)), ((), ()))
TN_DIMS = (((0,), (0,)), ((), ()))


def _scan_fast_kernel(z_ref, dec_ref, sg0_ref, sr0_ref, gnw_ref, rnw_ref,
                      ua_ref, ub_ref, sg_out_ref, sr_out_ref, sg_scr, sr_scr, *, n_chunks):
    c = pl.program_id(1)
    L = CHUNK

    @pl.when(c == 0)
    def _():
        sg_scr[...] = sg0_ref[0]
        sr_scr[...] = sr0_ref[0]

    row = lax.broadcasted_iota(jnp.int32, (L, L), 0)
    col = lax.broadcasted_iota(jnp.int32, (L, L), 1)
    causal = row >= col
    rel = (row - col).astype(F32)
    t_col = lax.broadcasted_iota(jnp.int32, (L, 1), 0).astype(F32)
    dec = dec_ref[0]

    def nt(a, b):
        return lax.dot_general(a, b, NT_DIMS, preferred_element_type=F32)

    def tn(a, b):
        return lax.dot_general(a, b, TN_DIMS, preferred_element_type=F32)

    for h in range(HEADS):
        qt = z_ref[:, PQT + h * DK:PQT + (h + 1) * DK]
        kt = z_ref[:, PKT + h * DK:PKT + (h + 1) * DK]
        kd = z_ref[:, PKD + h * DK:PKD + (h + 1) * DK]
        v = z_ref[:, PVA + h * DV:PVA + (h + 1) * DV]
        s = jnp.where(causal, nt(qt, kt), 0.0).astype(BF16)
        st_old = sg_scr[h]
        o = _dot(s, v) + nt(qt, st_old.astype(BF16))
        sg_scr[h] = dec[:, h * DK:(h + 1) * DK] * st_old + tn(v, kd)
        g = z_ref[:, PGA + h * DV:PGA + (h + 1) * DV].astype(F32)
        ua_ref[:, h * DV:(h + 1) * DV] = (_rms(o, gnw_ref[...]) * g).astype(BF16)

    for h in range(HEADS):
        lg = _ret_log_decay(h)
        qr = z_ref[:, PQB + h * DK:PQB + (h + 1) * DK]
        kr = z_ref[:, PKB + h * DK:PKB + (h + 1) * DK]
        kd = z_ref[:, PKDB + h * DK:PKDB + (h + 1) * DK]
        v = z_ref[:, PVB + h * DV:PVB + (h + 1) * DV]
        decay = jnp.exp(jnp.where(causal, lg * rel, -jnp.inf))
        s = (nt(qr, kr) * decay).astype(BF16)
        st_old = sr_scr[h]
        inner = jnp.exp(lg * (t_col + 1.0))
        o = _dot(s, v) + inner * nt(qr, st_old.astype(BF16))
        sr_scr[h] = math.exp(lg * L) * st_old + tn(v, kd)
        g = z_ref[:, PGB + h * DV:PGB + (h + 1) * DV].astype(F32)
        ub_ref[:, h * DV:(h + 1) * DV] = (_rms(o, rnw_ref[...]) * g).astype(BF16)

    @pl.when(c == n_chunks - 1)
    def _():
        sg_out_ref[0] = sg_scr[...]
        sr_out_ref[0] = sr_scr[...]


def _scan_fast(zp, dec, sg0t, sr0t, gnw, rnw, batch):
    T = zp.shape[0]
    n_chunks = T // (batch * CHUNK)
    state_spec = pl.BlockSpec((1, HEADS, DV, DK), lambda b, c: (b, 0, 0, 0))
    full = lambda shape: pl.BlockSpec(shape, lambda b, c: (0,) * len(shape))
    rows = lambda w: pl.BlockSpec((CHUNK, w), lambda b, c: (b * n_chunks + c, 0))
    return pl.pallas_call(
        functools.partial(_scan_fast_kernel, n_chunks=n_chunks),
        grid=(batch, n_chunks),
        in_specs=[rows(PZ_W),
                  pl.BlockSpec((1, 1, HEADS * DK), lambda b, c: (b * n_chunks + c, 0, 0)),
                  state_spec, state_spec, full((1, DV)), full((1, DV))],
        out_specs=[rows(HEADS * DV), rows(HEADS * DV), state_spec, state_spec],
        out_shape=[jax.ShapeDtypeStruct((T, HEADS * DV), BF16), jax.ShapeDtypeStruct((T, HEADS * DV), BF16),
                   jax.ShapeDtypeStruct((batch, HEADS, DV, DK), F32),
                   jax.ShapeDtypeStruct((batch, HEADS, DV, DK), F32)],
        scratch_shapes=[pltpu.VMEM((HEADS, DV, DK), F32), pltpu.VMEM((HEADS, DV, DK), F32)],
        compiler_params=_params(2),
        name="scanfast",
    )(zp, dec, sg0t, sr0t, gnw, rnw)


def _scan_kernel(z_ref, cs_ref, sn_ref, sg0_ref, sr0_ref, wup_ref, ba_ref, gnw_ref, rnw_ref,
                 ua_ref, ub_ref, sg_out_ref, sr_out_ref, sg_scr, sr_scr, k_scr, b_scr, *, n_chunks):
    c = pl.program_id(1)
    L = CHUNK

    @pl.when(c == 0)
    def _():
        sg_scr[...] = sg0_ref[0]
        sr_scr[...] = sr0_ref[0]

    row = lax.broadcasted_iota(jnp.int32, (L, L), 0)
    col = lax.broadcasted_iota(jnp.int32, (L, L), 1)
    causal = row >= col
    tril = causal.astype(BF16)

    la = _log_sigmoid(_dot(z_ref[:, RA:RA + LANES], wup_ref[...]) + ba_ref[...]) * (1.0 / GLA_TAU)
    hi, mid, lo = _split3(la)
    b_all = _dot(tril, hi) + _dot(tril, mid) + _dot(tril, lo)

    for h in range(HEADS):
        bh = b_all[:, h * DK:(h + 1) * DK]
        q = z_ref[:, QA + h * DK:QA + (h + 1) * DK].astype(F32) * (DK ** -0.5)
        k = z_ref[:, KA + h * DK:KA + (h + 1) * DK].astype(F32)
        v = z_ref[:, VA + h * DV:VA + (h + 1) * DV]
        qt = (q * jnp.exp(bh)).astype(BF16)
        kT = k.T
        bT = bh.T
        blT = bT[:, L - 1:L]
        kdT = (kT * jnp.exp(blT - bT)).astype(BF16)
        k_scr[...] = k
        b_scr[...] = bh

        def score_column(j, acc, q=q, bh=bh):
            e = jnp.exp(jnp.minimum(bh - b_scr[pl.ds(j, 1), :], 0.0))
            colv = jnp.sum(q * k_scr[pl.ds(j, 1), :] * e, axis=1, keepdims=True)
            return jnp.where(col == j, colv, acc)

        s = lax.fori_loop(0, L, score_column, jnp.zeros((L, L), F32))
        s = jnp.where(causal, s, 0.0).astype(BF16)
        s_old = sg_scr[h]
        o = _dot(s, v) + _dot(qt, s_old.astype(BF16))
        sg_scr[h] = jnp.exp(blT) * s_old + _dot(kdT, v)
        g = z_ref[:, GA + h * DV:GA + (h + 1) * DV].astype(F32)
        ua_ref[:, h * DV:(h + 1) * DV] = (_rms(o, gnw_ref[...]) * (g * jax.nn.sigmoid(g))).astype(BF16)

    cs = cs_ref[...]
    sn = sn_ref[...]
    rel = (row - col).astype(F32)
    t_col = lax.broadcasted_iota(jnp.int32, (L, 1), 0).astype(F32)
    s_row = lax.broadcasted_iota(jnp.int32, (1, L), 1).astype(F32)
    for h in range(HEADS):
        lg = _ret_log_decay(h)
        q = z_ref[:, QB + h * DK:QB + (h + 1) * DK].astype(F32)
        k = z_ref[:, KB + h * DK:KB + (h + 1) * DK].astype(F32)
        v = z_ref[:, VB + h * DV:VB + (h + 1) * DV]
        qr = _rotate(q, cs, sn).astype(BF16)
        kr = _rotate(k, cs, sn) * (DK ** -0.5)
        krT = kr.T
        decay = jnp.exp(jnp.where(causal, lg * rel, -jnp.inf))
        s = (_dot(qr, krT.astype(BF16)) * decay).astype(BF16)
        s_old = sr_scr[h]
        inner = jnp.exp(lg * (t_col + 1.0))
        o = _dot(s, v) + inner * _dot(qr, s_old.astype(BF16))
        kdT = (krT * jnp.exp(lg * (L - 1.0 - s_row))).astype(BF16)
        sr_scr[h] = math.exp(lg * L) * s_old + _dot(kdT, v)
        g = z_ref[:, GB + h * DV:GB + (h + 1) * DV].astype(F32)
        ub_ref[:, h * DV:(h + 1) * DV] = (_rms(o, rnw_ref[...]) * (g * jax.nn.sigmoid(g))).astype(BF16)

    @pl.when(c == n_chunks - 1)
    def _():
        sg_out_ref[0] = sg_scr[...]
        sr_out_ref[0] = sr_scr[...]


def _scan(zs, cs, sn, sg0, sr0, wup, ba, gnw, rnw, batch):
    T = zs.shape[0]
    n_chunks = T // (batch * CHUNK)
    state_spec = pl.BlockSpec((1, HEADS, DK, DV), lambda b, c: (b, 0, 0, 0))
    full = lambda shape: pl.BlockSpec(shape, lambda b, c: (0,) * len(shape))
    return pl.pallas_call(
        functools.partial(_scan_kernel, n_chunks=n_chunks),
        grid=(batch, n_chunks),
        in_specs=[pl.BlockSpec((CHUNK, ZS_W), lambda b, c: (b * n_chunks + c, 0)),
                  pl.BlockSpec((CHUNK, DK), lambda b, c: (c, 0)),
                  pl.BlockSpec((CHUNK, DK), lambda b, c: (c, 0)),
                  state_spec, state_spec,
                  full((LANES, HEADS * DK)), full((1, HEADS * DK)), full((1, DV)), full((1, DV))],
        out_specs=[pl.BlockSpec((CHUNK, HEADS * DV), lambda b, c: (b * n_chunks + c, 0)),
                   pl.BlockSpec((CHUNK, HEADS * DV), lambda b, c: (b * n_chunks + c, 0)),
                   state_spec, state_spec],
        out_shape=[jax.ShapeDtypeStruct((T, HEADS * DV), BF16), jax.ShapeDtypeStruct((T, HEADS * DV), BF16),
                   jax.ShapeDtypeStruct((batch, HEADS, DK, DV), F32),
                   jax.ShapeDtypeStruct((batch, HEADS, DK, DV), F32)],
        scratch_shapes=[pltpu.VMEM((HEADS, DK, DV), F32), pltpu.VMEM((HEADS, DK, DV), F32),
                        pltpu.VMEM((CHUNK, DK), F32), pltpu.VMEM((CHUNK, DK), F32)],
        compiler_params=_params(2),
        name="scan",
    )(zs, cs, sn, sg0, sr0, wup, ba, gnw, rnw)


def _post_kernel(ua_ref, ub_ref, zm_ref, x_ref, wa_ref, wb_ref, wo_ref, nfw_ref, wr_ref, br_ref, cnt0_ref,
                 earlier_ref, x1_ref, e_ref, g_ref, rank_ref, cnt_ref):
    @pl.when(pl.program_id(0) == 0)
    def _():
        cnt_ref[...] = cnt0_ref[...]

    a = _dot(ua_ref[...], wa_ref[...])
    b = _dot(ub_ref[...], wb_ref[...])
    merged = zm_ref[:, :D_MODEL].astype(F32) * a + zm_ref[:, D_MODEL:].astype(F32) * b
    x1 = x_ref[...] + _dot(merged.astype(BF16), wo_ref[...])
    x1_ref[...] = x1

    h2 = _rms(x1, nfw_ref[...])
    h_hi, h_mid, _ = _split3(h2)
    w_hi, w_mid = wr_ref[0], wr_ref[1]
    logits = _dot(h_hi, w_hi) + _dot(h_hi, w_mid) + _dot(h_mid, w_hi) + br_ref[...]
    lane = lax.broadcasted_iota(jnp.int32, logits.shape, 1)
    lane_f = lane.astype(F32)
    work = jnp.where(lane < N_EXPERTS, logits, -jnp.inf)
    vals, idxs = [], []
    for _ in range(TOP_K):
        m = jnp.max(work, axis=1, keepdims=True)
        idx = jnp.min(jnp.where(work == m, lane_f, float(LANES)), axis=1, keepdims=True)
        vals.append(m)
        idxs.append(idx)
        work = jnp.where(lane_f == idx, -jnp.inf, work)
    ps = [jnp.exp(v - vals[0]) for v in vals]
    den = ps[0] + ps[1] + ps[2] + ps[3]
    e_out = jnp.zeros(logits.shape, F32)
    g_out = jnp.zeros(logits.shape, F32)
    for k in range(TOP_K):
        e_out = jnp.where(lane == k, idxs[k], e_out)
        g_out = jnp.where(lane == k, ps[k] / den, g_out)
    e_ref[...] = e_out.astype(jnp.int32)
    g_ref[...] = g_out

    multi_hot = jnp.zeros(logits.shape, F32)
    for k in range(TOP_K):
        multi_hot = multi_hot + (lane_f == idxs[k]).astype(F32)
    before = _dot(earlier_ref[...], multi_hot.astype(BF16)) + cnt_ref[...]
    r_out = jnp.zeros(logits.shape, F32)
    for k in range(TOP_K):
        rk = jnp.sum(jnp.where(lane_f == idxs[k], before, 0.0), axis=1, keepdims=True)
        r_out = jnp.where(lane == k, rk, r_out)
    rank_ref[...] = r_out.astype(jnp.int32)
    cnt_ref[...] += jnp.sum(multi_hot, axis=0, keepdims=True)


def _post(ua, ub, zm, x, wa, wb, wo, nfw, wr, br, cnt0):
    T = x.shape[0]
    tm = min(ROW_TILE, T)
    assert T % tm == 0
    rows = lambda w: pl.BlockSpec((tm, w), lambda i: (i, 0))
    full = lambda shape: pl.BlockSpec(shape, lambda i: (0,) * len(shape))
    earlier = (jnp.arange(tm)[:, None] > jnp.arange(tm)[None, :]).astype(BF16)
    return pl.pallas_call(
        _post_kernel,
        grid=(T // tm,),
        in_specs=[rows(HEADS * DV), rows(HEADS * DV), rows(ZM_W), rows(D_MODEL),
                  full((HEADS * DV, D_MODEL)), full((HEADS * DV, D_MODEL)), full((D_MODEL, D_MODEL)),
                  full((1, D_MODEL)), full((2, D_MODEL, LANES)), full((1, LANES)), full((1, LANES)),
                  full((tm, tm))],
        out_specs=[rows(D_MODEL), rows(LANES), rows(LANES), rows(LANES), full((1, LANES))],
        out_shape=[jax.ShapeDtypeStruct((T, D_MODEL), F32), jax.ShapeDtypeStruct((T, LANES), jnp.int32),
                   jax.ShapeDtypeStruct((T, LANES), F32), jax.ShapeDtypeStruct((T, LANES), jnp.int32),
                   jax.ShapeDtypeStruct((1, LANES), F32)],
        compiler_params=_params(1),
        name="post",
    )(ua, ub, zm, x, wa, wb, wo, nfw, wr, br, cnt0, earlier)


ROW_SUBLANES = D_MODEL // LANES


def _row_tile(r):
    return pl.ds(pl.multiple_of(r * ROW_SUBLANES, ROW_SUBLANES), ROW_SUBLANES)


def _load_row_tiles(ref, n_rows):
    return [ref[pl.ds(c, n_rows, stride=ROW_SUBLANES), :] for c in range(ROW_SUBLANES)]


def _store_row_tiles(ref, x):
    for c in range(ROW_SUBLANES):
        ref[pl.ds(c, x.shape[0], stride=ROW_SUBLANES), :] = x[:, c * LANES:(c + 1) * LANES]


def _row_copy_wait(src_rows, dst_rows, sem):
    pltpu.make_async_copy(src_rows, dst_rows, sem).wait()


def _dispatch_kernel(dest_ref, xp_ref, xs_ref, nfw_ref, rows_ref, hbuf, sem, *, n_prompt_tiles, n_tiles):
    tm = hbuf.shape[1] // ROW_SUBLANES
    i = pl.program_id(0)
    slot = i % 2

    def wait_slot(s):
        for _ in range(TOP_K):
            _row_copy_wait(hbuf.at[s], rows_ref.at[pl.ds(0, tm * ROW_SUBLANES)], sem.at[s])

    @pl.when(i >= 2)
    def _():
        wait_slot(slot)

    @pl.when(i < n_prompt_tiles)
    def _():
        _store_row_tiles(hbuf.at[slot], _rms(xp_ref[...], nfw_ref[...]))

    @pl.when(i >= n_prompt_tiles)
    def _():
        _store_row_tiles(hbuf.at[slot], _rms(xs_ref[...], nfw_ref[...]))

    def issue(r, carry):
        src = hbuf.at[slot, _row_tile(r)]
        for k in range(TOP_K):
            d = dest_ref[0, 0, r * TOP_K + k]
            pltpu.make_async_copy(src, rows_ref.at[_row_tile(d)], sem.at[slot]).start()
        return carry

    lax.fori_loop(0, tm, issue, 0, unroll=ISSUE_UNROLL)

    @pl.when(i == n_tiles - 1)
    def _():
        @pl.when(i >= 1)
        def _():
            wait_slot(1 - slot)
        wait_slot(slot)


def _dispatch(dest, x1_p, x1_s, nfw):
    Tp, Ts = x1_p.shape[0], x1_s.shape[0]
    tm = math.gcd(DISPATCH_TILE, Tp, Ts)
    assert tm % 8 == 0
    n_p, n_s = Tp // tm, Ts // tm
    return pl.pallas_call(
        functools.partial(_dispatch_kernel, n_prompt_tiles=n_p, n_tiles=n_p + n_s),
        grid=(n_p + n_s,),
        in_specs=[pl.BlockSpec((1, 1, tm * TOP_K), lambda i: (i, 0, 0), memory_space=pltpu.SMEM),
                  pl.BlockSpec((tm, D_MODEL), lambda i: (jnp.minimum(i, n_p - 1), 0)),
                  pl.BlockSpec((tm, D_MODEL), lambda i: (jnp.maximum(i - n_p, 0), 0)),
                  pl.BlockSpec((1, D_MODEL), lambda i: (0, 0))],
        out_specs=pl.BlockSpec(memory_space=pl.ANY),
        out_shape=jax.ShapeDtypeStruct(((Tp + Ts) * TOP_K * ROW_SUBLANES, LANES), F32),
        scratch_shapes=[pltpu.VMEM((2, tm * ROW_SUBLANES, LANES), F32), pltpu.SemaphoreType.DMA((2,))],
        compiler_params=_params(1),
        name="dispatch",
    )(dest.reshape(n_p + n_s, 1, tm * TOP_K), x1_p, x1_s, nfw)


def _expert_kernel(blk_ref, exp_ref, lo_ref, hi_ref, xs_ref, wgu_ref, bgu_ref, wd_ref, bd_ref, ys_ref):
    del blk_ref, exp_ref
    i = pl.program_id(0)
    lo, hi = lo_ref[i], hi_ref[i]

    @pl.when(hi > lo)
    def _():
        bm = xs_ref.shape[0] // ROW_SUBLANES
        x = jnp.concatenate([t.astype(BF16) for t in _load_row_tiles(xs_ref, bm)], axis=1)
        hgu = _dot(x, wgu_ref[0]) + bgu_ref[0]
        glu = jnp.minimum(hgu[:, :D_FF], SWIGLU_LIMIT)
        lin = jnp.clip(hgu[:, D_FF:], -SWIGLU_LIMIT, SWIGLU_LIMIT)
        act = glu * jax.nn.sigmoid(SWIGLU_ALPHA * glu) * (lin + 1.0)
        y = _dot(act.astype(BF16), wd_ref[0]) + bd_ref[0]

        @pl.when(lo == 0)
        def _():
            _store_row_tiles(ys_ref, y)

        @pl.when(lo > 0)
        def _():
            keep = lax.broadcasted_iota(jnp.int32, (bm, 1), 0) < lo
            old = jnp.concatenate(_load_row_tiles(ys_ref, bm), axis=1)
            _store_row_tiles(ys_ref, jnp.where(keep, old, y))


def _experts(items, xs, wgu, bgu, wd, bd):
    P = xs.shape[0] // ROW_SUBLANES
    bm = EXPERT_BLOCK
    n_items = items[0].shape[0]
    blk = lambda i, b, e, lo, hi: (b[i], 0)
    ex3 = lambda i, b, e, lo, hi: (e[i], 0, 0)
    return pl.pallas_call(
        _expert_kernel,
        grid_spec=pltpu.PrefetchScalarGridSpec(
            num_scalar_prefetch=4,
            grid=(n_items,),
            in_specs=[pl.BlockSpec((bm * ROW_SUBLANES, LANES), blk),
                      pl.BlockSpec((1, D_MODEL, 2 * D_FF), ex3),
                      pl.BlockSpec((1, 1, 2 * D_FF), ex3),
                      pl.BlockSpec((1, D_FF, D_MODEL), ex3),
                      pl.BlockSpec((1, 1, D_MODEL), ex3)],
            out_specs=pl.BlockSpec((bm * ROW_SUBLANES, LANES), blk)),
        out_shape=jax.ShapeDtypeStruct((P * ROW_SUBLANES, LANES), F32),
        compiler_params=_params(1),
        name="experts",
    )(*items, xs, wgu, bgu, wd, bd)


def _combine_kernel(dest_ref, dest_next_ref, gate_ref, x1_ref, nw_ref, ys_ref, out_ref, buf, sem, *, n_tiles):
    tm = x1_ref.shape[0]
    i = pl.program_id(0)
    slot = i % 2

    def issue_tile(idx_ref, s):
        def issue(r, carry):
            for k in range(TOP_K):
                d = idx_ref[0, 0, r * TOP_K + k]
                pltpu.make_async_copy(ys_ref.at[_row_tile(d)], buf.at[s, k, _row_tile(r)], sem.at[s]).start()
            return carry

        lax.fori_loop(0, tm, issue, 0, unroll=ISSUE_UNROLL)

    @pl.when(i == 0)
    def _():
        issue_tile(dest_ref, slot)

    @pl.when(i + 1 < n_tiles)
    def _():
        issue_tile(dest_next_ref, 1 - slot)

    for k in range(TOP_K):
        _row_copy_wait(ys_ref.at[pl.ds(0, tm * ROW_SUBLANES)], buf.at[slot, k], sem.at[slot])
    gate = gate_ref[...]
    chunks = [_load_row_tiles(buf.at[slot, k], tm) for k in range(TOP_K)]
    f = []
    for c in range(ROW_SUBLANES):
        fc = gate[:, 0:1] * chunks[0][c]
        for k in range(1, TOP_K):
            fc = fc + gate[:, k:k + 1] * chunks[k][c]
        f.append(fc)
    out_ref[...] = _rms(x1_ref[...] + jnp.concatenate(f, axis=1), nw_ref[...])


def _combine(dest, gate, x1, nw, ys):
    T = x1.shape[0]
    tm = min(COMBINE_TILE, T)
    assert T % tm == 0
    n = T // tm
    dest3 = dest.reshape(n, 1, tm * TOP_K)
    return pl.pallas_call(
        functools.partial(_combine_kernel, n_tiles=n),
        grid=(n,),
        in_specs=[pl.BlockSpec((1, 1, tm * TOP_K), lambda i: (i, 0, 0), memory_space=pltpu.SMEM),
                  pl.BlockSpec((1, 1, tm * TOP_K), lambda i: (jnp.minimum(i + 1, n - 1), 0, 0),
                               memory_space=pltpu.SMEM),
                  pl.BlockSpec((tm, LANES), lambda i: (i, 0)),
                  pl.BlockSpec((tm, D_MODEL), lambda i: (i, 0)),
                  pl.BlockSpec((1, D_MODEL), lambda i: (0, 0)),
                  pl.BlockSpec(memory_space=pl.ANY)],
        out_specs=pl.BlockSpec((tm, D_MODEL), lambda i: (i, 0)),
        out_shape=jax.ShapeDtypeStruct((T, D_MODEL), F32),
        scratch_shapes=[pltpu.VMEM((2, TOP_K, tm * ROW_SUBLANES, LANES), F32), pltpu.SemaphoreType.DMA((2,))],
        compiler_params=_params(1),
        name="combine",
    )(dest3, dest3, gate, x1, nw, ys)


def _rotary_tables(pos):
    inv = ROPE_BASE ** (-jnp.arange(0, DK, 2, dtype=F32) / DK)
    ang = pos.astype(F32)[:, None] * inv[None, :]
    cos, sin = jnp.cos(ang), jnp.sin(ang)
    return jnp.concatenate([cos, cos], axis=1), jnp.concatenate([-sin, sin], axis=1)


def _routing(top_e, rank, counts):
    bm = EXPERT_BLOCK
    n_rows = top_e.size
    n_blocks = n_rows // bm
    n_items = n_blocks + N_EXPERTS - 1
    start = jnp.cumsum(counts) - counts
    end = start + counts
    expert_ids = jnp.arange(N_EXPERTS, dtype=jnp.int32)
    onehot = top_e[:, :, None] == expert_ids[None, None, :]
    dest = (jnp.sum(jnp.where(onehot, start[None, None, :], 0), axis=2) + rank).astype(jnp.int32)

    first_blk = start // bm
    n_it = jnp.where(counts > 0, (end - 1) // bm - first_blk + 1, 0)
    it_end = jnp.cumsum(n_it)
    it_start = it_end - n_it
    j = jnp.arange(n_items, dtype=jnp.int32)
    e_j = jnp.minimum(jnp.sum(j[:, None] >= it_end[None, :], axis=1), N_EXPERTS - 1).astype(jnp.int32)
    valid = j < it_end[-1]
    last_e = jnp.max(jnp.where(counts > 0, expert_ids, 0))
    e_j = jnp.where(valid, e_j, last_e)
    blk = jnp.where(valid, first_blk[e_j] + (j - it_start[e_j]), n_blocks - 1)
    lo = jnp.maximum(start[e_j], blk * bm) - blk * bm
    hi = jnp.minimum(end[e_j], (blk + 1) * bm) - blk * bm
    lo = jnp.where(valid, lo, 0)
    hi = jnp.where(valid, hi, 0)
    items = tuple(a.astype(jnp.int32) for a in (blk, e_j, lo, hi))
    return dest, items


def kernel(x_prompt, x_sample, state_gla, state_ret, meta_tokens, norm_mix_w, w_in, w_alpha_up, b_alpha, gla_norm_w, ret_norm_w, w_branch_gla, w_branch_ret, w_out, norm_ffn_w, w_router, b_router, w_gate_up, b_gate_up, w_down, b_down, norm_final_w):
    B, S, D = x_prompt.shape
    DB, DS, _ = x_sample.shape
    assert D == D_MODEL and S % CHUNK == 0 and DS == CHUNK and state_gla.shape[0] == 1
    l = 0

    offs = [0]
    for w in IN_WIDTHS:
        offs.append(offs[-1] + w)
    cols = [w_in[l][:, offs[i]:offs[i + 1]] for i in range(len(IN_WIDTHS))]
    qa, ka, va, ga, ra, qb, kb, vb, gb, m = cols
    ra = jnp.pad(ra, ((0, 0), (0, LANES - GLA_RANK)))
    ws = jnp.concatenate([qa, ka, va, ga, qb, kb, vb, gb, ra], axis=1).astype(BF16)
    wm = m.astype(BF16)
    wup = jnp.pad(w_alpha_up[l], ((0, LANES - GLA_RANK), (0, 0))).astype(BF16)
    ba = b_alpha[l][None, :]
    nmw = norm_mix_w[l][None, :]
    gnw = gla_norm_w[l][None, :]
    rnw = ret_norm_w[l][None, :]

    def mixer(x2d, pos, sg0, sr0, batch):
        T = x2d.shape[0]
        cs, sn = _rotary_tables(pos)
        reps = max(1, min(ROW_TILE, T) // pos.shape[0])
        zp, zm, dec, bmin = _prep(x2d, nmw, ws, wm, wup, ba, jnp.tile(cs, (reps, 1)), jnp.tile(sn, (reps, 1)))

        def ratio_form():
            ua, ub, sgt, srt = _scan_fast(zp, dec, jnp.swapaxes(sg0, 2, 3), jnp.swapaxes(sr0, 2, 3), gnw, rnw, batch)
            return ua, ub, jnp.swapaxes(sgt, 2, 3), jnp.swapaxes(srt, 2, 3)

        def exact_form():
            return tuple(_scan(_inproj(x2d, nmw, ws), cs, sn, sg0, sr0, wup, ba, gnw, rnw, batch))

        ua, ub, sg, sr = lax.cond(jnp.min(bmin) >= MIN_SAFE_LOG_FORGET, ratio_form, exact_form)
        return ua, ub, zm, sg, sr

    x_meta = jnp.concatenate([jnp.zeros((CHUNK - N_META, D), F32), meta_tokens.astype(F32)], axis=0)
    pos_meta = jnp.maximum(jnp.arange(CHUNK) - (CHUNK - N_META), 0)
    zero_state = jnp.zeros((1, HEADS, DK, DV), F32)
    _, _, _, sg_m, sr_m = mixer(x_meta, pos_meta, zero_state, zero_state, 1)

    xp = x_prompt.reshape(B * S, D)
    xs_ = x_sample.reshape(DB * DS, D)
    sg0 = jnp.broadcast_to(sg_m, (B, HEADS, DK, DV))
    sr0 = jnp.broadcast_to(sr_m, (B, HEADS, DK, DV))
    ua_p, ub_p, zm_p, sg_p, sr_p = mixer(xp, N_META + jnp.arange(S), sg0, sr0, B)
    ua_s, ub_s, zm_s, sg_s, sr_s = mixer(xs_, N_META + PAST_LEN + jnp.arange(DS),
                                         state_gla[l].astype(F32), state_ret[l].astype(F32), DB)

    wa = w_branch_gla[l].astype(BF16)
    wb = w_branch_ret[l].astype(BF16)
    wo = w_out[l].astype(BF16)
    nfw = norm_ffn_w[l][None, :]
    wr = jnp.pad(w_router[l].astype(F32), ((0, 0), (0, LANES - N_EXPERTS)))
    wr_hi = wr.astype(BF16)
    wr_mid = (wr - wr_hi.astype(F32)).astype(BF16)
    wr2 = jnp.stack([wr_hi, wr_mid])
    br = jnp.pad(b_router[l].astype(F32), (0, LANES - N_EXPERTS))[None, :]
    cnt0 = jnp.zeros((1, LANES), F32)
    x1_p, e_p, g_p, r_p, cnt_p = _post(ua_p, ub_p, zm_p, xp, wa, wb, wo, nfw, wr2, br, cnt0)
    x1_s, e_s, g_s, r_s, cnt = _post(ua_s, ub_s, zm_s, xs_, wa, wb, wo, nfw, wr2, br, cnt_p)

    Tp, Ts = B * S, DB * DS
    assert ((Tp + Ts) * TOP_K) % EXPERT_BLOCK == 0
    top_e = jnp.concatenate([e_p[:, :TOP_K], e_s[:, :TOP_K]], axis=0)
    rank = jnp.concatenate([r_p[:, :TOP_K], r_s[:, :TOP_K]], axis=0)
    dest, items = _routing(top_e, rank, cnt[0, :N_EXPERTS].astype(jnp.int32))
    dest = dest.reshape(-1)
    dest_p, dest_s = dest[:Tp * TOP_K], dest[Tp * TOP_K:]

    xs_rows = _dispatch(dest, x1_p, x1_s, nfw)
    ys_rows = _experts(items, xs_rows, w_gate_up[l].astype(BF16), b_gate_up[l][:, None, :],
                       w_down[l].astype(BF16), b_down[l][:, None, :])
    nw_final = norm_final_w[None, :]
    y_p = _combine(dest_p, g_p, x1_p, nw_final, ys_rows)
    y_s = _combine(dest_s, g_s, x1_s, nw_final, ys_rows)

    dt = state_gla.dtype
    return (y_p.reshape(B, S, D), y_s.reshape(DB, DS, D),
            sg_p[None].astype(dt), sr_p[None].astype(state_ret.dtype),
            sg_s[None].astype(dt), sr_s[None].astype(state_ret.dtype))
```

```python
import functools
import math

import jax
import jax.numpy as jnp
from jax import lax
from jax.experimental import pallas as pl
from jax.experimental.pallas import tpu as pltpu

F32 = jnp.float32
BF16 = jnp.bfloat16

D_MODEL = 1024
CHUNK = 64
N_META = 16
PAST_LEN = 4096
EPS = 1e-5
HEADS = 4
DK = 128
DV = 256
GLA_RANK = 16
GLA_TAU = 16.0
ROPE_BASE = 10000.0
N_EXPERTS = 32
TOP_K = 4
D_FF = 1024
SWIGLU_ALPHA = 1.702
SWIGLU_LIMIT = 7.0
IN_WIDTHS = (HEADS * DK, HEADS * DK, HEADS * DV, HEADS * DV, GLA_RANK,
             HEADS * DK, HEADS * DK, HEADS * DV, HEADS * DV, 2 * D_MODEL)

LANES = 128
QA, KA, VA, GA, QB, KB, VB, GB, RA = 0, 512, 1024, 2048, 3072, 3584, 4096, 5120, 6144
ZS_W = RA + LANES
ZM_W = 2 * D_MODEL
PQT, PKT, PKD, PVA, PGA, PQB, PKB, PKDB, PVB, PGB = 0, 512, 1024, 1536, 2560, 3584, 4096, 4608, 5120, 6144
PZ_W = PGB + HEADS * DV
MIN_SAFE_LOG_FORGET = -60.0

ROW_TILE = 512
DISPATCH_TILE = 256
COMBINE_TILE = 128
EXPERT_BLOCK = 512
ISSUE_UNROLL = 4
SCAN_CHUNKS_PER_STEP = 4
VMEM_LIMIT = 56 * 1024 * 1024


def _params(n_axes):
    return pltpu.CompilerParams(dimension_semantics=("arbitrary",) * n_axes, vmem_limit_bytes=VMEM_LIMIT)


def _rms(x, w):
    return x * lax.rsqrt(jnp.mean(x * x, axis=-1, keepdims=True) + EPS) * w


def _dot(a, b):
    return jnp.dot(a, b, preferred_element_type=F32)


def _split3(x):
    hi = x.astype(BF16)
    r1 = x - hi.astype(F32)
    mid = r1.astype(BF16)
    lo = (r1 - mid.astype(F32)).astype(BF16)
    return hi, mid, lo


def _log_sigmoid(x):
    return jnp.minimum(x, 0.0) - jnp.log1p(jnp.exp(-jnp.abs(x)))


def _rotate(t, cs, sn):
    return t * cs + pltpu.roll(t, DK // 2, axis=1) * sn


def _prep_kernel(x_ref, nw_ref, ws_ref, wm_ref, wup_ref, ba_ref, tril_ref, cs_ref, sn_ref,
                 zp_ref, zm_ref, dec_ref, bmin_ref):
    tm = x_ref.shape[0]
    n = tm // CHUNK
    h = _rms(x_ref[...], nw_ref[...]).astype(BF16)

    def proj(c0, w):
        return _dot(h, ws_ref[:, c0:c0 + w])

    ra = proj(RA, LANES).astype(BF16)
    la = _log_sigmoid(_dot(ra, wup_ref[...]) + ba_ref[...]) * (1.0 / GLA_TAU)
    hi, mid, lo = _split3(la)
    tril = tril_ref[...]
    b = _dot(tril, hi) + _dot(tril, mid) + _dot(tril, lo)
    b3 = b.reshape(n, CHUNK, HEADS * DK)
    bl3 = b3[:, CHUNK - 1:CHUNK, :]
    dec_ref[...] = jnp.exp(bl3)
    bl = jnp.broadcast_to(bl3, b3.shape).reshape(tm, HEADS * DK)
    bmin = jnp.min(jnp.min(b, axis=0, keepdims=True), axis=1, keepdims=True)
    bmin_ref[0] = jnp.broadcast_to(bmin, (1, LANES))

    q = proj(QA, HEADS * DK) * (DK ** -0.5)
    zp_ref[:, PQT:PQT + HEADS * DK] = (q * jnp.exp(b)).astype(BF16)
    k = proj(KA, HEADS * DK)
    zp_ref[:, PKT:PKT + HEADS * DK] = (k * jnp.exp(-b)).astype(BF16)
    zp_ref[:, PKD:PKD + HEADS * DK] = (k * jnp.exp(bl - b)).astype(BF16)

    half = HEADS * DV // 2
    for j in range(2):
        zp_ref[:, PVA + j * half:PVA + (j + 1) * half] = proj(VA + j * half, half).astype(BF16)
        g = proj(GA + j * half, half)
        zp_ref[:, PGA + j * half:PGA + (j + 1) * half] = (g * jax.nn.sigmoid(g)).astype(BF16)
        zp_ref[:, PVB + j * half:PVB + (j + 1) * half] = proj(VB + j * half, half).astype(BF16)
        g = proj(GB + j * half, half)
        zp_ref[:, PGB + j * half:PGB + (j + 1) * half] = (g * jax.nn.sigmoid(g)).astype(BF16)

    cs = cs_ref[...]
    sn = sn_ref[...]
    qb = proj(QB, HEADS * DK)
    kb = proj(KB, HEADS * DK)
    steps_left = (CHUNK - 1 - (lax.broadcasted_iota(jnp.int32, (tm, 1), 0) % CHUNK)).astype(F32)
    for hd in range(HEADS):
        sl = slice(hd * DK, (hd + 1) * DK)
        zp_ref[:, PQB + hd * DK:PQB + (hd + 1) * DK] = _rotate(qb[:, sl], cs, sn).astype(BF16)
        kr = _rotate(kb[:, sl], cs, sn) * (DK ** -0.5)
        zp_ref[:, PKB + hd * DK:PKB + (hd + 1) * DK] = kr.astype(BF16)
        zp_ref[:, PKDB + hd * DK:PKDB + (hd + 1) * DK] = (kr * jnp.exp(_ret_log_decay(hd) * steps_left)).astype(BF16)

    for c0 in range(0, ZM_W, 512):
        zm_ref[:, c0:c0 + 512] = jax.nn.sigmoid(_dot(h, wm_ref[:, c0:c0 + 512])).astype(BF16)


def _prep(x, nw, ws, wm, wup, ba, cs, sn):
    T = x.shape[0]
    tm = min(ROW_TILE, T)
    assert T % tm == 0 and cs.shape[0] % tm == 0
    n_tab = cs.shape[0] // tm
    n = tm // CHUNK
    r = jnp.arange(tm)
    tril = ((r[:, None] >= r[None, :]) & (r[:, None] // CHUNK == r[None, :] // CHUNK)).astype(BF16)
    const = dict(pipeline_mode=pl.Buffered(1))
    full = lambda shape, **kw: pl.BlockSpec(shape, lambda i: (0,) * len(shape), **kw)
    return pl.pallas_call(
        _prep_kernel,
        grid=(T // tm,),
        in_specs=[pl.BlockSpec((tm, D_MODEL), lambda i: (i, 0)),
                  full((1, D_MODEL)),
                  full((D_MODEL, ZS_W), **const), full((D_MODEL, ZM_W), **const),
                  full((LANES, HEADS * DK)), full((1, HEADS * DK)), full((tm, tm)),
                  pl.BlockSpec((tm, DK), lambda i: (i % n_tab, 0)),
                  pl.BlockSpec((tm, DK), lambda i: (i % n_tab, 0))],
        out_specs=[pl.BlockSpec((tm, PZ_W), lambda i: (i, 0)),
                   pl.BlockSpec((tm, ZM_W), lambda i: (i, 0)),
                   pl.BlockSpec((n, 1, HEADS * DK), lambda i: (i, 0, 0)),
                   pl.BlockSpec((1, 1, LANES), lambda i: (i, 0, 0))],
        out_shape=[jax.ShapeDtypeStruct((T, PZ_W), BF16), jax.ShapeDtypeStruct((T, ZM_W), BF16),
                   jax.ShapeDtypeStruct((T // CHUNK, 1, HEADS * DK), F32),
                   jax.ShapeDtypeStruct((T // tm, 1, LANES), F32)],
        compiler_params=_params(1),
        name="prep",
    )(x, nw, ws, wm, wup, ba, tril, cs, sn)


def _inproj_kernel(x_ref, nw_ref, ws_ref, zs_ref):
    h = _rms(x_ref[...], nw_ref[...]).astype(BF16)
    step = 512
    for c0 in range(0, ZS_W, step):
        c1 = min(c0 + step, ZS_W)
        zs_ref[:, c0:c1] = _dot(h, ws_ref[:, c0:c1]).astype(BF16)


def _inproj(x, nw, ws):
    T = x.shape[0]
    tm = min(ROW_TILE, T)
    assert T % tm == 0
    return pl.pallas_call(
        _inproj_kernel,
        grid=(T // tm,),
        in_specs=[pl.BlockSpec((tm, D_MODEL), lambda i: (i, 0)),
                  pl.BlockSpec((1, D_MODEL), lambda i: (0, 0)),
                  pl.BlockSpec((D_MODEL, ZS_W), lambda i: (0, 0), pipeline_mode=pl.Buffered(1))],
        out_specs=pl.BlockSpec((tm, ZS_W), lambda i: (i, 0)),
        out_shape=jax.ShapeDtypeStruct((T, ZS_W), BF16),
        compiler_params=_params(1),
        name="inproj",
    )(x, nw, ws)


def _ret_log_decay(head):
    return math.log(1.0 - 2.0 ** (-5.0 - head))


NT_DIMS = (((1,), (1,)), ((), ()))
TN_DIMS = (((0,), (0,)), ((), ()))


def _scan_fast_kernel(z_ref, dec_ref, sg0_ref, sr0_ref, gnw_ref, rnw_ref,
                      ua_ref, ub_ref, sg_out_ref, sr_out_ref, sg_scr, sr_scr, *, n_steps):
    step = pl.program_id(1)
    L = CHUNK
    chunks_per_step = dec_ref.shape[0]

    @pl.when(step == 0)
    def _():
        sg_scr[...] = sg0_ref[0]
        sr_scr[...] = sr0_ref[0]

    row = lax.broadcasted_iota(jnp.int32, (L, L), 0)
    col = lax.broadcasted_iota(jnp.int32, (L, L), 1)
    causal = row >= col
    rel = (row - col).astype(F32)
    t_col = lax.broadcasted_iota(jnp.int32, (L, 1), 0).astype(F32)

    def nt(a, b):
        return lax.dot_general(a, b, NT_DIMS, preferred_element_type=F32)

    def tn(a, b):
        return lax.dot_general(a, b, TN_DIMS, preferred_element_type=F32)

    def one_chunk(j, carry):
        rows = pl.ds(pl.multiple_of(j * L, L), L)
        dec = dec_ref[j]

        for h in range(HEADS):
            qt = z_ref[rows, PQT + h * DK:PQT + (h + 1) * DK]
            kt = z_ref[rows, PKT + h * DK:PKT + (h + 1) * DK]
            kd = z_ref[rows, PKD + h * DK:PKD + (h + 1) * DK]
            v = z_ref[rows, PVA + h * DV:PVA + (h + 1) * DV]
            s = jnp.where(causal, nt(qt, kt), 0.0).astype(BF16)
            st_old = sg_scr[h]
            o = _dot(s, v) + nt(qt, st_old.astype(BF16))
            sg_scr[h] = dec[:, h * DK:(h + 1) * DK] * st_old + tn(v, kd)
            g = z_ref[rows, PGA + h * DV:PGA + (h + 1) * DV].astype(F32)
            ua_ref[rows, h * DV:(h + 1) * DV] = (_rms(o, gnw_ref[...]) * g).astype(BF16)

        for h in range(HEADS):
            lg = _ret_log_decay(h)
            qr = z_ref[rows, PQB + h * DK:PQB + (h + 1) * DK]
            kr = z_ref[rows, PKB + h * DK:PKB + (h + 1) * DK]
            kd = z_ref[rows, PKDB + h * DK:PKDB + (h + 1) * DK]
            v = z_ref[rows, PVB + h * DV:PVB + (h + 1) * DV]
            decay = jnp.exp(jnp.where(causal, lg * rel, -jnp.inf))
            s = (nt(qr, kr) * decay).astype(BF16)
            st_old = sr_scr[h]
            inner = jnp.exp(lg * (t_col + 1.0))
            o = _dot(s, v) + inner * nt(qr, st_old.astype(BF16))
            sr_scr[h] = math.exp(lg * L) * st_old + tn(v, kd)
            g = z_ref[rows, PGB + h * DV:PGB + (h + 1) * DV].astype(F32)
            ub_ref[rows, h * DV:(h + 1) * DV] = (_rms(o, rnw_ref[...]) * g).astype(BF16)
        return carry

    lax.fori_loop(0, chunks_per_step, one_chunk, 0)

    @pl.when(step == n_steps - 1)
    def _():
        sg_out_ref[0] = sg_scr[...]
        sr_out_ref[0] = sr_scr[...]


def _scan_fast(zp, dec, sg0t, sr0t, gnw, rnw, batch):
    T = zp.shape[0]
    n_chunks = T // (batch * CHUNK)
    cps = math.gcd(SCAN_CHUNKS_PER_STEP, n_chunks)
    n_steps = n_chunks // cps
    state_spec = pl.BlockSpec((1, HEADS, DV, DK), lambda b, c: (b, 0, 0, 0))
    full = lambda shape: pl.BlockSpec(shape, lambda b, c: (0,) * len(shape))
    rows = lambda w: pl.BlockSpec((cps * CHUNK, w), lambda b, c: (b * n_steps + c, 0))
    return pl.pallas_call(
        functools.partial(_scan_fast_kernel, n_steps=n_steps),
        grid=(batch, n_steps),
        in_specs=[rows(PZ_W),
                  pl.BlockSpec((cps, 1, HEADS * DK), lambda b, c: (b * n_steps + c, 0, 0)),
                  state_spec, state_spec, full((1, DV)), full((1, DV))],
        out_specs=[rows(HEADS * DV), rows(HEADS * DV), state_spec, state_spec],
        out_shape=[jax.ShapeDtypeStruct((T, HEADS * DV), BF16), jax.ShapeDtypeStruct((T, HEADS * DV), BF16),
                   jax.ShapeDtypeStruct((batch, HEADS, DV, DK), F32),
                   jax.ShapeDtypeStruct((batch, HEADS, DV, DK), F32)],
        scratch_shapes=[pltpu.VMEM((HEADS, DV, DK), F32), pltpu.VMEM((HEADS, DV, DK), F32)],
        compiler_params=_params(2),
        name="scanfast",
    )(zp, dec, sg0t, sr0t, gnw, rnw)


def _scan_kernel(z_ref, cs_ref, sn_ref, sg0_ref, sr0_ref, wup_ref, ba_ref, gnw_ref, rnw_ref,
                 ua_ref, ub_ref, sg_out_ref, sr_out_ref, sg_scr, sr_scr, k_scr, b_scr, *, n_chunks):
    c = pl.program_id(1)
    L = CHUNK

    @pl.when(c == 0)
    def _():
        sg_scr[...] = sg0_ref[0]
        sr_scr[...] = sr0_ref[0]

    row = lax.broadcasted_iota(jnp.int32, (L, L), 0)
    col = lax.broadcasted_iota(jnp.int32, (L, L), 1)
    causal = row >= col
    tril = causal.astype(BF16)

    la = _log_sigmoid(_dot(z_ref[:, RA:RA + LANES], wup_ref[...]) + ba_ref[...]) * (1.0 / GLA_TAU)
    hi, mid, lo = _split3(la)
    b_all = _dot(tril, hi) + _dot(tril, mid) + _dot(tril, lo)

    for h in range(HEADS):
        bh = b_all[:, h * DK:(h + 1) * DK]
        q = z_ref[:, QA + h * DK:QA + (h + 1) * DK].astype(F32) * (DK ** -0.5)
        k = z_ref[:, KA + h * DK:KA + (h + 1) * DK].astype(F32)
        v = z_ref[:, VA + h * DV:VA + (h + 1) * DV]
        qt = (q * jnp.exp(bh)).astype(BF16)
        kT = k.T
        bT = bh.T
        blT = bT[:, L - 1:L]
        kdT = (kT * jnp.exp(blT - bT)).astype(BF16)
        k_scr[...] = k
        b_scr[...] = bh

        def score_column(j, acc, q=q, bh=bh):
            e = jnp.exp(jnp.minimum(bh - b_scr[pl.ds(j, 1), :], 0.0))
            colv = jnp.sum(q * k_scr[pl.ds(j, 1), :] * e, axis=1, keepdims=True)
            return jnp.where(col == j, colv, acc)

        s = lax.fori_loop(0, L, score_column, jnp.zeros((L, L), F32))
        s = jnp.where(causal, s, 0.0).astype(BF16)
        s_old = sg_scr[h]
        o = _dot(s, v) + _dot(qt, s_old.astype(BF16))
        sg_scr[h] = jnp.exp(blT) * s_old + _dot(kdT, v)
        g = z_ref[:, GA + h * DV:GA + (h + 1) * DV].astype(F32)
        ua_ref[:, h * DV:(h + 1) * DV] = (_rms(o, gnw_ref[...]) * (g * jax.nn.sigmoid(g))).astype(BF16)

    cs = cs_ref[...]
    sn = sn_ref[...]
    rel = (row - col).astype(F32)
    t_col = lax.broadcasted_iota(jnp.int32, (L, 1), 0).astype(F32)
    s_row = lax.broadcasted_iota(jnp.int32, (1, L), 1).astype(F32)
    for h in range(HEADS):
        lg = _ret_log_decay(h)
        q = z_ref[:, QB + h * DK:QB + (h + 1) * DK].astype(F32)
        k = z_ref[:, KB + h * DK:KB + (h + 1) * DK].astype(F32)
        v = z_ref[:, VB + h * DV:VB + (h + 1) * DV]
        qr = _rotate(q, cs, sn).astype(BF16)
        kr = _rotate(k, cs, sn) * (DK ** -0.5)
        krT = kr.T
        decay = jnp.exp(jnp.where(causal, lg * rel, -jnp.inf))
        s = (_dot(qr, krT.astype(BF16)) * decay).astype(BF16)
        s_old = sr_scr[h]
        inner = jnp.exp(lg * (t_col + 1.0))
        o = _dot(s, v) + inner * _dot(qr, s_old.astype(BF16))
        kdT = (krT * jnp.exp(lg * (L - 1.0 - s_row))).astype(BF16)
        sr_scr[h] = math.exp(lg * L) * s_old + _dot(kdT, v)
        g = z_ref[:, GB + h * DV:GB + (h + 1) * DV].astype(F32)
        ub_ref[:, h * DV:(h + 1) * DV] = (_rms(o, rnw_ref[...]) * (g * jax.nn.sigmoid(g))).astype(BF16)

    @pl.when(c == n_chunks - 1)
    def _():
        sg_out_ref[0] = sg_scr[...]
        sr_out_ref[0] = sr_scr[...]


def _scan(zs, cs, sn, sg0, sr0, wup, ba, gnw, rnw, batch):
    T = zs.shape[0]
    n_chunks = T // (batch * CHUNK)
    state_spec = pl.BlockSpec((1, HEADS, DK, DV), lambda b, c: (b, 0, 0, 0))
    full = lambda shape: pl.BlockSpec(shape, lambda b, c: (0,) * len(shape))
    return pl.pallas_call(
        functools.partial(_scan_kernel, n_chunks=n_chunks),
        grid=(batch, n_chunks),
        in_specs=[pl.BlockSpec((CHUNK, ZS_W), lambda b, c: (b * n_chunks + c, 0)),
                  pl.BlockSpec((CHUNK, DK), lambda b, c: (c, 0)),
                  pl.BlockSpec((CHUNK, DK), lambda b, c: (c, 0)),
                  state_spec, state_spec,
                  full((LANES, HEADS * DK)), full((1, HEADS * DK)), full((1, DV)), full((1, DV))],
        out_specs=[pl.BlockSpec((CHUNK, HEADS * DV), lambda b, c: (b * n_chunks + c, 0)),
                   pl.BlockSpec((CHUNK, HEADS * DV), lambda b, c: (b * n_chunks + c, 0)),
                   state_spec, state_spec],
        out_shape=[jax.ShapeDtypeStruct((T, HEADS * DV), BF16), jax.ShapeDtypeStruct((T, HEADS * DV), BF16),
                   jax.ShapeDtypeStruct((batch, HEADS, DK, DV), F32),
                   jax.ShapeDtypeStruct((batch, HEADS, DK, DV), F32)],
        scratch_shapes=[pltpu.VMEM((HEADS, DK, DV), F32), pltpu.VMEM((HEADS, DK, DV), F32),
                        pltpu.VMEM((CHUNK, DK), F32), pltpu.VMEM((CHUNK, DK), F32)],
        compiler_params=_params(2),
        name="scan",
    )(zs, cs, sn, sg0, sr0, wup, ba, gnw, rnw)


def _post_kernel(ua_ref, ub_ref, zm_ref, x_ref, wa_ref, wb_ref, wo_ref, nfw_ref, wr_ref, br_ref, cnt0_ref,
                 earlier_ref, x1_ref, e_ref, g_ref, rank_ref, cnt_ref):
    @pl.when(pl.program_id(0) == 0)
    def _():
        cnt_ref[...] = cnt0_ref[...]

    a = _dot(ua_ref[...], wa_ref[...])
    b = _dot(ub_ref[...], wb_ref[...])
    merged = zm_ref[:, :D_MODEL].astype(F32) * a + zm_ref[:, D_MODEL:].astype(F32) * b
    x1 = x_ref[...] + _dot(merged.astype(BF16), wo_ref[...])
    x1_ref[...] = x1

    h2 = _rms(x1, nfw_ref[...])
    h_hi, h_mid, _ = _split3(h2)
    w_hi, w_mid = wr_ref[0], wr_ref[1]
    logits = _dot(h_hi, w_hi) + _dot(h_hi, w_mid) + _dot(h_mid, w_hi) + br_ref[...]
    lane = lax.broadcasted_iota(jnp.int32, logits.shape, 1)
    lane_f = lane.astype(F32)
    work = jnp.where(lane < N_EXPERTS, logits, -jnp.inf)
    vals, idxs = [], []
    for _ in range(TOP_K):
        m = jnp.max(work, axis=1, keepdims=True)
        idx = jnp.min(jnp.where(work == m, lane_f, float(LANES)), axis=1, keepdims=True)
        vals.append(m)
        idxs.append(idx)
        work = jnp.where(lane_f == idx, -jnp.inf, work)
    ps = [jnp.exp(v - vals[0]) for v in vals]
    den = ps[0] + ps[1] + ps[2] + ps[3]
    e_out = jnp.zeros(logits.shape, F32)
    g_out = jnp.zeros(logits.shape, F32)
    for k in range(TOP_K):
        e_out = jnp.where(lane == k, idxs[k], e_out)
        g_out = jnp.where(lane == k, ps[k] / den, g_out)
    e_ref[...] = e_out.astype(jnp.int32)
    g_ref[...] = g_out

    multi_hot = jnp.zeros(logits.shape, F32)
    for k in range(TOP_K):
        multi_hot = multi_hot + (lane_f == idxs[k]).astype(F32)
    before = _dot(earlier_ref[...], multi_hot.astype(BF16)) + cnt_ref[...]
    r_out = jnp.zeros(logits.shape, F32)
    for k in range(TOP_K):
        rk = jnp.sum(jnp.where(lane_f == idxs[k], before, 0.0), axis=1, keepdims=True)
        r_out = jnp.where(lane == k, rk, r_out)
    rank_ref[...] = r_out.astype(jnp.int32)
    cnt_ref[...] += jnp.sum(multi_hot, axis=0, keepdims=True)


def _post(ua, ub, zm, x, wa, wb, wo, nfw, wr, br, cnt0):
    T = x.shape[0]
    tm = min(ROW_TILE, T)
    assert T % tm == 0
    rows = lambda w: pl.BlockSpec((tm, w), lambda i: (i, 0))
    full = lambda shape: pl.BlockSpec(shape, lambda i: (0,) * len(shape))
    earlier = (jnp.arange(tm)[:, None] > jnp.arange(tm)[None, :]).astype(BF16)
    return pl.pallas_call(
        _post_kernel,
        grid=(T // tm,),
        in_specs=[rows(HEADS * DV), rows(HEADS * DV), rows(ZM_W), rows(D_MODEL),
                  full((HEADS * DV, D_MODEL)), full((HEADS * DV, D_MODEL)), full((D_MODEL, D_MODEL)),
                  full((1, D_MODEL)), full((2, D_MODEL, LANES)), full((1, LANES)), full((1, LANES)),
                  full((tm, tm))],
        out_specs=[rows(D_MODEL), rows(LANES), rows(LANES), rows(LANES), full((1, LANES))],
        out_shape=[jax.ShapeDtypeStruct((T, D_MODEL), F32), jax.ShapeDtypeStruct((T, LANES), jnp.int32),
                   jax.ShapeDtypeStruct((T, LANES), F32), jax.ShapeDtypeStruct((T, LANES), jnp.int32),
                   jax.ShapeDtypeStruct((1, LANES), F32)],
        compiler_params=_params(1),
        name="post",
    )(ua, ub, zm, x, wa, wb, wo, nfw, wr, br, cnt0, earlier)


ROW_SUBLANES = D_MODEL // LANES


def _row_tile(r):
    return pl.ds(pl.multiple_of(r * ROW_SUBLANES, ROW_SUBLANES), ROW_SUBLANES)


def _load_row_tiles(ref, n_rows):
    return [ref[pl.ds(c, n_rows, stride=ROW_SUBLANES), :] for c in range(ROW_SUBLANES)]


def _store_row_tiles(ref, x):
    for c in range(ROW_SUBLANES):
        ref[pl.ds(c, x.shape[0], stride=ROW_SUBLANES), :] = x[:, c * LANES:(c + 1) * LANES]


def _row_copy_wait(src_rows, dst_rows, sem):
    pltpu.make_async_copy(src_rows, dst_rows, sem).wait()


def _dispatch_kernel(dest_ref, xp_ref, xs_ref, nfw_ref, rows_ref, hbuf, sem, *, n_prompt_tiles, n_tiles):
    tm = hbuf.shape[1] // ROW_SUBLANES
    i = pl.program_id(0)
    slot = i % 2

    def wait_slot(s):
        for _ in range(TOP_K):
            _row_copy_wait(hbuf.at[s], rows_ref.at[pl.ds(0, tm * ROW_SUBLANES)], sem.at[s])

    @pl.when(i >= 2)
    def _():
        wait_slot(slot)

    @pl.when(i < n_prompt_tiles)
    def _():
        _store_row_tiles(hbuf.at[slot], _rms(xp_ref[...], nfw_ref[...]))

    @pl.when(i >= n_prompt_tiles)
    def _():
        _store_row_tiles(hbuf.at[slot], _rms(xs_ref[...], nfw_ref[...]))

    def issue(r, carry):
        src = hbuf.at[slot, _row_tile(r)]
        for k in range(TOP_K):
            d = dest_ref[0, 0, r * TOP_K + k]
            pltpu.make_async_copy(src, rows_ref.at[_row_tile(d)], sem.at[slot]).start(priority=k % 2)
        return carry

    lax.fori_loop(0, tm, issue, 0, unroll=ISSUE_UNROLL)

    @pl.when(i == n_tiles - 1)
    def _():
        @pl.when(i >= 1)
        def _():
            wait_slot(1 - slot)
        wait_slot(slot)


def _dispatch(dest, x1_p, x1_s, nfw):
    Tp, Ts = x1_p.shape[0], x1_s.shape[0]
    tm = math.gcd(DISPATCH_TILE, Tp, Ts)
    assert tm % 8 == 0
    n_p, n_s = Tp // tm, Ts // tm
    return pl.pallas_call(
        functools.partial(_dispatch_kernel, n_prompt_tiles=n_p, n_tiles=n_p + n_s),
        grid=(n_p + n_s,),
        in_specs=[pl.BlockSpec((1, 1, tm * TOP_K), lambda i: (i, 0, 0), memory_space=pltpu.SMEM),
                  pl.BlockSpec((tm, D_MODEL), lambda i: (jnp.minimum(i, n_p - 1), 0)),
                  pl.BlockSpec((tm, D_MODEL), lambda i: (jnp.maximum(i - n_p, 0), 0)),
                  pl.BlockSpec((1, D_MODEL), lambda i: (0, 0))],
        out_specs=pl.BlockSpec(memory_space=pl.ANY),
        out_shape=jax.ShapeDtypeStruct(((Tp + Ts) * TOP_K * ROW_SUBLANES, LANES), F32),
        scratch_shapes=[pltpu.VMEM((2, tm * ROW_SUBLANES, LANES), F32), pltpu.SemaphoreType.DMA((2,))],
        compiler_params=_params(1),
        name="dispatch",
    )(dest.reshape(n_p + n_s, 1, tm * TOP_K), x1_p, x1_s, nfw)


def _expert_kernel(blk_ref, exp_ref, lo_ref, hi_ref, xs_ref, wgu_ref, bgu_ref, wd_ref, bd_ref, ys_ref, saved):
    del blk_ref, exp_ref
    i = pl.program_id(0)
    lo, hi = lo_ref[i], hi_ref[i]

    @pl.when(hi > lo)
    def _():
        bm = xs_ref.shape[0] // ROW_SUBLANES

        @pl.when(lo > 0)
        def _():
            saved[...] = ys_ref[...]

        x = jnp.concatenate([t.astype(BF16) for t in _load_row_tiles(xs_ref, bm)], axis=1)
        hgu = _dot(x, wgu_ref[0]) + bgu_ref[0]
        glu = jnp.minimum(hgu[:, :D_FF], SWIGLU_LIMIT)
        lin = jnp.clip(hgu[:, D_FF:], -SWIGLU_LIMIT, SWIGLU_LIMIT)
        act = glu * jax.nn.sigmoid(SWIGLU_ALPHA * glu) * (lin + 1.0)
        _store_row_tiles(ys_ref, _dot(act.astype(BF16), wd_ref[0]) + bd_ref[0])

        @pl.when(lo > 0)
        def _():
            keep = lax.broadcasted_iota(jnp.int32, (bm, 1), 0) < lo
            for c in range(ROW_SUBLANES):
                rows = pl.ds(c, bm, stride=ROW_SUBLANES)
                ys_ref[rows, :] = jnp.where(keep, saved[rows, :], ys_ref[rows, :])


def _experts(items, xs, wgu, bgu, wd, bd):
    P = xs.shape[0] // ROW_SUBLANES
    bm = EXPERT_BLOCK
    n_items = items[0].shape[0]
    blk = lambda i, b, e, lo, hi: (b[i], 0)
    ex3 = lambda i, b, e, lo, hi: (e[i], 0, 0)
    return pl.pallas_call(
        _expert_kernel,
        grid_spec=pltpu.PrefetchScalarGridSpec(
            num_scalar_prefetch=4,
            grid=(n_items,),
            in_specs=[pl.BlockSpec((bm * ROW_SUBLANES, LANES), blk),
                      pl.BlockSpec((1, D_MODEL, 2 * D_FF), ex3),
                      pl.BlockSpec((1, 1, 2 * D_FF), ex3),
                      pl.BlockSpec((1, D_FF, D_MODEL), ex3),
                      pl.BlockSpec((1, 1, D_MODEL), ex3)],
            out_specs=pl.BlockSpec((bm * ROW_SUBLANES, LANES), blk),
            scratch_shapes=[pltpu.VMEM((bm * ROW_SUBLANES, LANES), F32)]),
        out_shape=jax.ShapeDtypeStruct((P * ROW_SUBLANES, LANES), F32),
        compiler_params=_params(1),
        name="experts",
    )(*items, xs, wgu, bgu, wd, bd)


def _combine_kernel(dest_ref, dest_next_ref, gate_ref, x1_ref, nw_ref, ys_ref, out_ref, buf, sem, *, n_tiles):
    tm = x1_ref.shape[0]
    i = pl.program_id(0)
    slot = i % 2

    def issue_tile(idx_ref, s):
        def issue(r, carry):
            for k in range(TOP_K):
                d = idx_ref[0, 0, r * TOP_K + k]
                pltpu.make_async_copy(ys_ref.at[_row_tile(d)], buf.at[s, k, _row_tile(r)],
                                      sem.at[s]).start(priority=k % 2)
            return carry

        lax.fori_loop(0, tm, issue, 0, unroll=ISSUE_UNROLL)

    @pl.when(i == 0)
    def _():
        issue_tile(dest_ref, slot)

    @pl.when(i + 1 < n_tiles)
    def _():
        issue_tile(dest_next_ref, 1 - slot)

    for k in range(TOP_K):
        _row_copy_wait(ys_ref.at[pl.ds(0, tm * ROW_SUBLANES)], buf.at[slot, k], sem.at[slot])
    gate = gate_ref[...]
    chunks = [_load_row_tiles(buf.at[slot, k], tm) for k in range(TOP_K)]
    f = []
    for c in range(ROW_SUBLANES):
        fc = gate[:, 0:1] * chunks[0][c]
        for k in range(1, TOP_K):
            fc = fc + gate[:, k:k + 1] * chunks[k][c]
        f.append(fc)
    out_ref[...] = _rms(x1_ref[...] + jnp.concatenate(f, axis=1), nw_ref[...])


def _combine(dest, gate, x1, nw, ys):
    T = x1.shape[0]
    tm = min(COMBINE_TILE, T)
    assert T % tm == 0
    n = T // tm
    dest3 = dest.reshape(n, 1, tm * TOP_K)
    return pl.pallas_call(
        functools.partial(_combine_kernel, n_tiles=n),
        grid=(n,),
        in_specs=[pl.BlockSpec((1, 1, tm * TOP_K), lambda i: (i, 0, 0), memory_space=pltpu.SMEM),
                  pl.BlockSpec((1, 1, tm * TOP_K), lambda i: (jnp.minimum(i + 1, n - 1), 0, 0),
                               memory_space=pltpu.SMEM),
                  pl.BlockSpec((tm, LANES), lambda i: (i, 0)),
                  pl.BlockSpec((tm, D_MODEL), lambda i: (i, 0)),
                  pl.BlockSpec((1, D_MODEL), lambda i: (0, 0)),
                  pl.BlockSpec(memory_space=pl.ANY)],
        out_specs=pl.BlockSpec((tm, D_MODEL), lambda i: (i, 0)),
        out_shape=jax.ShapeDtypeStruct((T, D_MODEL), F32),
        scratch_shapes=[pltpu.VMEM((2, TOP_K, tm * ROW_SUBLANES, LANES), F32), pltpu.SemaphoreType.DMA((2,))],
        compiler_params=_params(1),
        name="combine",
    )(dest3, dest3, gate, x1, nw, ys)


def _rotary_tables(pos):
    inv = ROPE_BASE ** (-jnp.arange(0, DK, 2, dtype=F32) / DK)
    ang = pos.astype(F32)[:, None] * inv[None, :]
    cos, sin = jnp.cos(ang), jnp.sin(ang)
    return jnp.concatenate([cos, cos], axis=1), jnp.concatenate([-sin, sin], axis=1)


def _routing(top_e, rank, counts):
    bm = EXPERT_BLOCK
    n_rows = top_e.size
    n_blocks = n_rows // bm
    n_items = n_blocks + N_EXPERTS - 1
    start = jnp.cumsum(counts) - counts
    end = start + counts
    expert_ids = jnp.arange(N_EXPERTS, dtype=jnp.int32)
    onehot = top_e[:, :, None] == expert_ids[None, None, :]
    dest = (jnp.sum(jnp.where(onehot, start[None, None, :], 0), axis=2) + rank).astype(jnp.int32)

    first_blk = start // bm
    n_it = jnp.where(counts > 0, (end - 1) // bm - first_blk + 1, 0)
    it_end = jnp.cumsum(n_it)
    it_start = it_end - n_it
    j = jnp.arange(n_items, dtype=jnp.int32)
    e_j = jnp.minimum(jnp.sum(j[:, None] >= it_end[None, :], axis=1), N_EXPERTS - 1).astype(jnp.int32)
    valid = j < it_end[-1]
    last_e = jnp.max(jnp.where(counts > 0, expert_ids, 0))
    e_j = jnp.where(valid, e_j, last_e)
    blk = jnp.where(valid, first_blk[e_j] + (j - it_start[e_j]), n_blocks - 1)
    lo = jnp.maximum(start[e_j], blk * bm) - blk * bm
    hi = jnp.minimum(end[e_j], (blk + 1) * bm) - blk * bm
    lo = jnp.where(valid, lo, 0)
    hi = jnp.where(valid, hi, 0)
    items = tuple(a.astype(jnp.int32) for a in (blk, e_j, lo, hi))
    return dest, items


def kernel(x_prompt, x_sample, state_gla, state_ret, meta_tokens, norm_mix_w, w_in, w_alpha_up, b_alpha, gla_norm_w, ret_norm_w, w_branch_gla, w_branch_ret, w_out, norm_ffn_w, w_router, b_router, w_gate_up, b_gate_up, w_down, b_down, norm_final_w):
    B, S, D = x_prompt.shape
    DB, DS, _ = x_sample.shape
    assert D == D_MODEL and S % CHUNK == 0 and DS == CHUNK and state_gla.shape[0] == 1
    l = 0

    offs = [0]
    for w in IN_WIDTHS:
        offs.append(offs[-1] + w)
    cols = [w_in[l][:, offs[i]:offs[i + 1]] for i in range(len(IN_WIDTHS))]
    qa, ka, va, ga, ra, qb, kb, vb, gb, m = cols
    ra = jnp.pad(ra, ((0, 0), (0, LANES - GLA_RANK)))
    ws = jnp.concatenate([qa, ka, va, ga, qb, kb, vb, gb, ra], axis=1).astype(BF16)
    wm = m.astype(BF16)
    wup = jnp.pad(w_alpha_up[l], ((0, LANES - GLA_RANK), (0, 0))).astype(BF16)
    ba = b_alpha[l][None, :]
    nmw = norm_mix_w[l][None, :]
    gnw = gla_norm_w[l][None, :]
    rnw = ret_norm_w[l][None, :]

    def mixer(x2d, pos, sg0, sr0, batch):
        T = x2d.shape[0]
        cs, sn = _rotary_tables(pos)
        reps = max(1, min(ROW_TILE, T) // pos.shape[0])
        zp, zm, dec, bmin = _prep(x2d, nmw, ws, wm, wup, ba, jnp.tile(cs, (reps, 1)), jnp.tile(sn, (reps, 1)))

        def ratio_form():
            ua, ub, sgt, srt = _scan_fast(zp, dec, jnp.swapaxes(sg0, 2, 3), jnp.swapaxes(sr0, 2, 3), gnw, rnw, batch)
            return ua, ub, jnp.swapaxes(sgt, 2, 3), jnp.swapaxes(srt, 2, 3)

        def exact_form():
            return tuple(_scan(_inproj(x2d, nmw, ws), cs, sn, sg0, sr0, wup, ba, gnw, rnw, batch))

        ua, ub, sg, sr = lax.cond(jnp.min(bmin) >= MIN_SAFE_LOG_FORGET, ratio_form, exact_form)
        return ua, ub, zm, sg, sr

    x_meta = jnp.concatenate([jnp.zeros((CHUNK - N_META, D), F32), meta_tokens.astype(F32)], axis=0)
    pos_meta = jnp.maximum(jnp.arange(CHUNK) - (CHUNK - N_META), 0)
    zero_state = jnp.zeros((1, HEADS, DK, DV), F32)
    _, _, _, sg_m, sr_m = mixer(x_meta, pos_meta, zero_state, zero_state, 1)

    xp = x_prompt.reshape(B * S, D)
    xs_ = x_sample.reshape(DB * DS, D)
    sg0 = jnp.broadcast_to(sg_m, (B, HEADS, DK, DV))
    sr0 = jnp.broadcast_to(sr_m, (B, HEADS, DK, DV))
    ua_p, ub_p, zm_p, sg_p, sr_p = mixer(xp, N_META + jnp.arange(S), sg0, sr0, B)
    ua_s, ub_s, zm_s, sg_s, sr_s = mixer(xs_, N_META + PAST_LEN + jnp.arange(DS),
                                         state_gla[l].astype(F32), state_ret[l].astype(F32), DB)

    wa = w_branch_gla[l].astype(BF16)
    wb = w_branch_ret[l].astype(BF16)
    wo = w_out[l].astype(BF16)
    nfw = norm_ffn_w[l][None, :]
    wr = jnp.pad(w_router[l].astype(F32), ((0, 0), (0, LANES - N_EXPERTS)))
    wr_hi = wr.astype(BF16)
    wr_mid = (wr - wr_hi.astype(F32)).astype(BF16)
    wr2 = jnp.stack([wr_hi, wr_mid])
    br = jnp.pad(b_router[l].astype(F32), (0, LANES - N_EXPERTS))[None, :]
    cnt0 = jnp.zeros((1, LANES), F32)
    x1_p, e_p, g_p, r_p, cnt_p = _post(ua_p, ub_p, zm_p, xp, wa, wb, wo, nfw, wr2, br, cnt0)
    x1_s, e_s, g_s, r_s, cnt = _post(ua_s, ub_s, zm_s, xs_, wa, wb, wo, nfw, wr2, br, cnt_p)

    Tp, Ts = B * S, DB * DS
    assert ((Tp + Ts) * TOP_K) % EXPERT_BLOCK == 0
    top_e = jnp.concatenate([e_p[:, :TOP_K], e_s[:, :TOP_K]], axis=0)
    rank = jnp.concatenate([r_p[:, :TOP_K], r_s[:, :TOP_K]], axis=0)
    dest, items = _routing(top_e, rank, cnt[0, :N_EXPERTS].astype(jnp.int32))
    dest = dest.reshape(-1)
    dest_p, dest_s = dest[:Tp * TOP_K], dest[Tp * TOP_K:]

    xs_rows = _dispatch(dest, x1_p, x1_s, nfw)
    ys_rows = _experts(items, xs_rows, w_gate_up[l].astype(BF16), b_gate_up[l][:, None, :],
                       w_down[l].astype(BF16), b_down[l][:, None, :])
    nw_final = norm_final_w[None, :]
    y_p = _combine(dest_p, g_p, x1_p, nw_final, ys_rows)
    y_s = _combine(dest_s, g_s, x1_s, nw_final, ys_rows)

    dt = state_gla.dtype
    return (y_p.reshape(B, S, D), y_s.reshape(DB, DS, D),
            sg_p[None].astype(dt), sr_p[None].astype(state_ret.dtype),
            sg_s[None].astype(dt), sr_s[None].astype(state_ret.dtype))
```

```python
import functools
import math

import jax
import jax.numpy as jnp
from jax import lax
from jax.experimental import pallas as pl
from jax.experimental.pallas import tpu as pltpu

F32 = jnp.float32
BF16 = jnp.bfloat16

D_MODEL = 1024
CHUNK = 64
N_META = 16
PAST_LEN = 4096
EPS = 1e-5
HEADS = 4
DK = 128
DV = 256
GLA_RANK = 16
GLA_TAU = 16.0
ROPE_BASE = 10000.0
N_EXPERTS = 32
TOP_K = 4
D_FF = 1024
SWIGLU_ALPHA = 1.702
SWIGLU_LIMIT = 7.0
IN_WIDTHS = (HEADS * DK, HEADS * DK, HEADS * DV, HEADS * DV, GLA_RANK,
             HEADS * DK, HEADS * DK, HEADS * DV, HEADS * DV, 2 * D_MODEL)

LANES = 128
QA, KA, VA, GA, QB, KB, VB, GB, RA = 0, 512, 1024, 2048, 3072, 3584, 4096, 5120, 6144
ZS_W = RA + LANES
ZM_W = 2 * D_MODEL
PQT, PKT, PKD, PVA, PGA, PQB, PKB, PKDB, PVB, PGB = 0, 512, 1024, 1536, 2560, 3584, 4096, 4608, 5120, 6144
PZ_W = PGB + HEADS * DV
MIN_SAFE_LOG_FORGET = -60.0

ROW_TILE = 512
DISPATCH_TILE = 256
COMBINE_TILE = 128
EXPERT_BLOCK = 512
ISSUE_UNROLL = 4
SCAN_CHUNKS_PER_STEP = 8
VMEM_LIMIT = 56 * 1024 * 1024


def _params(n_axes):
    return pltpu.CompilerParams(dimension_semantics=("arbitrary",) * n_axes, vmem_limit_bytes=VMEM_LIMIT)


def _rms(x, w):
    return x * lax.rsqrt(jnp.mean(x * x, axis=-1, keepdims=True) + EPS) * w


def _dot(a, b):
    return jnp.dot(a, b, preferred_element_type=F32)


def _split3(x):
    hi = x.astype(BF16)
    r1 = x - hi.astype(F32)
    mid = r1.astype(BF16)
    lo = (r1 - mid.astype(F32)).astype(BF16)
    return hi, mid, lo


def _log_sigmoid(x):
    return jnp.minimum(x, 0.0) - jnp.log1p(jnp.exp(-jnp.abs(x)))


def _rotate(t, cs, sn):
    return t * cs + pltpu.roll(t, DK // 2, axis=1) * sn


def _prep_kernel(x_ref, nw_ref, ws_ref, wm_ref, wup_ref, ba_ref, tril_ref, cs_ref, sn_ref,
                 zp_ref, zm_ref, dec_ref, bmin_ref):
    tm = x_ref.shape[0]
    n = tm // CHUNK
    h = _rms(x_ref[...], nw_ref[...]).astype(BF16)

    def proj(c0, w):
        return _dot(h, ws_ref[:, c0:c0 + w])

    ra = proj(RA, LANES).astype(BF16)
    la = _log_sigmoid(_dot(ra, wup_ref[...]) + ba_ref[...]) * (1.0 / GLA_TAU)
    hi, mid, lo = _split3(la)
    tril = tril_ref[...]
    b = _dot(tril, hi) + _dot(tril, mid) + _dot(tril, lo)
    b3 = b.reshape(n, CHUNK, HEADS * DK)
    bl3 = b3[:, CHUNK - 1:CHUNK, :]
    dec_ref[...] = jnp.exp(bl3)
    bl = jnp.broadcast_to(bl3, b3.shape).reshape(tm, HEADS * DK)
    bmin = jnp.min(jnp.min(b, axis=0, keepdims=True), axis=1, keepdims=True)
    bmin_ref[0] = jnp.broadcast_to(bmin, (1, LANES))

    q = proj(QA, HEADS * DK) * (DK ** -0.5)
    zp_ref[:, PQT:PQT + HEADS * DK] = (q * jnp.exp(b)).astype(BF16)
    k = proj(KA, HEADS * DK)
    zp_ref[:, PKT:PKT + HEADS * DK] = (k * jnp.exp(-b)).astype(BF16)
    zp_ref[:, PKD:PKD + HEADS * DK] = (k * jnp.exp(bl - b)).astype(BF16)

    half = HEADS * DV // 2
    for j in range(2):
        zp_ref[:, PVA + j * half:PVA + (j + 1) * half] = proj(VA + j * half, half).astype(BF16)
        g = proj(GA + j * half, half)
        zp_ref[:, PGA + j * half:PGA + (j + 1) * half] = (g * jax.nn.sigmoid(g)).astype(BF16)
        zp_ref[:, PVB + j * half:PVB + (j + 1) * half] = proj(VB + j * half, half).astype(BF16)
        g = proj(GB + j * half, half)
        zp_ref[:, PGB + j * half:PGB + (j + 1) * half] = (g * jax.nn.sigmoid(g)).astype(BF16)

    cs = cs_ref[...]
    sn = sn_ref[...]
    qb = proj(QB, HEADS * DK)
    kb = proj(KB, HEADS * DK)
    steps_left = (CHUNK - 1 - (lax.broadcasted_iota(jnp.int32, (tm, 1), 0) % CHUNK)).astype(F32)
    for hd in range(HEADS):
        sl = slice(hd * DK, (hd + 1) * DK)
        zp_ref[:, PQB + hd * DK:PQB + (hd + 1) * DK] = _rotate(qb[:, sl], cs, sn).astype(BF16)
        kr = _rotate(kb[:, sl], cs, sn) * (DK ** -0.5)
        zp_ref[:, PKB + hd * DK:PKB + (hd + 1) * DK] = kr.astype(BF16)
        zp_ref[:, PKDB + hd * DK:PKDB + (hd + 1) * DK] = (kr * jnp.exp(_ret_log_decay(hd) * steps_left)).astype(BF16)

    for c0 in range(0, ZM_W, 512):
        zm_ref[:, c0:c0 + 512] = jax.nn.sigmoid(_dot(h, wm_ref[:, c0:c0 + 512])).astype(BF16)


def _prep(x, nw, ws, wm, wup, ba, cs, sn):
    T = x.shape[0]
    tm = min(ROW_TILE, T)
    assert T % tm == 0 and cs.shape[0] % tm == 0
    n_tab = cs.shape[0] // tm
    n = tm // CHUNK
    r = jnp.arange(tm)
    tril = ((r[:, None] >= r[None, :]) & (r[:, None] // CHUNK == r[None, :] // CHUNK)).astype(BF16)
    const = dict(pipeline_mode=pl.Buffered(1))
    full = lambda shape, **kw: pl.BlockSpec(shape, lambda i: (0,) * len(shape), **kw)
    return pl.pallas_call(
        _prep_kernel,
        grid=(T // tm,),
        in_specs=[pl.BlockSpec((tm, D_MODEL), lambda i: (i, 0)),
                  full((1, D_MODEL)),
                  full((D_MODEL, ZS_W), **const), full((D_MODEL, ZM_W), **const),
                  full((LANES, HEADS * DK)), full((1, HEADS * DK)), full((tm, tm)),
                  pl.BlockSpec((tm, DK), lambda i: (i % n_tab, 0)),
                  pl.BlockSpec((tm, DK), lambda i: (i % n_tab, 0))],
        out_specs=[pl.BlockSpec((tm, PZ_W), lambda i: (i, 0)),
                   pl.BlockSpec((tm, ZM_W), lambda i: (i, 0)),
                   pl.BlockSpec((n, 1, HEADS * DK), lambda i: (i, 0, 0)),
                   pl.BlockSpec((1, 1, LANES), lambda i: (i, 0, 0))],
        out_shape=[jax.ShapeDtypeStruct((T, PZ_W), BF16), jax.ShapeDtypeStruct((T, ZM_W), BF16),
                   jax.ShapeDtypeStruct((T // CHUNK, 1, HEADS * DK), F32),
                   jax.ShapeDtypeStruct((T // tm, 1, LANES), F32)],
        compiler_params=_params(1),
        name="prep",
    )(x, nw, ws, wm, wup, ba, tril, cs, sn)


def _inproj_kernel(x_ref, nw_ref, ws_ref, zs_ref):
    h = _rms(x_ref[...], nw_ref[...]).astype(BF16)
    step = 512
    for c0 in range(0, ZS_W, step):
        c1 = min(c0 + step, ZS_W)
        zs_ref[:, c0:c1] = _dot(h, ws_ref[:, c0:c1]).astype(BF16)


def _inproj(x, nw, ws):
    T = x.shape[0]
    tm = min(ROW_TILE, T)
    assert T % tm == 0
    return pl.pallas_call(
        _inproj_kernel,
        grid=(T // tm,),
        in_specs=[pl.BlockSpec((tm, D_MODEL), lambda i: (i, 0)),
                  pl.BlockSpec((1, D_MODEL), lambda i: (0, 0)),
                  pl.BlockSpec((D_MODEL, ZS_W), lambda i: (0, 0), pipeline_mode=pl.Buffered(1))],
        out_specs=pl.BlockSpec((tm, ZS_W), lambda i: (i, 0)),
        out_shape=jax.ShapeDtypeStruct((T, ZS_W), BF16),
        compiler_params=_params(1),
        name="inproj",
    )(x, nw, ws)


def _ret_log_decay(head):
    return math.log(1.0 - 2.0 ** (-5.0 - head))


NT_DIMS = (((1,), (1,)), ((), ()))
TN_DIMS = (((0,), (0,)), ((), ()))


def _scan_fast_kernel(z_ref, dec_ref, sg0_ref, sr0_ref, gnw_ref, rnw_ref,
                      ua_ref, ub_ref, sg_out_ref, sr_out_ref, sg_scr, sr_scr, *, n_steps):
    step = pl.program_id(1)
    L = CHUNK
    chunks_per_step = dec_ref.shape[0]

    @pl.when(step == 0)
    def _():
        sg_scr[...] = sg0_ref[0]
        sr_scr[...] = sr0_ref[0]

    row = lax.broadcasted_iota(jnp.int32, (L, L), 0)
    col = lax.broadcasted_iota(jnp.int32, (L, L), 1)
    causal = row >= col
    rel = (row - col).astype(F32)
    t_col = lax.broadcasted_iota(jnp.int32, (L, 1), 0).astype(F32)

    def nt(a, b):
        return lax.dot_general(a, b, NT_DIMS, preferred_element_type=F32)

    def tn(a, b):
        return lax.dot_general(a, b, TN_DIMS, preferred_element_type=F32)

    units = [(j, gla, h) for j in range(chunks_per_step) for gla in (True, False) for h in range(HEADS)]

    def cols(gla, h):
        q, k, kd, v, g = (PQT, PKT, PKD, PVA, PGA) if gla else (PQB, PKB, PKDB, PVB, PGB)
        return (slice(q + h * DK, q + (h + 1) * DK), slice(k + h * DK, k + (h + 1) * DK),
                slice(kd + h * DK, kd + (h + 1) * DK), slice(v + h * DV, v + (h + 1) * DV),
                slice(g + h * DV, g + (h + 1) * DV))

    ret_decay = [jnp.exp(jnp.where(causal, _ret_log_decay(h) * rel, -jnp.inf)) for h in range(HEADS)]
    ret_inner = [jnp.exp(_ret_log_decay(h) * (t_col + 1.0)) for h in range(HEADS)]

    intra = {}
    for j, gla, h in units:
        rows = pl.ds(j * L, L)
        qc, kc, _, vc, _ = cols(gla, h)
        s = nt(z_ref[rows, qc], z_ref[rows, kc])
        s = jnp.where(causal, s, 0.0) if gla else s * ret_decay[h]
        intra[j, gla, h] = _dot(s.astype(BF16), z_ref[rows, vc])

    for j, gla, h in units:
        rows = pl.ds(j * L, L)
        qc, _, kdc, vc, gc = cols(gla, h)
        scr, out_ref, nw_ref = (sg_scr, ua_ref, gnw_ref) if gla else (sr_scr, ub_ref, rnw_ref)
        st_old = scr[h]
        inter = nt(z_ref[rows, qc], st_old.astype(BF16))
        update = tn(z_ref[rows, vc], z_ref[rows, kdc])
        if gla:
            scr[h] = dec_ref[j][:, h * DK:(h + 1) * DK] * st_old + update
            o = intra[j, gla, h] + inter
        else:
            scr[h] = math.exp(_ret_log_decay(h) * L) * st_old + update
            o = intra[j, gla, h] + ret_inner[h] * inter
        g = z_ref[rows, gc].astype(F32)
        out_ref[rows, h * DV:(h + 1) * DV] = (_rms(o, nw_ref[...]) * g).astype(BF16)

    @pl.when(step == n_steps - 1)
    def _():
        sg_out_ref[0] = sg_scr[...]
        sr_out_ref[0] = sr_scr[...]


def _scan_fast(zp, dec, sg0t, sr0t, gnw, rnw, batch):
    T = zp.shape[0]
    n_chunks = T // (batch * CHUNK)
    cps = math.gcd(SCAN_CHUNKS_PER_STEP, n_chunks)
    n_steps = n_chunks // cps
    state_spec = pl.BlockSpec((1, HEADS, DV, DK), lambda b, c: (b, 0, 0, 0))
    full = lambda shape: pl.BlockSpec(shape, lambda b, c: (0,) * len(shape))
    rows = lambda w: pl.BlockSpec((cps * CHUNK, w), lambda b, c: (b * n_steps + c, 0))
    return pl.pallas_call(
        functools.partial(_scan_fast_kernel, n_steps=n_steps),
        grid=(batch, n_steps),
        in_specs=[rows(PZ_W),
                  pl.BlockSpec((cps, 1, HEADS * DK), lambda b, c: (b * n_steps + c, 0, 0)),
                  state_spec, state_spec, full((1, DV)), full((1, DV))],
        out_specs=[rows(HEADS * DV), rows(HEADS * DV), state_spec, state_spec],
        out_shape=[jax.ShapeDtypeStruct((T, HEADS * DV), BF16), jax.ShapeDtypeStruct((T, HEADS * DV), BF16),
                   jax.ShapeDtypeStruct((batch, HEADS, DV, DK), F32),
                   jax.ShapeDtypeStruct((batch, HEADS, DV, DK), F32)],
        scratch_shapes=[pltpu.VMEM((HEADS, DV, DK), F32), pltpu.VMEM((HEADS, DV, DK), F32)],
        compiler_params=_params(2),
        name="scanfast",
    )(zp, dec, sg0t, sr0t, gnw, rnw)


def _scan_kernel(z_ref, cs_ref, sn_ref, sg0_ref, sr0_ref, wup_ref, ba_ref, gnw_ref, rnw_ref,
                 ua_ref, ub_ref, sg_out_ref, sr_out_ref, sg_scr, sr_scr, k_scr, b_scr, *, n_chunks):
    c = pl.program_id(1)
    L = CHUNK

    @pl.when(c == 0)
    def _():
        sg_scr[...] = sg0_ref[0]
        sr_scr[...] = sr0_ref[0]

    row = lax.broadcasted_iota(jnp.int32, (L, L), 0)
    col = lax.broadcasted_iota(jnp.int32, (L, L), 1)
    causal = row >= col
    tril = causal.astype(BF16)

    la = _log_sigmoid(_dot(z_ref[:, RA:RA + LANES], wup_ref[...]) + ba_ref[...]) * (1.0 / GLA_TAU)
    hi, mid, lo = _split3(la)
    b_all = _dot(tril, hi) + _dot(tril, mid) + _dot(tril, lo)

    for h in range(HEADS):
        bh = b_all[:, h * DK:(h + 1) * DK]
        q = z_ref[:, QA + h * DK:QA + (h + 1) * DK].astype(F32) * (DK ** -0.5)
        k = z_ref[:, KA + h * DK:KA + (h + 1) * DK].astype(F32)
        v = z_ref[:, VA + h * DV:VA + (h + 1) * DV]
        qt = (q * jnp.exp(bh)).astype(BF16)
        kT = k.T
        bT = bh.T
        blT = bT[:, L - 1:L]
        kdT = (kT * jnp.exp(blT - bT)).astype(BF16)
        k_scr[...] = k
        b_scr[...] = bh

        def score_column(j, acc, q=q, bh=bh):
            e = jnp.exp(jnp.minimum(bh - b_scr[pl.ds(j, 1), :], 0.0))
            colv = jnp.sum(q * k_scr[pl.ds(j, 1), :] * e, axis=1, keepdims=True)
            return jnp.where(col == j, colv, acc)

        s = lax.fori_loop(0, L, score_column, jnp.zeros((L, L), F32))
        s = jnp.where(causal, s, 0.0).astype(BF16)
        s_old = sg_scr[h]
        o = _dot(s, v) + _dot(qt, s_old.astype(BF16))
        sg_scr[h] = jnp.exp(blT) * s_old + _dot(kdT, v)
        g = z_ref[:, GA + h * DV:GA + (h + 1) * DV].astype(F32)
        ua_ref[:, h * DV:(h + 1) * DV] = (_rms(o, gnw_ref[...]) * (g * jax.nn.sigmoid(g))).astype(BF16)

    cs = cs_ref[...]
    sn = sn_ref[...]
    rel = (row - col).astype(F32)
    t_col = lax.broadcasted_iota(jnp.int32, (L, 1), 0).astype(F32)
    s_row = lax.broadcasted_iota(jnp.int32, (1, L), 1).astype(F32)
    for h in range(HEADS):
        lg = _ret_log_decay(h)
        q = z_ref[:, QB + h * DK:QB + (h + 1) * DK].astype(F32)
        k = z_ref[:, KB + h * DK:KB + (h + 1) * DK].astype(F32)
        v = z_ref[:, VB + h * DV:VB + (h + 1) * DV]
        qr = _rotate(q, cs, sn).astype(BF16)
        kr = _rotate(k, cs, sn) * (DK ** -0.5)
        krT = kr.T
        decay = jnp.exp(jnp.where(causal, lg * rel, -jnp.inf))
        s = (_dot(qr, krT.astype(BF16)) * decay).astype(BF16)
        s_old = sr_scr[h]
        inner = jnp.exp(lg * (t_col + 1.0))
        o = _dot(s, v) + inner * _dot(qr, s_old.astype(BF16))
        kdT = (krT * jnp.exp(lg * (L - 1.0 - s_row))).astype(BF16)
        sr_scr[h] = math.exp(lg * L) * s_old + _dot(kdT, v)
        g = z_ref[:, GB + h * DV:GB + (h + 1) * DV].astype(F32)
        ub_ref[:, h * DV:(h + 1) * DV] = (_rms(o, rnw_ref[...]) * (g * jax.nn.sigmoid(g))).astype(BF16)

    @pl.when(c == n_chunks - 1)
    def _():
        sg_out_ref[0] = sg_scr[...]
        sr_out_ref[0] = sr_scr[...]


def _scan(zs, cs, sn, sg0, sr0, wup, ba, gnw, rnw, batch):
    T = zs.shape[0]
    n_chunks = T // (batch * CHUNK)
    state_spec = pl.BlockSpec((1, HEADS, DK, DV), lambda b, c: (b, 0, 0, 0))
    full = lambda shape: pl.BlockSpec(shape, lambda b, c: (0,) * len(shape))
    return pl.pallas_call(
        functools.partial(_scan_kernel, n_chunks=n_chunks),
        grid=(batch, n_chunks),
        in_specs=[pl.BlockSpec((CHUNK, ZS_W), lambda b, c: (b * n_chunks + c, 0)),
                  pl.BlockSpec((CHUNK, DK), lambda b, c: (c, 0)),
                  pl.BlockSpec((CHUNK, DK), lambda b, c: (c, 0)),
                  state_spec, state_spec,
                  full((LANES, HEADS * DK)), full((1, HEADS * DK)), full((1, DV)), full((1, DV))],
        out_specs=[pl.BlockSpec((CHUNK, HEADS * DV), lambda b, c: (b * n_chunks + c, 0)),
                   pl.BlockSpec((CHUNK, HEADS * DV), lambda b, c: (b * n_chunks + c, 0)),
                   state_spec, state_spec],
        out_shape=[jax.ShapeDtypeStruct((T, HEADS * DV), BF16), jax.ShapeDtypeStruct((T, HEADS * DV), BF16),
                   jax.ShapeDtypeStruct((batch, HEADS, DK, DV), F32),
                   jax.ShapeDtypeStruct((batch, HEADS, DK, DV), F32)],
        scratch_shapes=[pltpu.VMEM((HEADS, DK, DV), F32), pltpu.VMEM((HEADS, DK, DV), F32),
                        pltpu.VMEM((CHUNK, DK), F32), pltpu.VMEM((CHUNK, DK), F32)],
        compiler_params=_params(2),
        name="scan",
    )(zs, cs, sn, sg0, sr0, wup, ba, gnw, rnw)


def _post_kernel(ua_ref, ub_ref, zm_ref, x_ref, wa_ref, wb_ref, wo_ref, nfw_ref, wr_ref, br_ref, cnt0_ref,
                 earlier_ref, x1_ref, e_ref, g_ref, rank_ref, cnt_ref):
    @pl.when(pl.program_id(0) == 0)
    def _():
        cnt_ref[...] = cnt0_ref[...]

    a = _dot(ua_ref[...], wa_ref[...])
    b = _dot(ub_ref[...], wb_ref[...])
    merged = zm_ref[:, :D_MODEL].astype(F32) * a + zm_ref[:, D_MODEL:].astype(F32) * b
    x1 = x_ref[...] + _dot(merged.astype(BF16), wo_ref[...])
    x1_ref[...] = x1

    h2 = _rms(x1, nfw_ref[...])
    h_hi, h_mid, _ = _split3(h2)
    w_hi, w_mid = wr_ref[0], wr_ref[1]
    logits = _dot(h_hi, w_hi) + _dot(h_hi, w_mid) + _dot(h_mid, w_hi) + br_ref[...]
    lane = lax.broadcasted_iota(jnp.int32, logits.shape, 1)
    lane_f = lane.astype(F32)
    work = jnp.where(lane < N_EXPERTS, logits, -jnp.inf)
    vals, idxs = [], []
    for _ in range(TOP_K):
        m = jnp.max(work, axis=1, keepdims=True)
        idx = jnp.min(jnp.where(work == m, lane_f, float(LANES)), axis=1, keepdims=True)
        vals.append(m)
        idxs.append(idx)
        work = jnp.where(lane_f == idx, -jnp.inf, work)
    ps = [jnp.exp(v - vals[0]) for v in vals]
    den = ps[0] + ps[1] + ps[2] + ps[3]
    e_out = jnp.zeros(logits.shape, F32)
    g_out = jnp.zeros(logits.shape, F32)
    for k in range(TOP_K):
        e_out = jnp.where(lane == k, idxs[k], e_out)
        g_out = jnp.where(lane == k, ps[k] / den, g_out)
    e_ref[...] = e_out.astype(jnp.int32)
    g_ref[...] = g_out

    multi_hot = jnp.zeros(logits.shape, F32)
    for k in range(TOP_K):
        multi_hot = multi_hot + (lane_f == idxs[k]).astype(F32)
    before = _dot(earlier_ref[...], multi_hot.astype(BF16)) + cnt_ref[...]
    r_out = jnp.zeros(logits.shape, F32)
    for k in range(TOP_K):
        rk = jnp.sum(jnp.where(lane_f == idxs[k], before, 0.0), axis=1, keepdims=True)
        r_out = jnp.where(lane == k, rk, r_out)
    rank_ref[...] = r_out.astype(jnp.int32)
    cnt_ref[...] += jnp.sum(multi_hot, axis=0, keepdims=True)


def _post(ua, ub, zm, x, wa, wb, wo, nfw, wr, br, cnt0):
    T = x.shape[0]
    tm = min(ROW_TILE, T)
    assert T % tm == 0
    rows = lambda w: pl.BlockSpec((tm, w), lambda i: (i, 0))
    full = lambda shape: pl.BlockSpec(shape, lambda i: (0,) * len(shape))
    earlier = (jnp.arange(tm)[:, None] > jnp.arange(tm)[None, :]).astype(BF16)
    return pl.pallas_call(
        _post_kernel,
        grid=(T // tm,),
        in_specs=[rows(HEADS * DV), rows(HEADS * DV), rows(ZM_W), rows(D_MODEL),
                  full((HEADS * DV, D_MODEL)), full((HEADS * DV, D_MODEL)), full((D_MODEL, D_MODEL)),
                  full((1, D_MODEL)), full((2, D_MODEL, LANES)), full((1, LANES)), full((1, LANES)),
                  full((tm, tm))],
        out_specs=[rows(D_MODEL), rows(LANES), rows(LANES), rows(LANES), full((1, LANES))],
        out_shape=[jax.ShapeDtypeStruct((T, D_MODEL), F32), jax.ShapeDtypeStruct((T, LANES), jnp.int32),
                   jax.ShapeDtypeStruct((T, LANES), F32), jax.ShapeDtypeStruct((T, LANES), jnp.int32),
                   jax.ShapeDtypeStruct((1, LANES), F32)],
        compiler_params=_params(1),
        name="post",
    )(ua, ub, zm, x, wa, wb, wo, nfw, wr, br, cnt0, earlier)


ROW_SUBLANES = D_MODEL // LANES


def _row_tile(r):
    return pl.ds(pl.multiple_of(r * ROW_SUBLANES, ROW_SUBLANES), ROW_SUBLANES)


def _load_row_tiles(ref, n_rows):
    return [ref[pl.ds(c, n_rows, stride=ROW_SUBLANES), :] for c in range(ROW_SUBLANES)]


def _store_row_tiles(ref, x):
    for c in range(ROW_SUBLANES):
        ref[pl.ds(c, x.shape[0], stride=ROW_SUBLANES), :] = x[:, c * LANES:(c + 1) * LANES]


def _row_copy_wait(src_rows, dst_rows, sem):
    pltpu.make_async_copy(src_rows, dst_rows, sem).wait()


def _dispatch_kernel(dest_ref, xp_ref, xs_ref, nfw_ref, rows_ref, hbuf, sem, *, n_prompt_tiles, n_tiles):
    tm = hbuf.shape[1] // ROW_SUBLANES
    i = pl.program_id(0)
    slot = i % 2

    def wait_slot(s):
        for _ in range(TOP_K):
            _row_copy_wait(hbuf.at[s], rows_ref.at[pl.ds(0, tm * ROW_SUBLANES)], sem.at[s])

    @pl.when(i >= 2)
    def _():
        wait_slot(slot)

    @pl.when(i < n_prompt_tiles)
    def _():
        _store_row_tiles(hbuf.at[slot], _rms(xp_ref[...], nfw_ref[...]))

    @pl.when(i >= n_prompt_tiles)
    def _():
        _store_row_tiles(hbuf.at[slot], _rms(xs_ref[...], nfw_ref[...]))

    def issue(r, carry):
        src = hbuf.at[slot, _row_tile(r)]
        for k in range(TOP_K):
            d = dest_ref[0, 0, r * TOP_K + k]
            pltpu.make_async_copy(src, rows_ref.at[_row_tile(d)], sem.at[slot]).start(priority=k % 2)
        return carry

    lax.fori_loop(0, tm, issue, 0, unroll=ISSUE_UNROLL)

    @pl.when(i == n_tiles - 1)
    def _():
        @pl.when(i >= 1)
        def _():
            wait_slot(1 - slot)
        wait_slot(slot)


def _dispatch(dest, x1_p, x1_s, nfw):
    Tp, Ts = x1_p.shape[0], x1_s.shape[0]
    tm = math.gcd(DISPATCH_TILE, Tp, Ts)
    assert tm % 8 == 0
    n_p, n_s = Tp // tm, Ts // tm
    return pl.pallas_call(
        functools.partial(_dispatch_kernel, n_prompt_tiles=n_p, n_tiles=n_p + n_s),
        grid=(n_p + n_s,),
        in_specs=[pl.BlockSpec((1, 1, tm * TOP_K), lambda i: (i, 0, 0), memory_space=pltpu.SMEM),
                  pl.BlockSpec((tm, D_MODEL), lambda i: (jnp.minimum(i, n_p - 1), 0)),
                  pl.BlockSpec((tm, D_MODEL), lambda i: (jnp.maximum(i - n_p, 0), 0)),
                  pl.BlockSpec((1, D_MODEL), lambda i: (0, 0))],
        out_specs=pl.BlockSpec(memory_space=pl.ANY),
        out_shape=jax.ShapeDtypeStruct(((Tp + Ts) * TOP_K * ROW_SUBLANES, LANES), F32),
        scratch_shapes=[pltpu.VMEM((2, tm * ROW_SUBLANES, LANES), F32), pltpu.SemaphoreType.DMA((2,))],
        compiler_params=_params(1),
        name="dispatch",
    )(dest.reshape(n_p + n_s, 1, tm * TOP_K), x1_p, x1_s, nfw)


def _expert_kernel(blk_ref, exp_ref, lo_ref, hi_ref, xs_ref, wgu_ref, bgu_ref, wd_ref, bd_ref, ys_ref,
                   saved, wgu_bf, wd_bf):
    del blk_ref
    i = pl.program_id(0)
    lo, hi = lo_ref[i], hi_ref[i]

    @pl.when((i == 0) | (exp_ref[i] != exp_ref[jnp.maximum(i - 1, 0)]))
    def _():
        wgu_bf[...] = wgu_ref[0].astype(BF16)
        wd_bf[...] = wd_ref[0].astype(BF16)

    @pl.when(hi > lo)
    def _():
        bm = xs_ref.shape[0] // ROW_SUBLANES

        @pl.when(lo > 0)
        def _():
            saved[...] = ys_ref[...]

        x = jnp.concatenate([t.astype(BF16) for t in _load_row_tiles(xs_ref, bm)], axis=1)
        hgu = _dot(x, wgu_bf[...]) + bgu_ref[0]
        glu = jnp.minimum(hgu[:, :D_FF], SWIGLU_LIMIT)
        lin = jnp.clip(hgu[:, D_FF:], -SWIGLU_LIMIT, SWIGLU_LIMIT)
        act = glu * jax.nn.sigmoid(SWIGLU_ALPHA * glu) * (lin + 1.0)
        _store_row_tiles(ys_ref, _dot(act.astype(BF16), wd_bf[...]) + bd_ref[0])

        @pl.when(lo > 0)
        def _():
            keep = lax.broadcasted_iota(jnp.int32, (bm, 1), 0) < lo
            for c in range(ROW_SUBLANES):
                rows = pl.ds(c, bm, stride=ROW_SUBLANES)
                ys_ref[rows, :] = jnp.where(keep, saved[rows, :], ys_ref[rows, :])


def _experts(items, xs, wgu, bgu, wd, bd):
    P = xs.shape[0] // ROW_SUBLANES
    bm = EXPERT_BLOCK
    n_items = items[0].shape[0]
    blk = lambda i, b, e, lo, hi: (b[i], 0)
    ex3 = lambda i, b, e, lo, hi: (e[i], 0, 0)
    return pl.pallas_call(
        _expert_kernel,
        grid_spec=pltpu.PrefetchScalarGridSpec(
            num_scalar_prefetch=4,
            grid=(n_items,),
            in_specs=[pl.BlockSpec((bm * ROW_SUBLANES, LANES), blk),
                      pl.BlockSpec((1, D_MODEL, 2 * D_FF), ex3),
                      pl.BlockSpec((1, 1, 2 * D_FF), ex3),
                      pl.BlockSpec((1, D_FF, D_MODEL), ex3),
                      pl.BlockSpec((1, 1, D_MODEL), ex3)],
            out_specs=pl.BlockSpec((bm * ROW_SUBLANES, LANES), blk),
            scratch_shapes=[pltpu.VMEM((bm * ROW_SUBLANES, LANES), F32),
                            pltpu.VMEM((D_MODEL, 2 * D_FF), BF16), pltpu.VMEM((D_FF, D_MODEL), BF16)]),
        out_shape=jax.ShapeDtypeStruct((P * ROW_SUBLANES, LANES), F32),
        compiler_params=_params(1),
        name="experts",
    )(*items, xs, wgu, bgu, wd, bd)


def _combine_kernel(dest_ref, dest_next_ref, gate_ref, x1_ref, nw_ref, ys_ref, out_ref, buf, sem, *, n_tiles):
    tm = x1_ref.shape[0]
    i = pl.program_id(0)
    slot = i % 2

    def issue_tile(idx_ref, s):
        def issue(r, carry):
            for k in range(TOP_K):
                d = idx_ref[0, 0, r * TOP_K + k]
                pltpu.make_async_copy(ys_ref.at[_row_tile(d)], buf.at[s, k, _row_tile(r)],
                                      sem.at[s]).start(priority=k % 2)
            return carry

        lax.fori_loop(0, tm, issue, 0, unroll=ISSUE_UNROLL)

    @pl.when(i == 0)
    def _():
        issue_tile(dest_ref, slot)

    @pl.when(i + 1 < n_tiles)
    def _():
        issue_tile(dest_next_ref, 1 - slot)

    for k in range(TOP_K):
        _row_copy_wait(ys_ref.at[pl.ds(0, tm * ROW_SUBLANES)], buf.at[slot, k], sem.at[slot])
    gate = gate_ref[...]
    chunks = [_load_row_tiles(buf.at[slot, k], tm) for k in range(TOP_K)]
    f = []
    for c in range(ROW_SUBLANES):
        fc = gate[:, 0:1] * chunks[0][c]
        for k in range(1, TOP_K):
            fc = fc + gate[:, k:k + 1] * chunks[k][c]
        f.append(fc)
    out_ref[...] = _rms(x1_ref[...] + jnp.concatenate(f, axis=1), nw_ref[...])


def _combine(dest, gate, x1, nw, ys):
    T = x1.shape[0]
    tm = min(COMBINE_TILE, T)
    assert T % tm == 0
    n = T // tm
    dest3 = dest.reshape(n, 1, tm * TOP_K)
    return pl.pallas_call(
        functools.partial(_combine_kernel, n_tiles=n),
        grid=(n,),
        in_specs=[pl.BlockSpec((1, 1, tm * TOP_K), lambda i: (i, 0, 0), memory_space=pltpu.SMEM),
                  pl.BlockSpec((1, 1, tm * TOP_K), lambda i: (jnp.minimum(i + 1, n - 1), 0, 0),
                               memory_space=pltpu.SMEM),
                  pl.BlockSpec((tm, LANES), lambda i: (i, 0)),
                  pl.BlockSpec((tm, D_MODEL), lambda i: (i, 0)),
                  pl.BlockSpec((1, D_MODEL), lambda i: (0, 0)),
                  pl.BlockSpec(memory_space=pl.ANY)],
        out_specs=pl.BlockSpec((tm, D_MODEL), lambda i: (i, 0)),
        out_shape=jax.ShapeDtypeStruct((T, D_MODEL), F32),
        scratch_shapes=[pltpu.VMEM((2, TOP_K, tm * ROW_SUBLANES, LANES), F32), pltpu.SemaphoreType.DMA((2,))],
        compiler_params=_params(1),
        name="combine",
    )(dest3, dest3, gate, x1, nw, ys)


def _rotary_tables(pos):
    inv = ROPE_BASE ** (-jnp.arange(0, DK, 2, dtype=F32) / DK)
    ang = pos.astype(F32)[:, None] * inv[None, :]
    cos, sin = jnp.cos(ang), jnp.sin(ang)
    return jnp.concatenate([cos, cos], axis=1), jnp.concatenate([-sin, sin], axis=1)


def _routing(top_e, rank, counts):
    bm = EXPERT_BLOCK
    n_rows = top_e.size
    n_blocks = n_rows // bm
    n_items = n_blocks + N_EXPERTS - 1
    start = jnp.cumsum(counts) - counts
    end = start + counts
    expert_ids = jnp.arange(N_EXPERTS, dtype=jnp.int32)
    onehot = top_e[:, :, None] == expert_ids[None, None, :]
    dest = (jnp.sum(jnp.where(onehot, start[None, None, :], 0), axis=2) + rank).astype(jnp.int32)

    first_blk = start // bm
    n_it = jnp.where(counts > 0, (end - 1) // bm - first_blk + 1, 0)
    it_end = jnp.cumsum(n_it)
    it_start = it_end - n_it
    j = jnp.arange(n_items, dtype=jnp.int32)
    e_j = jnp.minimum(jnp.sum(j[:, None] >= it_end[None, :], axis=1), N_EXPERTS - 1).astype(jnp.int32)
    valid = j < it_end[-1]
    last_e = jnp.max(jnp.where(counts > 0, expert_ids, 0))
    e_j = jnp.where(valid, e_j, last_e)
    blk = jnp.where(valid, first_blk[e_j] + (j - it_start[e_j]), n_blocks - 1)
    lo = jnp.maximum(start[e_j], blk * bm) - blk * bm
    hi = jnp.minimum(end[e_j], (blk + 1) * bm) - blk * bm
    lo = jnp.where(valid, lo, 0)
    hi = jnp.where(valid, hi, 0)
    items = tuple(a.astype(jnp.int32) for a in (blk, e_j, lo, hi))
    return dest, items


def kernel(x_prompt, x_sample, state_gla, state_ret, meta_tokens, norm_mix_w, w_in, w_alpha_up, b_alpha, gla_norm_w, ret_norm_w, w_branch_gla, w_branch_ret, w_out, norm_ffn_w, w_router, b_router, w_gate_up, b_gate_up, w_down, b_down, norm_final_w):
    B, S, D = x_prompt.shape
    DB, DS, _ = x_sample.shape
    assert D == D_MODEL and S % CHUNK == 0 and DS == CHUNK and state_gla.shape[0] == 1
    l = 0

    offs = [0]
    for w in IN_WIDTHS:
        offs.append(offs[-1] + w)
    cols = [w_in[l][:, offs[i]:offs[i + 1]] for i in range(len(IN_WIDTHS))]
    qa, ka, va, ga, ra, qb, kb, vb, gb, m = cols
    ra = jnp.pad(ra, ((0, 0), (0, LANES - GLA_RANK)))
    ws = jnp.concatenate([qa, ka, va, ga, qb, kb, vb, gb, ra], axis=1).astype(BF16)
    wm = m.astype(BF16)
    wup = jnp.pad(w_alpha_up[l], ((0, LANES - GLA_RANK), (0, 0))).astype(BF16)
    ba = b_alpha[l][None, :]
    nmw = norm_mix_w[l][None, :]
    gnw = gla_norm_w[l][None, :]
    rnw = ret_norm_w[l][None, :]

    def mixer(x2d, pos, sg0, sr0, batch):
        T = x2d.shape[0]
        cs, sn = _rotary_tables(pos)
        reps = max(1, min(ROW_TILE, T) // pos.shape[0])
        zp, zm, dec, bmin = _prep(x2d, nmw, ws, wm, wup, ba, jnp.tile(cs, (reps, 1)), jnp.tile(sn, (reps, 1)))

        def ratio_form():
            ua, ub, sgt, srt = _scan_fast(zp, dec, jnp.swapaxes(sg0, 2, 3), jnp.swapaxes(sr0, 2, 3), gnw, rnw, batch)
            return ua, ub, jnp.swapaxes(sgt, 2, 3), jnp.swapaxes(srt, 2, 3)

        def exact_form():
            return tuple(_scan(_inproj(x2d, nmw, ws), cs, sn, sg0, sr0, wup, ba, gnw, rnw, batch))

        ua, ub, sg, sr = lax.cond(jnp.min(bmin) >= MIN_SAFE_LOG_FORGET, ratio_form, exact_form)
        return ua, ub, zm, sg, sr

    x_meta = jnp.concatenate([jnp.zeros((CHUNK - N_META, D), F32), meta_tokens.astype(F32)], axis=0)
    pos_meta = jnp.maximum(jnp.arange(CHUNK) - (CHUNK - N_META), 0)
    zero_state = jnp.zeros((1, HEADS, DK, DV), F32)
    _, _, _, sg_m, sr_m = mixer(x_meta, pos_meta, zero_state, zero_state, 1)

    xp = x_prompt.reshape(B * S, D)
    xs_ = x_sample.reshape(DB * DS, D)
    sg0 = jnp.broadcast_to(sg_m, (B, HEADS, DK, DV))
    sr0 = jnp.broadcast_to(sr_m, (B, HEADS, DK, DV))
    ua_p, ub_p, zm_p, sg_p, sr_p = mixer(xp, N_META + jnp.arange(S), sg0, sr0, B)
    ua_s, ub_s, zm_s, sg_s, sr_s = mixer(xs_, N_META + PAST_LEN + jnp.arange(DS),
                                         state_gla[l].astype(F32), state_ret[l].astype(F32), DB)

    wa = w_branch_gla[l].astype(BF16)
    wb = w_branch_ret[l].astype(BF16)
    wo = w_out[l].astype(BF16)
    nfw = norm_ffn_w[l][None, :]
    wr = jnp.pad(w_router[l].astype(F32), ((0, 0), (0, LANES - N_EXPERTS)))
    wr_hi = wr.astype(BF16)
    wr_mid = (wr - wr_hi.astype(F32)).astype(BF16)
    wr2 = jnp.stack([wr_hi, wr_mid])
    br = jnp.pad(b_router[l].astype(F32), (0, LANES - N_EXPERTS))[None, :]
    cnt0 = jnp.zeros((1, LANES), F32)
    x1_p, e_p, g_p, r_p, cnt_p = _post(ua_p, ub_p, zm_p, xp, wa, wb, wo, nfw, wr2, br, cnt0)
    x1_s, e_s, g_s, r_s, cnt = _post(ua_s, ub_s, zm_s, xs_, wa, wb, wo, nfw, wr2, br, cnt_p)

    Tp, Ts = B * S, DB * DS
    assert ((Tp + Ts) * TOP_K) % EXPERT_BLOCK == 0
    top_e = jnp.concatenate([e_p[:, :TOP_K], e_s[:, :TOP_K]], axis=0)
    rank = jnp.concatenate([r_p[:, :TOP_K], r_s[:, :TOP_K]], axis=0)
    dest, items = _routing(top_e, rank, cnt[0, :N_EXPERTS].astype(jnp.int32))
    dest = dest.reshape(-1)
    dest_p, dest_s = dest[:Tp * TOP_K], dest[Tp * TOP_K:]

    xs_rows = _dispatch(dest, x1_p, x1_s, nfw)
    ys_rows = _experts(items, xs_rows, w_gate_up[l], b_gate_up[l][:, None, :],
                       w_down[l], b_down[l][:, None, :])
    nw_final = norm_final_w[None, :]
    y_p = _combine(dest_p, g_p, x1_p, nw_final, ys_rows)
    y_s = _combine(dest_s, g_s, x1_s, nw_final, ys_rows)

    dt = state_gla.dtype
    return (y_p.reshape(B, S, D), y_s.reshape(DB, DS, D),
            sg_p[None].astype(dt), sr_p[None].astype(state_ret.dtype),
            sg_s[None].astype(dt), sr_s[None].astype(state_ret.dtype))
```

```python
import functools
import math

import jax
import jax.numpy as jnp
from jax import lax
from jax.experimental import pallas as pl
from jax.experimental.pallas import tpu as pltpu

F32 = jnp.float32
BF16 = jnp.bfloat16

D_MODEL = 1024
CHUNK = 64
N_META = 16
PAST_LEN = 4096
EPS = 1e-5
HEADS = 4
DK = 128
DV = 256
GLA_RANK = 16
GLA_TAU = 16.0
ROPE_BASE = 10000.0
N_EXPERTS = 32
TOP_K = 4
D_FF = 1024
SWIGLU_ALPHA = 1.702
SWIGLU_LIMIT = 7.0
IN_WIDTHS = (HEADS * DK, HEADS * DK, HEADS * DV, HEADS * DV, GLA_RANK,
             HEADS * DK, HEADS * DK, HEADS * DV, HEADS * DV, 2 * D_MODEL)

LANES = 128
SUBLANES = 8
QA, KA, VA, GA, QB, KB, VB, GB, RA = 0, 512, 1024, 2048, 3072, 3584, 4096, 5120, 6144
ZS_W = RA + LANES
ZM_W = 2 * D_MODEL
PQT, PKT, PKD, PVA, PGA, PQB, PKB, PKDB, PVB, PGB = 0, 512, 1024, 1536, 2560, 3584, 4096, 4608, 5120, 6144
PZ_W = PGB + HEADS * DV
MIN_SAFE_LOG_FORGET = -60.0

ROW_TILE = 512
DISPATCH_TILE = 256
COMBINE_TILE = 128
EXPERT_BLOCK = 512
ISSUE_UNROLL = 4
SCAN_CHUNKS_PER_STEP = 8
VMEM_LIMIT = 56 * 1024 * 1024


def _params(n_axes):
    return pltpu.CompilerParams(dimension_semantics=("arbitrary",) * n_axes, vmem_limit_bytes=VMEM_LIMIT)


def _rms(x, w):
    return x * lax.rsqrt(jnp.mean(x * x, axis=-1, keepdims=True) + EPS) * w


def _dot(a, b):
    return jnp.dot(a, b, preferred_element_type=F32)


def _split3(x):
    hi = x.astype(BF16)
    r1 = x - hi.astype(F32)
    mid = r1.astype(BF16)
    lo = (r1 - mid.astype(F32)).astype(BF16)
    return hi, mid, lo


def _log_sigmoid(x):
    return jnp.minimum(x, 0.0) - jnp.log1p(jnp.exp(-jnp.abs(x)))


def _rotate(t, cs, sn):
    return t * cs + pltpu.roll(t, DK // 2, axis=1) * sn


def _prep_kernel(x_ref, nw_ref, ws_ref, wm_ref, wup_ref, ba_ref, tril_ref, cs_ref, sn_ref,
                 zp_ref, zm_ref, dec_ref, bmin_ref):
    tm = x_ref.shape[0]
    n = tm // CHUNK
    h = _rms(x_ref[...], nw_ref[...]).astype(BF16)

    def proj(c0, w):
        return _dot(h, ws_ref[:, c0:c0 + w])

    ra = proj(RA, LANES).astype(BF16)
    la = _log_sigmoid(_dot(ra, wup_ref[...]) + ba_ref[...]) * (1.0 / GLA_TAU)
    hi, mid, lo = _split3(la)
    tril = tril_ref[...]
    b = _dot(tril, hi) + _dot(tril, mid) + _dot(tril, lo)
    b3 = b.reshape(n, CHUNK, HEADS * DK)
    bl3 = b3[:, CHUNK - 1:CHUNK, :]
    dec_ref[...] = jnp.exp(bl3)
    bl = jnp.broadcast_to(bl3, b3.shape).reshape(tm, HEADS * DK)
    bmin = jnp.min(jnp.min(b, axis=0, keepdims=True), axis=1, keepdims=True)
    bmin_ref[0] = jnp.broadcast_to(bmin, (1, LANES))

    q = proj(QA, HEADS * DK) * (DK ** -0.5)
    zp_ref[:, PQT:PQT + HEADS * DK] = (q * jnp.exp(b)).astype(BF16)
    k = proj(KA, HEADS * DK)
    zp_ref[:, PKT:PKT + HEADS * DK] = (k * jnp.exp(-b)).astype(BF16)
    zp_ref[:, PKD:PKD + HEADS * DK] = (k * jnp.exp(bl - b)).astype(BF16)

    half = HEADS * DV // 2
    for j in range(2):
        zp_ref[:, PVA + j * half:PVA + (j + 1) * half] = proj(VA + j * half, half).astype(BF16)
        g = proj(GA + j * half, half)
        zp_ref[:, PGA + j * half:PGA + (j + 1) * half] = (g * jax.nn.sigmoid(g)).astype(BF16)
        zp_ref[:, PVB + j * half:PVB + (j + 1) * half] = proj(VB + j * half, half).astype(BF16)
        g = proj(GB + j * half, half)
        zp_ref[:, PGB + j * half:PGB + (j + 1) * half] = (g * jax.nn.sigmoid(g)).astype(BF16)

    cs = cs_ref[...]
    sn = sn_ref[...]
    qb = proj(QB, HEADS * DK)
    kb = proj(KB, HEADS * DK)
    steps_left = (CHUNK - 1 - (lax.broadcasted_iota(jnp.int32, (tm, 1), 0) % CHUNK)).astype(F32)
    for hd in range(HEADS):
        sl = slice(hd * DK, (hd + 1) * DK)
        zp_ref[:, PQB + hd * DK:PQB + (hd + 1) * DK] = _rotate(qb[:, sl], cs, sn).astype(BF16)
        kr = _rotate(kb[:, sl], cs, sn) * (DK ** -0.5)
        zp_ref[:, PKB + hd * DK:PKB + (hd + 1) * DK] = kr.astype(BF16)
        zp_ref[:, PKDB + hd * DK:PKDB + (hd + 1) * DK] = (kr * jnp.exp(_ret_log_decay(hd) * steps_left)).astype(BF16)

    for c0 in range(0, ZM_W, 512):
        zm_ref[:, c0:c0 + 512] = jax.nn.sigmoid(_dot(h, wm_ref[:, c0:c0 + 512])).astype(BF16)


def _prep(x, nw, ws, wm, wup, ba, cs, sn):
    T = x.shape[0]
    tm = min(ROW_TILE, T)
    assert T % tm == 0 and cs.shape[0] % tm == 0
    n_tab = cs.shape[0] // tm
    n = tm // CHUNK
    r = jnp.arange(tm)
    tril = ((r[:, None] >= r[None, :]) & (r[:, None] // CHUNK == r[None, :] // CHUNK)).astype(BF16)
    const = dict(pipeline_mode=pl.Buffered(1))
    full = lambda shape, **kw: pl.BlockSpec(shape, lambda i: (0,) * len(shape), **kw)
    return pl.pallas_call(
        _prep_kernel,
        grid=(T // tm,),
        in_specs=[pl.BlockSpec((tm, D_MODEL), lambda i: (i, 0)),
                  full((1, D_MODEL)),
                  full((D_MODEL, ZS_W), **const), full((D_MODEL, ZM_W), **const),
                  full((LANES, HEADS * DK)), full((1, HEADS * DK)), full((tm, tm)),
                  pl.BlockSpec((tm, DK), lambda i: (i % n_tab, 0)),
                  pl.BlockSpec((tm, DK), lambda i: (i % n_tab, 0))],
        out_specs=[pl.BlockSpec((tm, PZ_W), lambda i: (i, 0)),
                   pl.BlockSpec((tm, ZM_W), lambda i: (i, 0)),
                   pl.BlockSpec((n, 1, HEADS * DK), lambda i: (i, 0, 0)),
                   pl.BlockSpec((1, 1, LANES), lambda i: (i, 0, 0))],
        out_shape=[jax.ShapeDtypeStruct((T, PZ_W), BF16), jax.ShapeDtypeStruct((T, ZM_W), BF16),
                   jax.ShapeDtypeStruct((T // CHUNK, 1, HEADS * DK), F32),
                   jax.ShapeDtypeStruct((T // tm, 1, LANES), F32)],
        compiler_params=_params(1),
        name="prep",
    )(x, nw, ws, wm, wup, ba, tril, cs, sn)


def _inproj_kernel(x_ref, nw_ref, ws_ref, zs_ref):
    h = _rms(x_ref[...], nw_ref[...]).astype(BF16)
    step = 512
    for c0 in range(0, ZS_W, step):
        c1 = min(c0 + step, ZS_W)
        zs_ref[:, c0:c1] = _dot(h, ws_ref[:, c0:c1]).astype(BF16)


def _inproj(x, nw, ws):
    T = x.shape[0]
    tm = min(ROW_TILE, T)
    assert T % tm == 0
    return pl.pallas_call(
        _inproj_kernel,
        grid=(T // tm,),
        in_specs=[pl.BlockSpec((tm, D_MODEL), lambda i: (i, 0)),
                  pl.BlockSpec((1, D_MODEL), lambda i: (0, 0)),
                  pl.BlockSpec((D_MODEL, ZS_W), lambda i: (0, 0), pipeline_mode=pl.Buffered(1))],
        out_specs=pl.BlockSpec((tm, ZS_W), lambda i: (i, 0)),
        out_shape=jax.ShapeDtypeStruct((T, ZS_W), BF16),
        compiler_params=_params(1),
        name="inproj",
    )(x, nw, ws)


def _ret_log_decay(head):
    return math.log(1.0 - 2.0 ** (-5.0 - head))


NT_DIMS = (((1,), (1,)), ((), ()))
TN_DIMS = (((0,), (0,)), ((), ()))


def _scan_fast_kernel(z_ref, dec_ref, sg0_ref, sr0_ref, gnw_ref, rnw_ref,
                      ua_ref, ub_ref, sg_out_ref, sr_out_ref, sg_scr, sr_scr, *, n_steps):
    step = pl.program_id(1)
    L = CHUNK
    chunks_per_step = dec_ref.shape[0]

    @pl.when(step == 0)
    def _():
        sg_scr[...] = sg0_ref[0]
        sr_scr[...] = sr0_ref[0]

    row = lax.broadcasted_iota(jnp.int32, (L, L), 0)
    col = lax.broadcasted_iota(jnp.int32, (L, L), 1)
    causal = row >= col
    rel = (row - col).astype(F32)
    t_col = lax.broadcasted_iota(jnp.int32, (L, 1), 0).astype(F32)

    def nt(a, b):
        return lax.dot_general(a, b, NT_DIMS, preferred_element_type=F32)

    def tn(a, b):
        return lax.dot_general(a, b, TN_DIMS, preferred_element_type=F32)

    units = [(j, gla, h) for j in range(chunks_per_step) for gla in (True, False) for h in range(HEADS)]

    def cols(gla, h):
        q, k, kd, v, g = (PQT, PKT, PKD, PVA, PGA) if gla else (PQB, PKB, PKDB, PVB, PGB)
        return (slice(q + h * DK, q + (h + 1) * DK), slice(k + h * DK, k + (h + 1) * DK),
                slice(kd + h * DK, kd + (h + 1) * DK), slice(v + h * DV, v + (h + 1) * DV),
                slice(g + h * DV, g + (h + 1) * DV))

    ret_decay = [jnp.exp(jnp.where(causal, _ret_log_decay(h) * rel, -jnp.inf)) for h in range(HEADS)]
    ret_inner = [jnp.exp(_ret_log_decay(h) * (t_col + 1.0)) for h in range(HEADS)]

    intra = {}
    for j, gla, h in units:
        rows = pl.ds(j * L, L)
        qc, kc, _, vc, _ = cols(gla, h)
        s = nt(z_ref[rows, qc], z_ref[rows, kc])
        s = jnp.where(causal, s, 0.0) if gla else s * ret_decay[h]
        intra[j, gla, h] = _dot(s.astype(BF16), z_ref[rows, vc])

    for j, gla, h in units:
        rows = pl.ds(j * L, L)
        qc, _, kdc, vc, gc = cols(gla, h)
        scr, out_ref, nw_ref = (sg_scr, ua_ref, gnw_ref) if gla else (sr_scr, ub_ref, rnw_ref)
        st_old = scr[h]
        inter = nt(z_ref[rows, qc], st_old.astype(BF16))
        update = tn(z_ref[rows, vc], z_ref[rows, kdc])
        if gla:
            scr[h] = dec_ref[j][:, h * DK:(h + 1) * DK] * st_old + update
            o = intra[j, gla, h] + inter
        else:
            scr[h] = math.exp(_ret_log_decay(h) * L) * st_old + update
            o = intra[j, gla, h] + ret_inner[h] * inter
        g = z_ref[rows, gc].astype(F32)
        out_ref[rows, h * DV:(h + 1) * DV] = (_rms(o, nw_ref[...]) * g).astype(BF16)

    @pl.when(step == n_steps - 1)
    def _():
        sg_out_ref[0] = sg_scr[...]
        sr_out_ref[0] = sr_scr[...]


def _scan_fast(zp, dec, sg0t, sr0t, gnw, rnw, batch):
    T = zp.shape[0]
    n_chunks = T // (batch * CHUNK)
    cps = math.gcd(SCAN_CHUNKS_PER_STEP, n_chunks)
    n_steps = n_chunks // cps
    state_spec = pl.BlockSpec((1, HEADS, DV, DK), lambda b, c: (b, 0, 0, 0))
    full = lambda shape: pl.BlockSpec(shape, lambda b, c: (0,) * len(shape))
    rows = lambda w: pl.BlockSpec((cps * CHUNK, w), lambda b, c: (b * n_steps + c, 0))
    return pl.pallas_call(
        functools.partial(_scan_fast_kernel, n_steps=n_steps),
        grid=(batch, n_steps),
        in_specs=[rows(PZ_W),
                  pl.BlockSpec((cps, 1, HEADS * DK), lambda b, c: (b * n_steps + c, 0, 0)),
                  state_spec, state_spec, full((1, DV)), full((1, DV))],
        out_specs=[rows(HEADS * DV), rows(HEADS * DV), state_spec, state_spec],
        out_shape=[jax.ShapeDtypeStruct((T, HEADS * DV), BF16), jax.ShapeDtypeStruct((T, HEADS * DV), BF16),
                   jax.ShapeDtypeStruct((batch, HEADS, DV, DK), F32),
                   jax.ShapeDtypeStruct((batch, HEADS, DV, DK), F32)],
        scratch_shapes=[pltpu.VMEM((HEADS, DV, DK), F32), pltpu.VMEM((HEADS, DV, DK), F32)],
        compiler_params=_params(2),
        name="scanfast",
    )(zp, dec, sg0t, sr0t, gnw, rnw)


def _scan_kernel(z_ref, cs_ref, sn_ref, sg0_ref, sr0_ref, wup_ref, ba_ref, gnw_ref, rnw_ref,
                 ua_ref, ub_ref, sg_out_ref, sr_out_ref, sg_scr, sr_scr, k_scr, b_scr, *, n_chunks):
    c = pl.program_id(1)
    L = CHUNK

    @pl.when(c == 0)
    def _():
        sg_scr[...] = sg0_ref[0]
        sr_scr[...] = sr0_ref[0]

    row = lax.broadcasted_iota(jnp.int32, (L, L), 0)
    col = lax.broadcasted_iota(jnp.int32, (L, L), 1)
    causal = row >= col
    tril = causal.astype(BF16)

    la = _log_sigmoid(_dot(z_ref[:, RA:RA + LANES], wup_ref[...]) + ba_ref[...]) * (1.0 / GLA_TAU)
    hi, mid, lo = _split3(la)
    b_all = _dot(tril, hi) + _dot(tril, mid) + _dot(tril, lo)

    for h in range(HEADS):
        bh = b_all[:, h * DK:(h + 1) * DK]
        q = z_ref[:, QA + h * DK:QA + (h + 1) * DK].astype(F32) * (DK ** -0.5)
        k = z_ref[:, KA + h * DK:KA + (h + 1) * DK].astype(F32)
        v = z_ref[:, VA + h * DV:VA + (h + 1) * DV]
        qt = (q * jnp.exp(bh)).astype(BF16)
        kT = k.T
        bT = bh.T
        blT = bT[:, L - 1:L]
        kdT = (kT * jnp.exp(blT - bT)).astype(BF16)
        k_scr[...] = k
        b_scr[...] = bh

        def score_column(j, acc, q=q, bh=bh):
            e = jnp.exp(jnp.minimum(bh - b_scr[pl.ds(j, 1), :], 0.0))
            colv = jnp.sum(q * k_scr[pl.ds(j, 1), :] * e, axis=1, keepdims=True)
            return jnp.where(col == j, colv, acc)

        s = lax.fori_loop(0, L, score_column, jnp.zeros((L, L), F32))
        s = jnp.where(causal, s, 0.0).astype(BF16)
        s_old = sg_scr[h]
        o = _dot(s, v) + _dot(qt, s_old.astype(BF16))
        sg_scr[h] = jnp.exp(blT) * s_old + _dot(kdT, v)
        g = z_ref[:, GA + h * DV:GA + (h + 1) * DV].astype(F32)
        ua_ref[:, h * DV:(h + 1) * DV] = (_rms(o, gnw_ref[...]) * (g * jax.nn.sigmoid(g))).astype(BF16)

    cs = cs_ref[...]
    sn = sn_ref[...]
    rel = (row - col).astype(F32)
    t_col = lax.broadcasted_iota(jnp.int32, (L, 1), 0).astype(F32)
    s_row = lax.broadcasted_iota(jnp.int32, (1, L), 1).astype(F32)
    for h in range(HEADS):
        lg = _ret_log_decay(h)
        q = z_ref[:, QB + h * DK:QB + (h + 1) * DK].astype(F32)
        k = z_ref[:, KB + h * DK:KB + (h + 1) * DK].astype(F32)
        v = z_ref[:, VB + h * DV:VB + (h + 1) * DV]
        qr = _rotate(q, cs, sn).astype(BF16)
        kr = _rotate(k, cs, sn) * (DK ** -0.5)
        krT = kr.T
        decay = jnp.exp(jnp.where(causal, lg * rel, -jnp.inf))
        s = (_dot(qr, krT.astype(BF16)) * decay).astype(BF16)
        s_old = sr_scr[h]
        inner = jnp.exp(lg * (t_col + 1.0))
        o = _dot(s, v) + inner * _dot(qr, s_old.astype(BF16))
        kdT = (krT * jnp.exp(lg * (L - 1.0 - s_row))).astype(BF16)
        sr_scr[h] = math.exp(lg * L) * s_old + _dot(kdT, v)
        g = z_ref[:, GB + h * DV:GB + (h + 1) * DV].astype(F32)
        ub_ref[:, h * DV:(h + 1) * DV] = (_rms(o, rnw_ref[...]) * (g * jax.nn.sigmoid(g))).astype(BF16)

    @pl.when(c == n_chunks - 1)
    def _():
        sg_out_ref[0] = sg_scr[...]
        sr_out_ref[0] = sr_scr[...]


def _scan(zs, cs, sn, sg0, sr0, wup, ba, gnw, rnw, batch):
    T = zs.shape[0]
    n_chunks = T // (batch * CHUNK)
    state_spec = pl.BlockSpec((1, HEADS, DK, DV), lambda b, c: (b, 0, 0, 0))
    full = lambda shape: pl.BlockSpec(shape, lambda b, c: (0,) * len(shape))
    return pl.pallas_call(
        functools.partial(_scan_kernel, n_chunks=n_chunks),
        grid=(batch, n_chunks),
        in_specs=[pl.BlockSpec((CHUNK, ZS_W), lambda b, c: (b * n_chunks + c, 0)),
                  pl.BlockSpec((CHUNK, DK), lambda b, c: (c, 0)),
                  pl.BlockSpec((CHUNK, DK), lambda b, c: (c, 0)),
                  state_spec, state_spec,
                  full((LANES, HEADS * DK)), full((1, HEADS * DK)), full((1, DV)), full((1, DV))],
        out_specs=[pl.BlockSpec((CHUNK, HEADS * DV), lambda b, c: (b * n_chunks + c, 0)),
                   pl.BlockSpec((CHUNK, HEADS * DV), lambda b, c: (b * n_chunks + c, 0)),
                   state_spec, state_spec],
        out_shape=[jax.ShapeDtypeStruct((T, HEADS * DV), BF16), jax.ShapeDtypeStruct((T, HEADS * DV), BF16),
                   jax.ShapeDtypeStruct((batch, HEADS, DK, DV), F32),
                   jax.ShapeDtypeStruct((batch, HEADS, DK, DV), F32)],
        scratch_shapes=[pltpu.VMEM((HEADS, DK, DV), F32), pltpu.VMEM((HEADS, DK, DV), F32),
                        pltpu.VMEM((CHUNK, DK), F32), pltpu.VMEM((CHUNK, DK), F32)],
        compiler_params=_params(2),
        name="scan",
    )(zs, cs, sn, sg0, sr0, wup, ba, gnw, rnw)


def _post_kernel(ua_ref, ub_ref, zm_ref, x_ref, wa_ref, wb_ref, wo_ref, nfw_ref, wr_ref, br_ref, cnt0_ref,
                 earlier_ref, x1_ref, e_ref, g_ref, rank_ref, cnt_ref):
    @pl.when(pl.program_id(0) == 0)
    def _():
        cnt_ref[...] = cnt0_ref[...]

    a = _dot(ua_ref[...], wa_ref[...])
    b = _dot(ub_ref[...], wb_ref[...])
    merged = zm_ref[:, :D_MODEL].astype(F32) * a + zm_ref[:, D_MODEL:].astype(F32) * b
    x1 = x_ref[...] + _dot(merged.astype(BF16), wo_ref[...])
    x1_ref[...] = x1

    h2 = _rms(x1, nfw_ref[...])
    h_hi, h_mid, _ = _split3(h2)
    w_hi, w_mid = wr_ref[0], wr_ref[1]
    logits = _dot(h_hi, w_hi) + _dot(h_hi, w_mid) + _dot(h_mid, w_hi) + br_ref[...]

    tm = logits.shape[0]
    work = logits.T[:N_EXPERTS, :]
    expert = lax.broadcasted_iota(jnp.int32, work.shape, 0).astype(F32)
    vals, idxs = [], []
    for _ in range(TOP_K):
        m = jnp.max(work, axis=0, keepdims=True)
        idx = jnp.min(jnp.where(work == m, expert, float(N_EXPERTS)), axis=0, keepdims=True)
        vals.append(m)
        idxs.append(idx)
        work = jnp.where(expert == idx, -jnp.inf, work)
    ps = [jnp.exp(v - vals[0]) for v in vals]
    den = ps[0] + ps[1] + ps[2] + ps[3]

    multi_hot = jnp.zeros(work.shape, F32)
    for k in range(TOP_K):
        multi_hot = multi_hot + (expert == idxs[k]).astype(F32)
    before = _dot(multi_hot.astype(BF16), earlier_ref[...]) + cnt_ref[:, 0:1]
    ranks = [jnp.sum(jnp.where(expert == idxs[k], before, 0.0), axis=0, keepdims=True) for k in range(TOP_K)]
    cnt_ref[...] += jnp.sum(multi_hot, axis=1, keepdims=True)

    slot = lax.broadcasted_iota(jnp.int32, (SUBLANES, tm), 0)
    e_out = jnp.zeros((SUBLANES, tm), F32)
    g_out = jnp.zeros((SUBLANES, tm), F32)
    r_out = jnp.zeros((SUBLANES, tm), F32)
    for k in range(TOP_K):
        e_out = jnp.where(slot == k, idxs[k], e_out)
        g_out = jnp.where(slot == k, ps[k] / den, g_out)
        r_out = jnp.where(slot == k, ranks[k], r_out)
    e_ref[...] = e_out.astype(jnp.int32)
    g_ref[...] = g_out
    rank_ref[...] = r_out.astype(jnp.int32)


def _post(ua, ub, zm, x, wa, wb, wo, nfw, wr, br, cnt0):
    T = x.shape[0]
    tm = min(ROW_TILE, T)
    assert T % tm == 0
    rows = lambda w: pl.BlockSpec((tm, w), lambda i: (i, 0))
    full = lambda shape: pl.BlockSpec(shape, lambda i: (0,) * len(shape))
    earlier = (jnp.arange(tm)[:, None] < jnp.arange(tm)[None, :]).astype(BF16)
    per_token = pl.BlockSpec((SUBLANES, tm), lambda i: (0, i))
    return pl.pallas_call(
        _post_kernel,
        grid=(T // tm,),
        in_specs=[rows(HEADS * DV), rows(HEADS * DV), rows(ZM_W), rows(D_MODEL),
                  full((HEADS * DV, D_MODEL)), full((HEADS * DV, D_MODEL)), full((D_MODEL, D_MODEL)),
                  full((1, D_MODEL)), full((2, D_MODEL, LANES)), full((1, LANES)), full((N_EXPERTS, LANES)),
                  full((tm, tm))],
        out_specs=[rows(D_MODEL), per_token, per_token, per_token, full((N_EXPERTS, LANES))],
        out_shape=[jax.ShapeDtypeStruct((T, D_MODEL), F32), jax.ShapeDtypeStruct((SUBLANES, T), jnp.int32),
                   jax.ShapeDtypeStruct((SUBLANES, T), F32), jax.ShapeDtypeStruct((SUBLANES, T), jnp.int32),
                   jax.ShapeDtypeStruct((N_EXPERTS, LANES), F32)],
        compiler_params=_params(1),
        name="post",
    )(ua, ub, zm, x, wa, wb, wo, nfw, wr, br, cnt0, earlier)


ROW_SUBLANES = D_MODEL // LANES


def _row_tile(r):
    return pl.ds(pl.multiple_of(r * ROW_SUBLANES, ROW_SUBLANES), ROW_SUBLANES)


def _load_row_tiles(ref, n_rows):
    return [ref[pl.ds(c, n_rows, stride=ROW_SUBLANES), :] for c in range(ROW_SUBLANES)]


def _store_row_tiles(ref, x):
    for c in range(ROW_SUBLANES):
        ref[pl.ds(c, x.shape[0], stride=ROW_SUBLANES), :] = x[:, c * LANES:(c + 1) * LANES]


def _row_copy_wait(src_rows, dst_rows, sem):
    pltpu.make_async_copy(src_rows, dst_rows, sem).wait()


def _dispatch_kernel(dest_ref, xp_ref, xs_ref, nfw_ref, rows_ref, hbuf, sem, *, n_prompt_tiles, n_tiles):
    tm = hbuf.shape[1] // ROW_SUBLANES
    i = pl.program_id(0)
    slot = i % 2

    def wait_slot(s):
        for _ in range(TOP_K):
            _row_copy_wait(hbuf.at[s], rows_ref.at[pl.ds(0, tm * ROW_SUBLANES)], sem.at[s])

    @pl.when(i >= 2)
    def _():
        wait_slot(slot)

    @pl.when(i < n_prompt_tiles)
    def _():
        _store_row_tiles(hbuf.at[slot], _rms(xp_ref[...], nfw_ref[...]))

    @pl.when(i >= n_prompt_tiles)
    def _():
        _store_row_tiles(hbuf.at[slot], _rms(xs_ref[...], nfw_ref[...]))

    def issue(r, carry):
        src = hbuf.at[slot, _row_tile(r)]
        for k in range(TOP_K):
            d = dest_ref[0, 0, r * TOP_K + k]
            pltpu.make_async_copy(src, rows_ref.at[_row_tile(d)], sem.at[slot]).start(priority=k % 2)
        return carry

    lax.fori_loop(0, tm, issue, 0, unroll=ISSUE_UNROLL)

    @pl.when(i == n_tiles - 1)
    def _():
        @pl.when(i >= 1)
        def _():
            wait_slot(1 - slot)
        wait_slot(slot)


def _dispatch(dest, x1_p, x1_s, nfw):
    Tp, Ts = x1_p.shape[0], x1_s.shape[0]
    tm = math.gcd(DISPATCH_TILE, Tp, Ts)
    assert tm % 8 == 0
    n_p, n_s = Tp // tm, Ts // tm
    return pl.pallas_call(
        functools.partial(_dispatch_kernel, n_prompt_tiles=n_p, n_tiles=n_p + n_s),
        grid=(n_p + n_s,),
        in_specs=[pl.BlockSpec((1, 1, tm * TOP_K), lambda i: (i, 0, 0), memory_space=pltpu.SMEM),
                  pl.BlockSpec((tm, D_MODEL), lambda i: (jnp.minimum(i, n_p - 1), 0)),
                  pl.BlockSpec((tm, D_MODEL), lambda i: (jnp.maximum(i - n_p, 0), 0)),
                  pl.BlockSpec((1, D_MODEL), lambda i: (0, 0))],
        out_specs=pl.BlockSpec(memory_space=pl.ANY),
        out_shape=jax.ShapeDtypeStruct(((Tp + Ts) * TOP_K * ROW_SUBLANES, LANES), F32),
        scratch_shapes=[pltpu.VMEM((2, tm * ROW_SUBLANES, LANES), F32), pltpu.SemaphoreType.DMA((2,))],
        compiler_params=_params(1),
        name="dispatch",
    )(dest.reshape(n_p + n_s, 1, tm * TOP_K), x1_p, x1_s, nfw)


def _expert_kernel(blk_ref, exp_ref, lo_ref, hi_ref, xs_ref, wgu_ref, bgu_ref, wd_ref, bd_ref, ys_ref,
                   saved, wgu_bf, wd_bf):
    del blk_ref
    i = pl.program_id(0)
    lo, hi = lo_ref[i], hi_ref[i]

    @pl.when((i == 0) | (exp_ref[i] != exp_ref[jnp.maximum(i - 1, 0)]))
    def _():
        wgu_bf[...] = wgu_ref[0].astype(BF16)
        wd_bf[...] = wd_ref[0].astype(BF16)

    @pl.when(hi > lo)
    def _():
        bm = xs_ref.shape[0] // ROW_SUBLANES

        @pl.when(lo > 0)
        def _():
            saved[...] = ys_ref[...]

        x = jnp.concatenate([t.astype(BF16) for t in _load_row_tiles(xs_ref, bm)], axis=1)
        hgu = _dot(x, wgu_bf[...]) + bgu_ref[0]
        glu = jnp.minimum(hgu[:, :D_FF], SWIGLU_LIMIT)
        lin = jnp.clip(hgu[:, D_FF:], -SWIGLU_LIMIT, SWIGLU_LIMIT)
        act = glu * jax.nn.sigmoid(SWIGLU_ALPHA * glu) * (lin + 1.0)
        _store_row_tiles(ys_ref, _dot(act.astype(BF16), wd_bf[...]) + bd_ref[0])

        @pl.when(lo > 0)
        def _():
            keep = lax.broadcasted_iota(jnp.int32, (bm, 1), 0) < lo
            for c in range(ROW_SUBLANES):
                rows = pl.ds(c, bm, stride=ROW_SUBLANES)
                ys_ref[rows, :] = jnp.where(keep, saved[rows, :], ys_ref[rows, :])


def _experts(items, xs, wgu, bgu, wd, bd):
    P = xs.shape[0] // ROW_SUBLANES
    bm = EXPERT_BLOCK
    n_items = items[0].shape[0]
    blk = lambda i, b, e, lo, hi: (b[i], 0)
    ex3 = lambda i, b, e, lo, hi: (e[i], 0, 0)
    return pl.pallas_call(
        _expert_kernel,
        grid_spec=pltpu.PrefetchScalarGridSpec(
            num_scalar_prefetch=4,
            grid=(n_items,),
            in_specs=[pl.BlockSpec((bm * ROW_SUBLANES, LANES), blk),
                      pl.BlockSpec((1, D_MODEL, 2 * D_FF), ex3),
                      pl.BlockSpec((1, 1, 2 * D_FF), ex3),
                      pl.BlockSpec((1, D_FF, D_MODEL), ex3),
                      pl.BlockSpec((1, 1, D_MODEL), ex3)],
            out_specs=pl.BlockSpec((bm * ROW_SUBLANES, LANES), blk),
            scratch_shapes=[pltpu.VMEM((bm * ROW_SUBLANES, LANES), F32),
                            pltpu.VMEM((D_MODEL, 2 * D_FF), BF16), pltpu.VMEM((D_FF, D_MODEL), BF16)]),
        out_shape=jax.ShapeDtypeStruct((P * ROW_SUBLANES, LANES), F32),
        compiler_params=_params(1),
        name="experts",
    )(*items, xs, wgu, bgu, wd, bd)


def _combine_kernel(dest_ref, dest_next_ref, gate_ref, x1_ref, nw_ref, ys_ref, out_ref, buf, sem, *, n_tiles):
    tm = x1_ref.shape[0]
    i = pl.program_id(0)
    slot = i % 2

    def issue_tile(idx_ref, s):
        def issue(r, carry):
            for k in range(TOP_K):
                d = idx_ref[0, 0, r * TOP_K + k]
                pltpu.make_async_copy(ys_ref.at[_row_tile(d)], buf.at[s, k, _row_tile(r)],
                                      sem.at[s]).start(priority=k % 2)
            return carry

        lax.fori_loop(0, tm, issue, 0, unroll=ISSUE_UNROLL)

    @pl.when(i == 0)
    def _():
        issue_tile(dest_ref, slot)

    @pl.when(i + 1 < n_tiles)
    def _():
        issue_tile(dest_next_ref, 1 - slot)

    for k in range(TOP_K):
        _row_copy_wait(ys_ref.at[pl.ds(0, tm * ROW_SUBLANES)], buf.at[slot, k], sem.at[slot])
    gate = jnp.concatenate([gate_ref[...], jnp.zeros((tm - SUBLANES, tm), F32)], axis=0).T
    chunks = [_load_row_tiles(buf.at[slot, k], tm) for k in range(TOP_K)]
    f = []
    for c in range(ROW_SUBLANES):
        fc = gate[:, 0:1] * chunks[0][c]
        for k in range(1, TOP_K):
            fc = fc + gate[:, k:k + 1] * chunks[k][c]
        f.append(fc)
    out_ref[...] = _rms(x1_ref[...] + jnp.concatenate(f, axis=1), nw_ref[...])


def _combine(dest, gate, x1, nw, ys):
    T = x1.shape[0]
    tm = min(COMBINE_TILE, T)
    assert T % tm == 0
    n = T // tm
    dest3 = dest.reshape(n, 1, tm * TOP_K)
    return pl.pallas_call(
        functools.partial(_combine_kernel, n_tiles=n),
        grid=(n,),
        in_specs=[pl.BlockSpec((1, 1, tm * TOP_K), lambda i: (i, 0, 0), memory_space=pltpu.SMEM),
                  pl.BlockSpec((1, 1, tm * TOP_K), lambda i: (jnp.minimum(i + 1, n - 1), 0, 0),
                               memory_space=pltpu.SMEM),
                  pl.BlockSpec((SUBLANES, tm), lambda i: (0, i)),
                  pl.BlockSpec((tm, D_MODEL), lambda i: (i, 0)),
                  pl.BlockSpec((1, D_MODEL), lambda i: (0, 0)),
                  pl.BlockSpec(memory_space=pl.ANY)],
        out_specs=pl.BlockSpec((tm, D_MODEL), lambda i: (i, 0)),
        out_shape=jax.ShapeDtypeStruct((T, D_MODEL), F32),
        scratch_shapes=[pltpu.VMEM((2, TOP_K, tm * ROW_SUBLANES, LANES), F32), pltpu.SemaphoreType.DMA((2,))],
        compiler_params=_params(1),
        name="combine",
    )(dest3, dest3, gate, x1, nw, ys)


def _rotary_tables(pos):
    inv = ROPE_BASE ** (-jnp.arange(0, DK, 2, dtype=F32) / DK)
    ang = pos.astype(F32)[:, None] * inv[None, :]
    cos, sin = jnp.cos(ang), jnp.sin(ang)
    return jnp.concatenate([cos, cos], axis=1), jnp.concatenate([-sin, sin], axis=1)


def _routing(top_e, rank, counts):
    bm = EXPERT_BLOCK
    n_rows = top_e.size
    n_blocks = n_rows // bm
    n_items = n_blocks + N_EXPERTS - 1
    start = jnp.cumsum(counts) - counts
    end = start + counts
    expert_ids = jnp.arange(N_EXPERTS, dtype=jnp.int32)
    onehot = top_e[None, :, :] == expert_ids[:, None, None]
    dest = (jnp.sum(jnp.where(onehot, start[:, None, None], 0), axis=0) + rank).astype(jnp.int32)

    first_blk = start // bm
    n_it = jnp.where(counts > 0, (end - 1) // bm - first_blk + 1, 0)
    it_end = jnp.cumsum(n_it)
    it_start = it_end - n_it
    j = jnp.arange(n_items, dtype=jnp.int32)
    e_j = jnp.minimum(jnp.sum(j[:, None] >= it_end[None, :], axis=1), N_EXPERTS - 1).astype(jnp.int32)
    valid = j < it_end[-1]
    last_e = jnp.max(jnp.where(counts > 0, expert_ids, 0))
    e_j = jnp.where(valid, e_j, last_e)
    blk = jnp.where(valid, first_blk[e_j] + (j - it_start[e_j]), n_blocks - 1)
    lo = jnp.maximum(start[e_j], blk * bm) - blk * bm
    hi = jnp.minimum(end[e_j], (blk + 1) * bm) - blk * bm
    lo = jnp.where(valid, lo, 0)
    hi = jnp.where(valid, hi, 0)
    items = tuple(a.astype(jnp.int32) for a in (blk, e_j, lo, hi))
    return dest, items


def kernel(x_prompt, x_sample, state_gla, state_ret, meta_tokens, norm_mix_w, w_in, w_alpha_up, b_alpha, gla_norm_w, ret_norm_w, w_branch_gla, w_branch_ret, w_out, norm_ffn_w, w_router, b_router, w_gate_up, b_gate_up, w_down, b_down, norm_final_w):
    B, S, D = x_prompt.shape
    DB, DS, _ = x_sample.shape
    assert D == D_MODEL and S % CHUNK == 0 and DS == CHUNK and state_gla.shape[0] == 1
    l = 0

    offs = [0]
    for w in IN_WIDTHS:
        offs.append(offs[-1] + w)
    cols = [w_in[l][:, offs[i]:offs[i + 1]] for i in range(len(IN_WIDTHS))]
    qa, ka, va, ga, ra, qb, kb, vb, gb, m = cols
    ra = jnp.pad(ra, ((0, 0), (0, LANES - GLA_RANK)))
    ws = jnp.concatenate([qa, ka, va, ga, qb, kb, vb, gb, ra], axis=1).astype(BF16)
    wm = m.astype(BF16)
    wup = jnp.pad(w_alpha_up[l], ((0, LANES - GLA_RANK), (0, 0))).astype(BF16)
    ba = b_alpha[l][None, :]
    nmw = norm_mix_w[l][None, :]
    gnw = gla_norm_w[l][None, :]
    rnw = ret_norm_w[l][None, :]

    def mixer(x2d, pos, sg0, sr0, batch):
        T = x2d.shape[0]
        cs, sn = _rotary_tables(pos)
        reps = max(1, min(ROW_TILE, T) // pos.shape[0])
        zp, zm, dec, bmin = _prep(x2d, nmw, ws, wm, wup, ba, jnp.tile(cs, (reps, 1)), jnp.tile(sn, (reps, 1)))

        def ratio_form():
            ua, ub, sgt, srt = _scan_fast(zp, dec, jnp.swapaxes(sg0, 2, 3), jnp.swapaxes(sr0, 2, 3), gnw, rnw, batch)
            return ua, ub, jnp.swapaxes(sgt, 2, 3), jnp.swapaxes(srt, 2, 3)

        def exact_form():
            return tuple(_scan(_inproj(x2d, nmw, ws), cs, sn, sg0, sr0, wup, ba, gnw, rnw, batch))

        ua, ub, sg, sr = lax.cond(jnp.min(bmin) >= MIN_SAFE_LOG_FORGET, ratio_form, exact_form)
        return ua, ub, zm, sg, sr

    x_meta = jnp.concatenate([jnp.zeros((CHUNK - N_META, D), F32), meta_tokens.astype(F32)], axis=0)
    pos_meta = jnp.maximum(jnp.arange(CHUNK) - (CHUNK - N_META), 0)
    zero_state = jnp.zeros((1, HEADS, DK, DV), F32)
    _, _, _, sg_m, sr_m = mixer(x_meta, pos_meta, zero_state, zero_state, 1)

    xp = x_prompt.reshape(B * S, D)
    xs_ = x_sample.reshape(DB * DS, D)
    sg0 = jnp.broadcast_to(sg_m, (B, HEADS, DK, DV))
    sr0 = jnp.broadcast_to(sr_m, (B, HEADS, DK, DV))
    ua_p, ub_p, zm_p, sg_p, sr_p = mixer(xp, N_META + jnp.arange(S), sg0, sr0, B)
    ua_s, ub_s, zm_s, sg_s, sr_s = mixer(xs_, N_META + PAST_LEN + jnp.arange(DS),
                                         state_gla[l].astype(F32), state_ret[l].astype(F32), DB)

    wa = w_branch_gla[l].astype(BF16)
    wb = w_branch_ret[l].astype(BF16)
    wo = w_out[l].astype(BF16)
    nfw = norm_ffn_w[l][None, :]
    wr = jnp.pad(w_router[l].astype(F32), ((0, 0), (0, LANES - N_EXPERTS)))
    wr_hi = wr.astype(BF16)
    wr_mid = (wr - wr_hi.astype(F32)).astype(BF16)
    wr2 = jnp.stack([wr_hi, wr_mid])
    br = jnp.pad(b_router[l].astype(F32), (0, LANES - N_EXPERTS))[None, :]
    cnt0 = jnp.zeros((N_EXPERTS, LANES), F32)
    x1_p, e_p, g_p, r_p, cnt_p = _post(ua_p, ub_p, zm_p, xp, wa, wb, wo, nfw, wr2, br, cnt0)
    x1_s, e_s, g_s, r_s, cnt = _post(ua_s, ub_s, zm_s, xs_, wa, wb, wo, nfw, wr2, br, cnt_p)

    Tp, Ts = B * S, DB * DS
    assert ((Tp + Ts) * TOP_K) % EXPERT_BLOCK == 0
    top_e = jnp.concatenate([e_p[:TOP_K], e_s[:TOP_K]], axis=1)
    rank = jnp.concatenate([r_p[:TOP_K], r_s[:TOP_K]], axis=1)
    dest, items = _routing(top_e, rank, cnt[:, 0].astype(jnp.int32))
    dest = dest.T.reshape(-1)
    dest_p, dest_s = dest[:Tp * TOP_K], dest[Tp * TOP_K:]

    xs_rows = _dispatch(dest, x1_p, x1_s, nfw)
    ys_rows = _experts(items, xs_rows, w_gate_up[l], b_gate_up[l][:, None, :],
                       w_down[l], b_down[l][:, None, :])
    nw_final = norm_final_w[None, :]
    y_p = _combine(dest_p, g_p, x1_p, nw_final, ys_rows)
    y_s = _combine(dest_s, g_s, x1_s, nw_final, ys_rows)

    dt = state_gla.dtype
    return (y_p.reshape(B, S, D), y_s.reshape(DB, DS, D),
            sg_p[None].astype(dt), sr_p[None].astype(state_ret.dtype),
            sg_s[None].astype(dt), sr_s[None].astype(state_ret.dtype))
```

```python
import functools
import math

import jax
import jax.numpy as jnp
from jax import lax
from jax.experimental import pallas as pl
from jax.experimental.pallas import tpu as pltpu

F32 = jnp.float32
BF16 = jnp.bfloat16

D_MODEL = 1024
CHUNK = 64
N_META = 16
PAST_LEN = 4096
EPS = 1e-5
HEADS = 4
DK = 128
DV = 256
GLA_RANK = 16
GLA_TAU = 16.0
ROPE_BASE = 10000.0
N_EXPERTS = 32
TOP_K = 4
D_FF = 1024
SWIGLU_ALPHA = 1.702
SWIGLU_LIMIT = 7.0
IN_WIDTHS = (HEADS * DK, HEADS * DK, HEADS * DV, HEADS * DV, GLA_RANK,
             HEADS * DK, HEADS * DK, HEADS * DV, HEADS * DV, 2 * D_MODEL)

LANES = 128
SUBLANES = 8
QA, KA, VA, GA, QB, KB, VB, GB, RA = 0, 512, 1024, 2048, 3072, 3584, 4096, 5120, 6144
ZS_W = RA + LANES
ZM_W = 2 * D_MODEL
PQT, PKT, PKD, PVA, PGA, PQB, PKB, PKDB, PVB, PGB = 0, 512, 1024, 1536, 2560, 3584, 4096, 4608, 5120, 6144
PZ_W = PGB + HEADS * DV
MIN_SAFE_LOG_FORGET = -60.0

ROW_TILE = 512
DISPATCH_TILE = 256
COMBINE_TILE = 128
EXPERT_BLOCK = 512
ISSUE_UNROLL = 4
SCAN_CHUNKS_PER_STEP = 8
VMEM_LIMIT = 56 * 1024 * 1024


def _params(n_axes):
    return pltpu.CompilerParams(dimension_semantics=("arbitrary",) * n_axes, vmem_limit_bytes=VMEM_LIMIT)


def _rms(x, w):
    return x * lax.rsqrt(jnp.mean(x * x, axis=-1, keepdims=True) + EPS) * w


def _dot(a, b):
    return jnp.dot(a, b, preferred_element_type=F32)


def _split3(x):
    hi = x.astype(BF16)
    r1 = x - hi.astype(F32)
    mid = r1.astype(BF16)
    lo = (r1 - mid.astype(F32)).astype(BF16)
    return hi, mid, lo


def _log_sigmoid(x):
    return jnp.minimum(x, 0.0) - jnp.log1p(jnp.exp(-jnp.abs(x)))


def _rotate(t, cs, sn):
    return t * cs + pltpu.roll(t, DK // 2, axis=1) * sn


def _prep_kernel(x_ref, nw_ref, ws_ref, wm_ref, wup_ref, ba_ref, tril_ref, cs_ref, sn_ref,
                 zp_ref, zm_ref, dec_ref, bmin_ref):
    tm = x_ref.shape[0]
    n = tm // CHUNK
    h = _rms(x_ref[...], nw_ref[...]).astype(BF16)

    def proj(c0, w):
        return _dot(h, ws_ref[:, c0:c0 + w])

    ra = proj(RA, LANES).astype(BF16)
    la = _log_sigmoid(_dot(ra, wup_ref[...]) + ba_ref[...]) * (1.0 / GLA_TAU)
    parts = _split3(la)[:2]
    tril = tril_ref[...]
    b = jnp.concatenate(
        [sum(_dot(tril, p[j * CHUNK:(j + 1) * CHUNK]) for p in parts) for j in range(n)], axis=0)
    b3 = b.reshape(n, CHUNK, HEADS * DK)
    bl3 = b3[:, CHUNK - 1:CHUNK, :]
    dec_ref[...] = jnp.exp(bl3)
    bl = jnp.broadcast_to(bl3, b3.shape).reshape(tm, HEADS * DK)
    bmin = jnp.min(jnp.min(b, axis=0, keepdims=True), axis=1, keepdims=True)
    bmin_ref[0] = jnp.broadcast_to(bmin, (1, LANES))

    q = proj(QA, HEADS * DK) * (DK ** -0.5)
    zp_ref[:, PQT:PQT + HEADS * DK] = (q * jnp.exp(b)).astype(BF16)
    k = proj(KA, HEADS * DK)
    zp_ref[:, PKT:PKT + HEADS * DK] = (k * jnp.exp(-b)).astype(BF16)
    zp_ref[:, PKD:PKD + HEADS * DK] = (k * jnp.exp(bl - b)).astype(BF16)

    half = HEADS * DV // 2
    for j in range(2):
        zp_ref[:, PVA + j * half:PVA + (j + 1) * half] = proj(VA + j * half, half).astype(BF16)
        g = proj(GA + j * half, half)
        zp_ref[:, PGA + j * half:PGA + (j + 1) * half] = (g * jax.nn.sigmoid(g)).astype(BF16)
        zp_ref[:, PVB + j * half:PVB + (j + 1) * half] = proj(VB + j * half, half).astype(BF16)
        g = proj(GB + j * half, half)
        zp_ref[:, PGB + j * half:PGB + (j + 1) * half] = (g * jax.nn.sigmoid(g)).astype(BF16)

    cs = cs_ref[...]
    sn = sn_ref[...]
    qb = proj(QB, HEADS * DK)
    kb = proj(KB, HEADS * DK)
    steps_left = (CHUNK - 1 - (lax.broadcasted_iota(jnp.int32, (tm, 1), 0) % CHUNK)).astype(F32)
    for hd in range(HEADS):
        sl = slice(hd * DK, (hd + 1) * DK)
        zp_ref[:, PQB + hd * DK:PQB + (hd + 1) * DK] = _rotate(qb[:, sl], cs, sn).astype(BF16)
        kr = _rotate(kb[:, sl], cs, sn) * (DK ** -0.5)
        zp_ref[:, PKB + hd * DK:PKB + (hd + 1) * DK] = kr.astype(BF16)
        zp_ref[:, PKDB + hd * DK:PKDB + (hd + 1) * DK] = (kr * jnp.exp(_ret_log_decay(hd) * steps_left)).astype(BF16)

    for c0 in range(0, ZM_W, 512):
        zm_ref[:, c0:c0 + 512] = jax.nn.sigmoid(_dot(h, wm_ref[:, c0:c0 + 512])).astype(BF16)


def _prep(x, nw, ws, wm, wup, ba, cs, sn):
    T = x.shape[0]
    tm = min(ROW_TILE, T)
    assert T % tm == 0 and cs.shape[0] % tm == 0
    n_tab = cs.shape[0] // tm
    n = tm // CHUNK
    r = jnp.arange(CHUNK)
    tril = (r[:, None] >= r[None, :]).astype(BF16)
    const = dict(pipeline_mode=pl.Buffered(1))
    full = lambda shape, **kw: pl.BlockSpec(shape, lambda i: (0,) * len(shape), **kw)
    return pl.pallas_call(
        _prep_kernel,
        grid=(T // tm,),
        in_specs=[pl.BlockSpec((tm, D_MODEL), lambda i: (i, 0)),
                  full((1, D_MODEL)),
                  full((D_MODEL, ZS_W), **const), full((D_MODEL, ZM_W), **const),
                  full((LANES, HEADS * DK)), full((1, HEADS * DK)), full((CHUNK, CHUNK)),
                  pl.BlockSpec((tm, DK), lambda i: (i % n_tab, 0)),
                  pl.BlockSpec((tm, DK), lambda i: (i % n_tab, 0))],
        out_specs=[pl.BlockSpec((tm, PZ_W), lambda i: (i, 0)),
                   pl.BlockSpec((tm, ZM_W), lambda i: (i, 0)),
                   pl.BlockSpec((n, 1, HEADS * DK), lambda i: (i, 0, 0)),
                   pl.BlockSpec((1, 1, LANES), lambda i: (i, 0, 0))],
        out_shape=[jax.ShapeDtypeStruct((T, PZ_W), BF16), jax.ShapeDtypeStruct((T, ZM_W), BF16),
                   jax.ShapeDtypeStruct((T // CHUNK, 1, HEADS * DK), F32),
                   jax.ShapeDtypeStruct((T // tm, 1, LANES), F32)],
        compiler_params=_params(1),
        name="prep",
    )(x, nw, ws, wm, wup, ba, tril, cs, sn)


def _inproj_kernel(x_ref, nw_ref, ws_ref, zs_ref):
    h = _rms(x_ref[...], nw_ref[...]).astype(BF16)
    step = 512
    for c0 in range(0, ZS_W, step):
        c1 = min(c0 + step, ZS_W)
        zs_ref[:, c0:c1] = _dot(h, ws_ref[:, c0:c1]).astype(BF16)


def _inproj(x, nw, ws):
    T = x.shape[0]
    tm = min(ROW_TILE, T)
    assert T % tm == 0
    return pl.pallas_call(
        _inproj_kernel,
        grid=(T // tm,),
        in_specs=[pl.BlockSpec((tm, D_MODEL), lambda i: (i, 0)),
                  pl.BlockSpec((1, D_MODEL), lambda i: (0, 0)),
                  pl.BlockSpec((D_MODEL, ZS_W), lambda i: (0, 0), pipeline_mode=pl.Buffered(1))],
        out_specs=pl.BlockSpec((tm, ZS_W), lambda i: (i, 0)),
        out_shape=jax.ShapeDtypeStruct((T, ZS_W), BF16),
        compiler_params=_params(1),
        name="inproj",
    )(x, nw, ws)


def _ret_log_decay(head):
    return math.log(1.0 - 2.0 ** (-5.0 - head))


NT_DIMS = (((1,), (1,)), ((), ()))
TN_DIMS = (((0,), (0,)), ((), ()))


def _scan_fast_kernel(z_ref, dec_ref, sg0_ref, sr0_ref, gnw_ref, rnw_ref,
                      ua_ref, ub_ref, sg_out_ref, sr_out_ref, sg_scr, sr_scr, *, n_steps):
    step = pl.program_id(1)
    L = CHUNK
    chunks_per_step = dec_ref.shape[0]

    @pl.when(step == 0)
    def _():
        sg_scr[...] = sg0_ref[0]
        sr_scr[...] = sr0_ref[0]

    row = lax.broadcasted_iota(jnp.int32, (L, L), 0)
    col = lax.broadcasted_iota(jnp.int32, (L, L), 1)
    causal = row >= col
    rel = (row - col).astype(F32)
    t_col = lax.broadcasted_iota(jnp.int32, (L, 1), 0).astype(F32)

    def nt(a, b):
        return lax.dot_general(a, b, NT_DIMS, preferred_element_type=F32)

    def tn(a, b):
        return lax.dot_general(a, b, TN_DIMS, preferred_element_type=F32)

    units = [(j, gla, h) for j in range(chunks_per_step) for gla in (True, False) for h in range(HEADS)]

    def cols(gla, h):
        q, k, kd, v, g = (PQT, PKT, PKD, PVA, PGA) if gla else (PQB, PKB, PKDB, PVB, PGB)
        return (slice(q + h * DK, q + (h + 1) * DK), slice(k + h * DK, k + (h + 1) * DK),
                slice(kd + h * DK, kd + (h + 1) * DK), slice(v + h * DV, v + (h + 1) * DV),
                slice(g + h * DV, g + (h + 1) * DV))

    ret_decay = [jnp.exp(jnp.where(causal, _ret_log_decay(h) * rel, -jnp.inf)) for h in range(HEADS)]
    ret_inner = [jnp.exp(_ret_log_decay(h) * (t_col + 1.0)) for h in range(HEADS)]

    intra = {}
    for j, gla, h in units:
        rows = pl.ds(j * L, L)
        qc, kc, _, vc, _ = cols(gla, h)
        s = nt(z_ref[rows, qc], z_ref[rows, kc])
        s = jnp.where(causal, s, 0.0) if gla else s * ret_decay[h]
        intra[j, gla, h] = _dot(s.astype(BF16), z_ref[rows, vc])

    for j, gla, h in units:
        rows = pl.ds(j * L, L)
        qc, _, kdc, vc, gc = cols(gla, h)
        scr, out_ref, nw_ref = (sg_scr, ua_ref, gnw_ref) if gla else (sr_scr, ub_ref, rnw_ref)
        st_old = scr[h]
        inter = nt(z_ref[rows, qc], st_old.astype(BF16))
        update = tn(z_ref[rows, vc], z_ref[rows, kdc])
        if gla:
            scr[h] = dec_ref[j][:, h * DK:(h + 1) * DK] * st_old + update
            o = intra[j, gla, h] + inter
        else:
            scr[h] = math.exp(_ret_log_decay(h) * L) * st_old + update
            o = intra[j, gla, h] + ret_inner[h] * inter
        g = z_ref[rows, gc].astype(F32)
        out_ref[rows, h * DV:(h + 1) * DV] = (_rms(o, nw_ref[...]) * g).astype(BF16)

    @pl.when(step == n_steps - 1)
    def _():
        sg_out_ref[0] = sg_scr[...]
        sr_out_ref[0] = sr_scr[...]


def _scan_fast(zp, dec, sg0t, sr0t, gnw, rnw, batch):
    T = zp.shape[0]
    n_chunks = T // (batch * CHUNK)
    cps = math.gcd(SCAN_CHUNKS_PER_STEP, n_chunks)
    n_steps = n_chunks // cps
    state_spec = pl.BlockSpec((1, HEADS, DV, DK), lambda b, c: (b, 0, 0, 0))
    full = lambda shape: pl.BlockSpec(shape, lambda b, c: (0,) * len(shape))
    rows = lambda w: pl.BlockSpec((cps * CHUNK, w), lambda b, c: (b * n_steps + c, 0))
    return pl.pallas_call(
        functools.partial(_scan_fast_kernel, n_steps=n_steps),
        grid=(batch, n_steps),
        in_specs=[rows(PZ_W),
                  pl.BlockSpec((cps, 1, HEADS * DK), lambda b, c: (b * n_steps + c, 0, 0)),
                  state_spec, state_spec, full((1, DV)), full((1, DV))],
        out_specs=[rows(HEADS * DV), rows(HEADS * DV), state_spec, state_spec],
        out_shape=[jax.ShapeDtypeStruct((T, HEADS * DV), BF16), jax.ShapeDtypeStruct((T, HEADS * DV), BF16),
                   jax.ShapeDtypeStruct((batch, HEADS, DV, DK), F32),
                   jax.ShapeDtypeStruct((batch, HEADS, DV, DK), F32)],
        scratch_shapes=[pltpu.VMEM((HEADS, DV, DK), F32), pltpu.VMEM((HEADS, DV, DK), F32)],
        compiler_params=_params(2),
        name="scanfast",
    )(zp, dec, sg0t, sr0t, gnw, rnw)


def _scan_kernel(z_ref, cs_ref, sn_ref, sg0_ref, sr0_ref, wup_ref, ba_ref, gnw_ref, rnw_ref,
                 ua_ref, ub_ref, sg_out_ref, sr_out_ref, sg_scr, sr_scr, k_scr, b_scr, *, n_chunks):
    c = pl.program_id(1)
    L = CHUNK

    @pl.when(c == 0)
    def _():
        sg_scr[...] = sg0_ref[0]
        sr_scr[...] = sr0_ref[0]

    row = lax.broadcasted_iota(jnp.int32, (L, L), 0)
    col = lax.broadcasted_iota(jnp.int32, (L, L), 1)
    causal = row >= col
    tril = causal.astype(BF16)

    la = _log_sigmoid(_dot(z_ref[:, RA:RA + LANES], wup_ref[...]) + ba_ref[...]) * (1.0 / GLA_TAU)
    hi, mid, lo = _split3(la)
    b_all = _dot(tril, hi) + _dot(tril, mid) + _dot(tril, lo)

    for h in range(HEADS):
        bh = b_all[:, h * DK:(h + 1) * DK]
        q = z_ref[:, QA + h * DK:QA + (h + 1) * DK].astype(F32) * (DK ** -0.5)
        k = z_ref[:, KA + h * DK:KA + (h + 1) * DK].astype(F32)
        v = z_ref[:, VA + h * DV:VA + (h + 1) * DV]
        qt = (q * jnp.exp(bh)).astype(BF16)
        kT = k.T
        bT = bh.T
        blT = bT[:, L - 1:L]
        kdT = (kT * jnp.exp(blT - bT)).astype(BF16)
        k_scr[...] = k
        b_scr[...] = bh

        def score_column(j, acc, q=q, bh=bh):
            e = jnp.exp(jnp.minimum(bh - b_scr[pl.ds(j, 1), :], 0.0))
            colv = jnp.sum(q * k_scr[pl.ds(j, 1), :] * e, axis=1, keepdims=True)
            return jnp.where(col == j, colv, acc)

        s = lax.fori_loop(0, L, score_column, jnp.zeros((L, L), F32))
        s = jnp.where(causal, s, 0.0).astype(BF16)
        s_old = sg_scr[h]
        o = _dot(s, v) + _dot(qt, s_old.astype(BF16))
        sg_scr[h] = jnp.exp(blT) * s_old + _dot(kdT, v)
        g = z_ref[:, GA + h * DV:GA + (h + 1) * DV].astype(F32)
        ua_ref[:, h * DV:(h + 1) * DV] = (_rms(o, gnw_ref[...]) * (g * jax.nn.sigmoid(g))).astype(BF16)

    cs = cs_ref[...]
    sn = sn_ref[...]
    rel = (row - col).astype(F32)
    t_col = lax.broadcasted_iota(jnp.int32, (L, 1), 0).astype(F32)
    s_row = lax.broadcasted_iota(jnp.int32, (1, L), 1).astype(F32)
    for h in range(HEADS):
        lg = _ret_log_decay(h)
        q = z_ref[:, QB + h * DK:QB + (h + 1) * DK].astype(F32)
        k = z_ref[:, KB + h * DK:KB + (h + 1) * DK].astype(F32)
        v = z_ref[:, VB + h * DV:VB + (h + 1) * DV]
        qr = _rotate(q, cs, sn).astype(BF16)
        kr = _rotate(k, cs, sn) * (DK ** -0.5)
        krT = kr.T
        decay = jnp.exp(jnp.where(causal, lg * rel, -jnp.inf))
        s = (_dot(qr, krT.astype(BF16)) * decay).astype(BF16)
        s_old = sr_scr[h]
        inner = jnp.exp(lg * (t_col + 1.0))
        o = _dot(s, v) + inner * _dot(qr, s_old.astype(BF16))
        kdT = (krT * jnp.exp(lg * (L - 1.0 - s_row))).astype(BF16)
        sr_scr[h] = math.exp(lg * L) * s_old + _dot(kdT, v)
        g = z_ref[:, GB + h * DV:GB + (h + 1) * DV].astype(F32)
        ub_ref[:, h * DV:(h + 1) * DV] = (_rms(o, rnw_ref[...]) * (g * jax.nn.sigmoid(g))).astype(BF16)

    @pl.when(c == n_chunks - 1)
    def _():
        sg_out_ref[0] = sg_scr[...]
        sr_out_ref[0] = sr_scr[...]


def _scan(zs, cs, sn, sg0, sr0, wup, ba, gnw, rnw, batch):
    T = zs.shape[0]
    n_chunks = T // (batch * CHUNK)
    state_spec = pl.BlockSpec((1, HEADS, DK, DV), lambda b, c: (b, 0, 0, 0))
    full = lambda shape: pl.BlockSpec(shape, lambda b, c: (0,) * len(shape))
    return pl.pallas_call(
        functools.partial(_scan_kernel, n_chunks=n_chunks),
        grid=(batch, n_chunks),
        in_specs=[pl.BlockSpec((CHUNK, ZS_W), lambda b, c: (b * n_chunks + c, 0)),
                  pl.BlockSpec((CHUNK, DK), lambda b, c: (c, 0)),
                  pl.BlockSpec((CHUNK, DK), lambda b, c: (c, 0)),
                  state_spec, state_spec,
                  full((LANES, HEADS * DK)), full((1, HEADS * DK)), full((1, DV)), full((1, DV))],
        out_specs=[pl.BlockSpec((CHUNK, HEADS * DV), lambda b, c: (b * n_chunks + c, 0)),
                   pl.BlockSpec((CHUNK, HEADS * DV), lambda b, c: (b * n_chunks + c, 0)),
                   state_spec, state_spec],
        out_shape=[jax.ShapeDtypeStruct((T, HEADS * DV), BF16), jax.ShapeDtypeStruct((T, HEADS * DV), BF16),
                   jax.ShapeDtypeStruct((batch, HEADS, DK, DV), F32),
                   jax.ShapeDtypeStruct((batch, HEADS, DK, DV), F32)],
        scratch_shapes=[pltpu.VMEM((HEADS, DK, DV), F32), pltpu.VMEM((HEADS, DK, DV), F32),
                        pltpu.VMEM((CHUNK, DK), F32), pltpu.VMEM((CHUNK, DK), F32)],
        compiler_params=_params(2),
        name="scan",
    )(zs, cs, sn, sg0, sr0, wup, ba, gnw, rnw)


def _post_kernel(ua_ref, ub_ref, zm_ref, x_ref, wa_ref, wb_ref, wo_ref, nfw_ref, wr_ref, br_ref, cnt0_ref,
                 earlier_ref, x1_ref, e_ref, g_ref, rank_ref, cnt_ref):
    @pl.when(pl.program_id(0) == 0)
    def _():
        cnt_ref[...] = cnt0_ref[...]

    a = _dot(ua_ref[...], wa_ref[...])
    b = _dot(ub_ref[...], wb_ref[...])
    merged = zm_ref[:, :D_MODEL].astype(F32) * a + zm_ref[:, D_MODEL:].astype(F32) * b
    x1 = x_ref[...] + _dot(merged.astype(BF16), wo_ref[...])
    x1_ref[...] = x1

    h2 = _rms(x1, nfw_ref[...])
    h_hi, h_mid, _ = _split3(h2)
    w_hi, w_mid = wr_ref[0], wr_ref[1]
    logits = _dot(h_hi, w_hi) + _dot(h_hi, w_mid) + _dot(h_mid, w_hi) + br_ref[...]

    tm = logits.shape[0]
    work = logits.T[:N_EXPERTS, :]
    expert = lax.broadcasted_iota(jnp.int32, work.shape, 0).astype(F32)
    vals, idxs = [], []
    for _ in range(TOP_K):
        m = jnp.max(work, axis=0, keepdims=True)
        idx = jnp.min(jnp.where(work == m, expert, float(N_EXPERTS)), axis=0, keepdims=True)
        vals.append(m)
        idxs.append(idx)
        work = jnp.where(expert == idx, -jnp.inf, work)
    ps = [jnp.exp(v - vals[0]) for v in vals]
    den = ps[0] + ps[1] + ps[2] + ps[3]

    multi_hot = jnp.zeros(work.shape, F32)
    for k in range(TOP_K):
        multi_hot = multi_hot + (expert == idxs[k]).astype(F32)
    before = _dot(multi_hot.astype(BF16), earlier_ref[...]) + cnt_ref[:, 0:1]
    ranks = [jnp.sum(jnp.where(expert == idxs[k], before, 0.0), axis=0, keepdims=True) for k in range(TOP_K)]
    cnt_ref[...] += jnp.sum(multi_hot, axis=1, keepdims=True)

    slot = lax.broadcasted_iota(jnp.int32, (SUBLANES, tm), 0)
    e_out = jnp.zeros((SUBLANES, tm), F32)
    g_out = jnp.zeros((SUBLANES, tm), F32)
    r_out = jnp.zeros((SUBLANES, tm), F32)
    for k in range(TOP_K):
        e_out = jnp.where(slot == k, idxs[k], e_out)
        g_out = jnp.where(slot == k, ps[k] / den, g_out)
        r_out = jnp.where(slot == k, ranks[k], r_out)
    e_ref[...] = e_out.astype(jnp.int32)
    g_ref[...] = g_out
    rank_ref[...] = r_out.astype(jnp.int32)


def _post(ua, ub, zm, x, wa, wb, wo, nfw, wr, br, cnt0):
    T = x.shape[0]
    tm = min(ROW_TILE, T)
    assert T % tm == 0
    rows = lambda w: pl.BlockSpec((tm, w), lambda i: (i, 0))
    full = lambda shape: pl.BlockSpec(shape, lambda i: (0,) * len(shape))
    earlier = (jnp.arange(tm)[:, None] < jnp.arange(tm)[None, :]).astype(BF16)
    per_token = pl.BlockSpec((SUBLANES, tm), lambda i: (0, i))
    return pl.pallas_call(
        _post_kernel,
        grid=(T // tm,),
        in_specs=[rows(HEADS * DV), rows(HEADS * DV), rows(ZM_W), rows(D_MODEL),
                  full((HEADS * DV, D_MODEL)), full((HEADS * DV, D_MODEL)), full((D_MODEL, D_MODEL)),
                  full((1, D_MODEL)), full((2, D_MODEL, LANES)), full((1, LANES)), full((N_EXPERTS, LANES)),
                  full((tm, tm))],
        out_specs=[rows(D_MODEL), per_token, per_token, per_token, full((N_EXPERTS, LANES))],
        out_shape=[jax.ShapeDtypeStruct((T, D_MODEL), F32), jax.ShapeDtypeStruct((SUBLANES, T), jnp.int32),
                   jax.ShapeDtypeStruct((SUBLANES, T), F32), jax.ShapeDtypeStruct((SUBLANES, T), jnp.int32),
                   jax.ShapeDtypeStruct((N_EXPERTS, LANES), F32)],
        compiler_params=_params(1),
        name="post",
    )(ua, ub, zm, x, wa, wb, wo, nfw, wr, br, cnt0, earlier)


ROW_SUBLANES = D_MODEL // LANES


def _row_tile(r):
    return pl.ds(pl.multiple_of(r * ROW_SUBLANES, ROW_SUBLANES), ROW_SUBLANES)


def _load_row_tiles(ref, n_rows):
    return [ref[pl.ds(c, n_rows, stride=ROW_SUBLANES), :] for c in range(ROW_SUBLANES)]


def _store_row_tiles(ref, x):
    for c in range(ROW_SUBLANES):
        ref[pl.ds(c, x.shape[0], stride=ROW_SUBLANES), :] = x[:, c * LANES:(c + 1) * LANES]


def _tile_indices(dest, tm):
    n = dest.shape[1] // tm
    return dest.reshape(TOP_K, n, tm).transpose(1, 0, 2).reshape(n, 1, TOP_K * tm)


def _row_copy_wait(src_rows, dst_rows, sem):
    pltpu.make_async_copy(src_rows, dst_rows, sem).wait()


def _dispatch_kernel(dest_ref, xp_ref, xs_ref, nfw_ref, rows_ref, hbuf, sem, *, n_prompt_tiles, n_tiles):
    tm = hbuf.shape[1] // ROW_SUBLANES
    i = pl.program_id(0)
    slot = i % 2

    def wait_slot(s):
        for _ in range(TOP_K):
            _row_copy_wait(hbuf.at[s], rows_ref.at[pl.ds(0, tm * ROW_SUBLANES)], sem.at[s])

    @pl.when(i >= 2)
    def _():
        wait_slot(slot)

    @pl.when(i < n_prompt_tiles)
    def _():
        _store_row_tiles(hbuf.at[slot], _rms(xp_ref[...], nfw_ref[...]))

    @pl.when(i >= n_prompt_tiles)
    def _():
        _store_row_tiles(hbuf.at[slot], _rms(xs_ref[...], nfw_ref[...]))

    def issue(r, carry):
        src = hbuf.at[slot, _row_tile(r)]
        for k in range(TOP_K):
            d = dest_ref[0, 0, k * tm + r]
            pltpu.make_async_copy(src, rows_ref.at[_row_tile(d)], sem.at[slot]).start(priority=k % 2)
        return carry

    lax.fori_loop(0, tm, issue, 0, unroll=ISSUE_UNROLL)

    @pl.when(i == n_tiles - 1)
    def _():
        @pl.when(i >= 1)
        def _():
            wait_slot(1 - slot)
        wait_slot(slot)


def _dispatch(dest, x1_p, x1_s, nfw):
    Tp, Ts = x1_p.shape[0], x1_s.shape[0]
    tm = math.gcd(DISPATCH_TILE, Tp, Ts)
    assert tm % 8 == 0
    n_p, n_s = Tp // tm, Ts // tm
    return pl.pallas_call(
        functools.partial(_dispatch_kernel, n_prompt_tiles=n_p, n_tiles=n_p + n_s),
        grid=(n_p + n_s,),
        in_specs=[pl.BlockSpec((1, 1, tm * TOP_K), lambda i: (i, 0, 0), memory_space=pltpu.SMEM),
                  pl.BlockSpec((tm, D_MODEL), lambda i: (jnp.minimum(i, n_p - 1), 0)),
                  pl.BlockSpec((tm, D_MODEL), lambda i: (jnp.maximum(i - n_p, 0), 0)),
                  pl.BlockSpec((1, D_MODEL), lambda i: (0, 0))],
        out_specs=pl.BlockSpec(memory_space=pl.ANY),
        out_shape=jax.ShapeDtypeStruct(((Tp + Ts) * TOP_K * ROW_SUBLANES, LANES), F32),
        scratch_shapes=[pltpu.VMEM((2, tm * ROW_SUBLANES, LANES), F32), pltpu.SemaphoreType.DMA((2,))],
        compiler_params=_params(1),
        name="dispatch",
    )(_tile_indices(dest, tm), x1_p, x1_s, nfw)


def _expert_kernel(blk_ref, exp_ref, lo_ref, hi_ref, xs_ref, wgu_ref, bgu_ref, wd_ref, bd_ref, ys_ref,
                   saved, wgu_bf, wd_bf):
    del blk_ref
    i = pl.program_id(0)
    lo, hi = lo_ref[i], hi_ref[i]

    @pl.when((i == 0) | (exp_ref[i] != exp_ref[jnp.maximum(i - 1, 0)]))
    def _():
        wgu_bf[...] = wgu_ref[0].astype(BF16)
        wd_bf[...] = wd_ref[0].astype(BF16)

    @pl.when(hi > lo)
    def _():
        bm = xs_ref.shape[0] // ROW_SUBLANES

        @pl.when(lo > 0)
        def _():
            saved[...] = ys_ref[...]

        x = jnp.concatenate([t.astype(BF16) for t in _load_row_tiles(xs_ref, bm)], axis=1)
        hgu = _dot(x, wgu_bf[...]) + bgu_ref[0]
        glu = jnp.minimum(hgu[:, :D_FF], SWIGLU_LIMIT)
        lin = jnp.clip(hgu[:, D_FF:], -SWIGLU_LIMIT, SWIGLU_LIMIT)
        act = glu * jax.nn.sigmoid(SWIGLU_ALPHA * glu) * (lin + 1.0)
        _store_row_tiles(ys_ref, _dot(act.astype(BF16), wd_bf[...]) + bd_ref[0])

        @pl.when(lo > 0)
        def _():
            keep = lax.broadcasted_iota(jnp.int32, (bm, 1), 0) < lo
            for c in range(ROW_SUBLANES):
                rows = pl.ds(c, bm, stride=ROW_SUBLANES)
                ys_ref[rows, :] = jnp.where(keep, saved[rows, :], ys_ref[rows, :])


def _experts(items, xs, wgu, bgu, wd, bd):
    P = xs.shape[0] // ROW_SUBLANES
    bm = EXPERT_BLOCK
    n_items = items[0].shape[0]
    blk = lambda i, b, e, lo, hi: (b[i], 0)
    ex3 = lambda i, b, e, lo, hi: (e[i], 0, 0)
    return pl.pallas_call(
        _expert_kernel,
        grid_spec=pltpu.PrefetchScalarGridSpec(
            num_scalar_prefetch=4,
            grid=(n_items,),
            in_specs=[pl.BlockSpec((bm * ROW_SUBLANES, LANES), blk),
                      pl.BlockSpec((1, D_MODEL, 2 * D_FF), ex3),
                      pl.BlockSpec((1, 1, 2 * D_FF), ex3),
                      pl.BlockSpec((1, D_FF, D_MODEL), ex3),
                      pl.BlockSpec((1, 1, D_MODEL), ex3)],
            out_specs=pl.BlockSpec((bm * ROW_SUBLANES, LANES), blk),
            scratch_shapes=[pltpu.VMEM((bm * ROW_SUBLANES, LANES), F32),
                            pltpu.VMEM((D_MODEL, 2 * D_FF), BF16), pltpu.VMEM((D_FF, D_MODEL), BF16)]),
        out_shape=jax.ShapeDtypeStruct((P * ROW_SUBLANES, LANES), F32),
        compiler_params=_params(1),
        name="experts",
    )(*items, xs, wgu, bgu, wd, bd)


def _combine_kernel(dest_ref, dest_next_ref, gate_ref, x1_ref, nw_ref, ys_ref, out_ref, buf, sem, *, n_tiles):
    tm = x1_ref.shape[0]
    i = pl.program_id(0)
    slot = i % 2

    def issue_tile(idx_ref, s):
        def issue(r, carry):
            for k in range(TOP_K):
                d = idx_ref[0, 0, k * tm + r]
                pltpu.make_async_copy(ys_ref.at[_row_tile(d)], buf.at[s, k, _row_tile(r)],
                                      sem.at[s]).start(priority=k % 2)
            return carry

        lax.fori_loop(0, tm, issue, 0, unroll=ISSUE_UNROLL)

    @pl.when(i == 0)
    def _():
        issue_tile(dest_ref, slot)

    @pl.when(i + 1 < n_tiles)
    def _():
        issue_tile(dest_next_ref, 1 - slot)

    for k in range(TOP_K):
        _row_copy_wait(ys_ref.at[pl.ds(0, tm * ROW_SUBLANES)], buf.at[slot, k], sem.at[slot])
    gate = jnp.concatenate([gate_ref[...], jnp.zeros((tm - SUBLANES, tm), F32)], axis=0).T
    chunks = [_load_row_tiles(buf.at[slot, k], tm) for k in range(TOP_K)]
    f = []
    for c in range(ROW_SUBLANES):
        fc = gate[:, 0:1] * chunks[0][c]
        for k in range(1, TOP_K):
            fc = fc + gate[:, k:k + 1] * chunks[k][c]
        f.append(fc)
    out_ref[...] = _rms(x1_ref[...] + jnp.concatenate(f, axis=1), nw_ref[...])


def _combine(dest, gate, x1, nw, ys):
    T = x1.shape[0]
    tm = min(COMBINE_TILE, T)
    assert T % tm == 0
    n = T // tm
    dest3 = _tile_indices(dest, tm)
    return pl.pallas_call(
        functools.partial(_combine_kernel, n_tiles=n),
        grid=(n,),
        in_specs=[pl.BlockSpec((1, 1, tm * TOP_K), lambda i: (i, 0, 0), memory_space=pltpu.SMEM),
                  pl.BlockSpec((1, 1, tm * TOP_K), lambda i: (jnp.minimum(i + 1, n - 1), 0, 0),
                               memory_space=pltpu.SMEM),
                  pl.BlockSpec((SUBLANES, tm), lambda i: (0, i)),
                  pl.BlockSpec((tm, D_MODEL), lambda i: (i, 0)),
                  pl.BlockSpec((1, D_MODEL), lambda i: (0, 0)),
                  pl.BlockSpec(memory_space=pl.ANY)],
        out_specs=pl.BlockSpec((tm, D_MODEL), lambda i: (i, 0)),
        out_shape=jax.ShapeDtypeStruct((T, D_MODEL), F32),
        scratch_shapes=[pltpu.VMEM((2, TOP_K, tm * ROW_SUBLANES, LANES), F32), pltpu.SemaphoreType.DMA((2,))],
        compiler_params=_params(1),
        name="combine",
    )(dest3, dest3, gate, x1, nw, ys)


def _rotary_tables(pos):
    inv = ROPE_BASE ** (-jnp.arange(0, DK, 2, dtype=F32) / DK)
    ang = pos.astype(F32)[:, None] * inv[None, :]
    cos, sin = jnp.cos(ang), jnp.sin(ang)
    return jnp.concatenate([cos, cos], axis=1), jnp.concatenate([-sin, sin], axis=1)


def _routing(top_e, rank, counts):
    bm = EXPERT_BLOCK
    n_rows = top_e.size
    n_blocks = n_rows // bm
    n_items = n_blocks + N_EXPERTS - 1
    start = jnp.cumsum(counts) - counts
    end = start + counts
    expert_ids = jnp.arange(N_EXPERTS, dtype=jnp.int32)
    onehot = top_e[None, :, :] == expert_ids[:, None, None]
    dest = (jnp.sum(jnp.where(onehot, start[:, None, None], 0), axis=0) + rank).astype(jnp.int32)

    first_blk = start // bm
    n_it = jnp.where(counts > 0, (end - 1) // bm - first_blk + 1, 0)
    it_end = jnp.cumsum(n_it)
    it_start = it_end - n_it
    j = jnp.arange(n_items, dtype=jnp.int32)
    e_j = jnp.minimum(jnp.sum(j[:, None] >= it_end[None, :], axis=1), N_EXPERTS - 1).astype(jnp.int32)
    valid = j < it_end[-1]
    last_e = jnp.max(jnp.where(counts > 0, expert_ids, 0))
    e_j = jnp.where(valid, e_j, last_e)
    blk = jnp.where(valid, first_blk[e_j] + (j - it_start[e_j]), n_blocks - 1)
    lo = jnp.maximum(start[e_j], blk * bm) - blk * bm
    hi = jnp.minimum(end[e_j], (blk + 1) * bm) - blk * bm
    lo = jnp.where(valid, lo, 0)
    hi = jnp.where(valid, hi, 0)
    items = tuple(a.astype(jnp.int32) for a in (blk, e_j, lo, hi))
    return dest, items


def kernel(x_prompt, x_sample, state_gla, state_ret, meta_tokens, norm_mix_w, w_in, w_alpha_up, b_alpha, gla_norm_w, ret_norm_w, w_branch_gla, w_branch_ret, w_out, norm_ffn_w, w_router, b_router, w_gate_up, b_gate_up, w_down, b_down, norm_final_w):
    B, S, D = x_prompt.shape
    DB, DS, _ = x_sample.shape
    assert D == D_MODEL and S % CHUNK == 0 and DS == CHUNK and state_gla.shape[0] == 1
    l = 0

    offs = [0]
    for w in IN_WIDTHS:
        offs.append(offs[-1] + w)
    cols = [w_in[l][:, offs[i]:offs[i + 1]] for i in range(len(IN_WIDTHS))]
    qa, ka, va, ga, ra, qb, kb, vb, gb, m = cols
    ra = jnp.pad(ra, ((0, 0), (0, LANES - GLA_RANK)))
    ws = jnp.concatenate([qa, ka, va, ga, qb, kb, vb, gb, ra], axis=1).astype(BF16)
    wm = m.astype(BF16)
    wup = jnp.pad(w_alpha_up[l], ((0, LANES - GLA_RANK), (0, 0))).astype(BF16)
    ba = b_alpha[l][None, :]
    nmw = norm_mix_w[l][None, :]
    gnw = gla_norm_w[l][None, :]
    rnw = ret_norm_w[l][None, :]

    def mixer(x2d, pos, sg0, sr0, batch):
        T = x2d.shape[0]
        cs, sn = _rotary_tables(pos)
        reps = max(1, min(ROW_TILE, T) // pos.shape[0])
        zp, zm, dec, bmin = _prep(x2d, nmw, ws, wm, wup, ba, jnp.tile(cs, (reps, 1)), jnp.tile(sn, (reps, 1)))

        def ratio_form():
            ua, ub, sgt, srt = _scan_fast(zp, dec, jnp.swapaxes(sg0, 2, 3), jnp.swapaxes(sr0, 2, 3), gnw, rnw, batch)
            return ua, ub, jnp.swapaxes(sgt, 2, 3), jnp.swapaxes(srt, 2, 3)

        def exact_form():
            return tuple(_scan(_inproj(x2d, nmw, ws), cs, sn, sg0, sr0, wup, ba, gnw, rnw, batch))

        ua, ub, sg, sr = lax.cond(jnp.min(bmin) >= MIN_SAFE_LOG_FORGET, ratio_form, exact_form)
        return ua, ub, zm, sg, sr

    x_meta = jnp.concatenate([jnp.zeros((CHUNK - N_META, D), F32), meta_tokens.astype(F32)], axis=0)
    pos_meta = jnp.maximum(jnp.arange(CHUNK) - (CHUNK - N_META), 0)
    zero_state = jnp.zeros((1, HEADS, DK, DV), F32)
    _, _, _, sg_m, sr_m = mixer(x_meta, pos_meta, zero_state, zero_state, 1)

    xp = x_prompt.reshape(B * S, D)
    xs_ = x_sample.reshape(DB * DS, D)
    sg0 = jnp.broadcast_to(sg_m, (B, HEADS, DK, DV))
    sr0 = jnp.broadcast_to(sr_m, (B, HEADS, DK, DV))
    ua_p, ub_p, zm_p, sg_p, sr_p = mixer(xp, N_META + jnp.arange(S), sg0, sr0, B)
    ua_s, ub_s, zm_s, sg_s, sr_s = mixer(xs_, N_META + PAST_LEN + jnp.arange(DS),
                                         state_gla[l].astype(F32), state_ret[l].astype(F32), DB)

    wa = w_branch_gla[l].astype(BF16)
    wb = w_branch_ret[l].astype(BF16)
    wo = w_out[l].astype(BF16)
    nfw = norm_ffn_w[l][None, :]
    wr = jnp.pad(w_router[l].astype(F32), ((0, 0), (0, LANES - N_EXPERTS)))
    wr_hi = wr.astype(BF16)
    wr_mid = (wr - wr_hi.astype(F32)).astype(BF16)
    wr2 = jnp.stack([wr_hi, wr_mid])
    br = jnp.pad(b_router[l].astype(F32), (0, LANES - N_EXPERTS))[None, :]
    cnt0 = jnp.zeros((N_EXPERTS, LANES), F32)
    x1_p, e_p, g_p, r_p, cnt_p = _post(ua_p, ub_p, zm_p, xp, wa, wb, wo, nfw, wr2, br, cnt0)
    x1_s, e_s, g_s, r_s, cnt = _post(ua_s, ub_s, zm_s, xs_, wa, wb, wo, nfw, wr2, br, cnt_p)

    Tp, Ts = B * S, DB * DS
    assert ((Tp + Ts) * TOP_K) % EXPERT_BLOCK == 0
    top_e = jnp.concatenate([e_p[:TOP_K], e_s[:TOP_K]], axis=1)
    rank = jnp.concatenate([r_p[:TOP_K], r_s[:TOP_K]], axis=1)
    dest, items = _routing(top_e, rank, cnt[:, 0].astype(jnp.int32))
    dest_p, dest_s = dest[:, :Tp], dest[:, Tp:]

    xs_rows = _dispatch(dest, x1_p, x1_s, nfw)
    ys_rows = _experts(items, xs_rows, w_gate_up[l], b_gate_up[l][:, None, :],
                       w_down[l], b_down[l][:, None, :])
    nw_final = norm_final_w[None, :]
    y_p = _combine(dest_p, g_p, x1_p, nw_final, ys_rows)
    y_s = _combine(dest_s, g_s, x1_s, nw_final, ys_rows)

    dt = state_gla.dtype
    return (y_p.reshape(B, S, D), y_s.reshape(DB, DS, D),
            sg_p[None].astype(dt), sr_p[None].astype(state_ret.dtype),
            sg_s[None].astype(dt), sr_s[None].astype(state_ret.dtype))
```

```python
import functools
import math

import jax
import jax.numpy as jnp
from jax import lax
from jax.experimental import pallas as pl
from jax.experimental.pallas import tpu as pltpu

F32 = jnp.float32
BF16 = jnp.bfloat16

D_MODEL = 1024
CHUNK = 64
N_META = 16
PAST_LEN = 4096
EPS = 1e-5
HEADS = 4
DK = 128
DV = 256
GLA_RANK = 16
GLA_TAU = 16.0
ROPE_BASE = 10000.0
N_EXPERTS = 32
TOP_K = 4
D_FF = 1024
SWIGLU_ALPHA = 1.702
SWIGLU_LIMIT = 7.0
IN_WIDTHS = (HEADS * DK, HEADS * DK, HEADS * DV, HEADS * DV, GLA_RANK,
             HEADS * DK, HEADS * DK, HEADS * DV, HEADS * DV, 2 * D_MODEL)

LANES = 128
SUBLANES = 8


def _offsets(widths):
    offs = [0]
    for w in widths:
        offs.append(offs[-1] + w)
    return offs


_QK, _V = HEADS * DK, HEADS * DV
QA, KA, VA, GA, QB, KB, VB, GB, RA, ZS_W = _offsets([_QK, _QK, _V, _V, _QK, _QK, _V, _V, LANES])
ZM_W = 2 * D_MODEL
PQT, PKT, PKD, PVA, PGA, PQB, PKB, PKDB, PVB, PGB, PZ_W = _offsets(
    [_QK, _QK, _QK, _V, _V, _QK, _QK, _QK, _V, _V])
PROJ_GROUP = 512
MIN_SAFE_LOG_FORGET = -60.0

ROW_TILE = 512
DISPATCH_TILE = 256
COMBINE_TILE = 128
EXPERT_BLOCK = 512
ISSUE_UNROLL = 4
SCAN_CHUNKS_PER_STEP = 8
VMEM_LIMIT = 56 * 1024 * 1024


def _params(n_axes):
    return pltpu.CompilerParams(dimension_semantics=("arbitrary",) * n_axes, vmem_limit_bytes=VMEM_LIMIT)


def _rms(x, w):
    return x * lax.rsqrt(jnp.mean(x * x, axis=-1, keepdims=True) + EPS) * w


def _dot(a, b):
    return jnp.dot(a, b, preferred_element_type=F32)


def _split3(x):
    hi = x.astype(BF16)
    r1 = x - hi.astype(F32)
    mid = r1.astype(BF16)
    lo = (r1 - mid.astype(F32)).astype(BF16)
    return hi, mid, lo


def _log_sigmoid(x):
    return jnp.minimum(x, 0.0) - jnp.log1p(jnp.exp(-jnp.abs(x)))


def _rotate(t, cs, sn):
    return t * cs + pltpu.roll(t, DK // 2, axis=1) * sn


def _prep_kernel(x_ref, nw_ref, ws_ref, wm_ref, wup_ref, ba_ref, tril_ref, cs_ref, sn_ref,
                 zp_ref, zm_ref, dec_ref, bmin_ref):
    tm = x_ref.shape[0]
    n = tm // CHUNK
    h = _rms(x_ref[...], nw_ref[...]).astype(BF16)

    def proj(c0, w):
        return _dot(h, ws_ref[:, c0:c0 + w])

    ra = proj(RA, LANES).astype(BF16)
    la = _log_sigmoid(_dot(ra, wup_ref[...]) + ba_ref[...]) * (1.0 / GLA_TAU)
    parts = _split3(la)[:2]
    tril = tril_ref[...]
    b = jnp.concatenate(
        [sum(_dot(tril, p[j * CHUNK:(j + 1) * CHUNK]) for p in parts) for j in range(n)], axis=0)
    b3 = b.reshape(n, CHUNK, HEADS * DK)
    bl3 = b3[:, CHUNK - 1:CHUNK, :]
    dec_ref[...] = jnp.exp(bl3)
    bl = jnp.broadcast_to(bl3, b3.shape).reshape(tm, HEADS * DK)
    bmin = jnp.min(jnp.min(b, axis=0, keepdims=True), axis=1, keepdims=True)
    bmin_ref[0] = jnp.broadcast_to(bmin, (1, LANES))

    q = proj(QA, HEADS * DK) * (DK ** -0.5)
    zp_ref[:, PQT:PQT + HEADS * DK] = (q * jnp.exp(b)).astype(BF16)
    k = proj(KA, HEADS * DK)
    zp_ref[:, PKT:PKT + HEADS * DK] = (k * jnp.exp(-b)).astype(BF16)
    zp_ref[:, PKD:PKD + HEADS * DK] = (k * jnp.exp(bl - b)).astype(BF16)

    half = HEADS * DV // 2
    for j in range(2):
        zp_ref[:, PVA + j * half:PVA + (j + 1) * half] = proj(VA + j * half, half).astype(BF16)
        g = proj(GA + j * half, half)
        zp_ref[:, PGA + j * half:PGA + (j + 1) * half] = (g * jax.nn.sigmoid(g)).astype(BF16)
        zp_ref[:, PVB + j * half:PVB + (j + 1) * half] = proj(VB + j * half, half).astype(BF16)
        g = proj(GB + j * half, half)
        zp_ref[:, PGB + j * half:PGB + (j + 1) * half] = (g * jax.nn.sigmoid(g)).astype(BF16)

    cs = cs_ref[...]
    sn = sn_ref[...]
    qb = proj(QB, HEADS * DK)
    kb = proj(KB, HEADS * DK)
    steps_left = (CHUNK - 1 - (lax.broadcasted_iota(jnp.int32, (tm, 1), 0) % CHUNK)).astype(F32)
    for hd in range(HEADS):
        sl = slice(hd * DK, (hd + 1) * DK)
        zp_ref[:, PQB + hd * DK:PQB + (hd + 1) * DK] = _rotate(qb[:, sl], cs, sn).astype(BF16)
        kr = _rotate(kb[:, sl], cs, sn) * (DK ** -0.5)
        zp_ref[:, PKB + hd * DK:PKB + (hd + 1) * DK] = kr.astype(BF16)
        zp_ref[:, PKDB + hd * DK:PKDB + (hd + 1) * DK] = (kr * jnp.exp(_ret_log_decay(hd) * steps_left)).astype(BF16)

    for c0 in range(0, ZM_W, PROJ_GROUP):
        cols = slice(c0, c0 + PROJ_GROUP)
        zm_ref[:, cols] = jax.nn.sigmoid(_dot(h, wm_ref[:, cols])).astype(BF16)


def _prep(x, nw, ws, wm, wup, ba, cs, sn):
    T = x.shape[0]
    tm = min(ROW_TILE, T)
    assert T % tm == 0 and cs.shape[0] % tm == 0
    n_tab = cs.shape[0] // tm
    n = tm // CHUNK
    r = jnp.arange(CHUNK)
    tril = (r[:, None] >= r[None, :]).astype(BF16)
    const = dict(pipeline_mode=pl.Buffered(1))
    full = lambda shape, **kw: pl.BlockSpec(shape, lambda i: (0,) * len(shape), **kw)
    return pl.pallas_call(
        _prep_kernel,
        grid=(T // tm,),
        in_specs=[pl.BlockSpec((tm, D_MODEL), lambda i: (i, 0)),
                  full((1, D_MODEL)),
                  full((D_MODEL, ZS_W), **const), full((D_MODEL, ZM_W), **const),
                  full((LANES, HEADS * DK)), full((1, HEADS * DK)), full((CHUNK, CHUNK)),
                  pl.BlockSpec((tm, DK), lambda i: (i % n_tab, 0)),
                  pl.BlockSpec((tm, DK), lambda i: (i % n_tab, 0))],
        out_specs=[pl.BlockSpec((tm, PZ_W), lambda i: (i, 0)),
                   pl.BlockSpec((tm, ZM_W), lambda i: (i, 0)),
                   pl.BlockSpec((n, 1, HEADS * DK), lambda i: (i, 0, 0)),
                   pl.BlockSpec((1, 1, LANES), lambda i: (i, 0, 0))],
        out_shape=[jax.ShapeDtypeStruct((T, PZ_W), BF16), jax.ShapeDtypeStruct((T, ZM_W), BF16),
                   jax.ShapeDtypeStruct((T // CHUNK, 1, HEADS * DK), F32),
                   jax.ShapeDtypeStruct((T // tm, 1, LANES), F32)],
        compiler_params=_params(1),
        name="prep",
    )(x, nw, ws, wm, wup, ba, tril, cs, sn)


def _inproj_kernel(x_ref, nw_ref, ws_ref, zs_ref):
    h = _rms(x_ref[...], nw_ref[...]).astype(BF16)
    for c0 in range(0, ZS_W, PROJ_GROUP):
        c1 = min(c0 + PROJ_GROUP, ZS_W)
        zs_ref[:, c0:c1] = _dot(h, ws_ref[:, c0:c1]).astype(BF16)


def _inproj(x, nw, ws):
    T = x.shape[0]
    tm = min(ROW_TILE, T)
    assert T % tm == 0
    return pl.pallas_call(
        _inproj_kernel,
        grid=(T // tm,),
        in_specs=[pl.BlockSpec((tm, D_MODEL), lambda i: (i, 0)),
                  pl.BlockSpec((1, D_MODEL), lambda i: (0, 0)),
                  pl.BlockSpec((D_MODEL, ZS_W), lambda i: (0, 0), pipeline_mode=pl.Buffered(1))],
        out_specs=pl.BlockSpec((tm, ZS_W), lambda i: (i, 0)),
        out_shape=jax.ShapeDtypeStruct((T, ZS_W), BF16),
        compiler_params=_params(1),
        name="inproj",
    )(x, nw, ws)


def _ret_log_decay(head):
    return math.log(1.0 - 2.0 ** (-5.0 - head))


NT_DIMS = (((1,), (1,)), ((), ()))
TN_DIMS = (((0,), (0,)), ((), ()))


def _scan_fast_kernel(z_ref, dec_ref, sg0_ref, sr0_ref, gnw_ref, rnw_ref,
                      ua_ref, ub_ref, sg_out_ref, sr_out_ref, sg_scr, sr_scr, *, n_steps):
    step = pl.program_id(1)
    L = CHUNK
    chunks_per_step = dec_ref.shape[0]

    @pl.when(step == 0)
    def _():
        sg_scr[...] = sg0_ref[0]
        sr_scr[...] = sr0_ref[0]

    row = lax.broadcasted_iota(jnp.int32, (L, L), 0)
    col = lax.broadcasted_iota(jnp.int32, (L, L), 1)
    causal = row >= col
    rel = (row - col).astype(F32)
    t_col = lax.broadcasted_iota(jnp.int32, (L, 1), 0).astype(F32)

    def nt(a, b):
        return lax.dot_general(a, b, NT_DIMS, preferred_element_type=F32)

    def tn(a, b):
        return lax.dot_general(a, b, TN_DIMS, preferred_element_type=F32)

    units = [(j, gla, h) for j in range(chunks_per_step) for gla in (True, False) for h in range(HEADS)]

    def cols(gla, h):
        q, k, kd, v, g = (PQT, PKT, PKD, PVA, PGA) if gla else (PQB, PKB, PKDB, PVB, PGB)
        return (slice(q + h * DK, q + (h + 1) * DK), slice(k + h * DK, k + (h + 1) * DK),
                slice(kd + h * DK, kd + (h + 1) * DK), slice(v + h * DV, v + (h + 1) * DV),
                slice(g + h * DV, g + (h + 1) * DV))

    ret_decay = [jnp.exp(jnp.where(causal, _ret_log_decay(h) * rel, -jnp.inf)) for h in range(HEADS)]
    ret_inner = [jnp.exp(_ret_log_decay(h) * (t_col + 1.0)) for h in range(HEADS)]

    intra = {}
    for j, gla, h in units:
        rows = pl.ds(j * L, L)
        qc, kc, _, vc, _ = cols(gla, h)
        s = nt(z_ref[rows, qc], z_ref[rows, kc])
        s = jnp.where(causal, s, 0.0) if gla else s * ret_decay[h]
        intra[j, gla, h] = _dot(s.astype(BF16), z_ref[rows, vc])

    for j, gla, h in units:
        rows = pl.ds(j * L, L)
        qc, _, kdc, vc, gc = cols(gla, h)
        scr, out_ref, nw_ref = (sg_scr, ua_ref, gnw_ref) if gla else (sr_scr, ub_ref, rnw_ref)
        st_old = scr[h]
        inter = nt(z_ref[rows, qc], st_old.astype(BF16))
        update = tn(z_ref[rows, vc], z_ref[rows, kdc])
        if gla:
            scr[h] = dec_ref[j][:, h * DK:(h + 1) * DK] * st_old + update
            o = intra[j, gla, h] + inter
        else:
            scr[h] = math.exp(_ret_log_decay(h) * L) * st_old + update
            o = intra[j, gla, h] + ret_inner[h] * inter
        g = z_ref[rows, gc].astype(F32)
        out_ref[rows, h * DV:(h + 1) * DV] = (_rms(o, nw_ref[...]) * g).astype(BF16)

    @pl.when(step == n_steps - 1)
    def _():
        sg_out_ref[0] = sg_scr[...]
        sr_out_ref[0] = sr_scr[...]


def _scan_fast(zp, dec, sg0t, sr0t, gnw, rnw, batch):
    T = zp.shape[0]
    n_chunks = T // (batch * CHUNK)
    cps = math.gcd(SCAN_CHUNKS_PER_STEP, n_chunks)
    n_steps = n_chunks // cps
    state_spec = pl.BlockSpec((1, HEADS, DV, DK), lambda b, c: (b, 0, 0, 0))
    full = lambda shape: pl.BlockSpec(shape, lambda b, c: (0,) * len(shape))
    rows = lambda w: pl.BlockSpec((cps * CHUNK, w), lambda b, c: (b * n_steps + c, 0))
    return pl.pallas_call(
        functools.partial(_scan_fast_kernel, n_steps=n_steps),
        grid=(batch, n_steps),
        in_specs=[rows(PZ_W),
                  pl.BlockSpec((cps, 1, HEADS * DK), lambda b, c: (b * n_steps + c, 0, 0)),
                  state_spec, state_spec, full((1, DV)), full((1, DV))],
        out_specs=[rows(HEADS * DV), rows(HEADS * DV), state_spec, state_spec],
        out_shape=[jax.ShapeDtypeStruct((T, HEADS * DV), BF16), jax.ShapeDtypeStruct((T, HEADS * DV), BF16),
                   jax.ShapeDtypeStruct((batch, HEADS, DV, DK), F32),
                   jax.ShapeDtypeStruct((batch, HEADS, DV, DK), F32)],
        scratch_shapes=[pltpu.VMEM((HEADS, DV, DK), F32), pltpu.VMEM((HEADS, DV, DK), F32)],
        compiler_params=_params(2),
        name="scanfast",
    )(zp, dec, sg0t, sr0t, gnw, rnw)


def _scan_kernel(z_ref, cs_ref, sn_ref, sg0_ref, sr0_ref, wup_ref, ba_ref, gnw_ref, rnw_ref,
                 ua_ref, ub_ref, sg_out_ref, sr_out_ref, sg_scr, sr_scr, k_scr, b_scr, *, n_chunks):
    c = pl.program_id(1)
    L = CHUNK

    @pl.when(c == 0)
    def _():
        sg_scr[...] = sg0_ref[0]
        sr_scr[...] = sr0_ref[0]

    row = lax.broadcasted_iota(jnp.int32, (L, L), 0)
    col = lax.broadcasted_iota(jnp.int32, (L, L), 1)
    causal = row >= col
    tril = causal.astype(BF16)

    la = _log_sigmoid(_dot(z_ref[:, RA:RA + LANES], wup_ref[...]) + ba_ref[...]) * (1.0 / GLA_TAU)
    hi, mid, lo = _split3(la)
    b_all = _dot(tril, hi) + _dot(tril, mid) + _dot(tril, lo)

    for h in range(HEADS):
        bh = b_all[:, h * DK:(h + 1) * DK]
        q = z_ref[:, QA + h * DK:QA + (h + 1) * DK].astype(F32) * (DK ** -0.5)
        k = z_ref[:, KA + h * DK:KA + (h + 1) * DK].astype(F32)
        v = z_ref[:, VA + h * DV:VA + (h + 1) * DV]
        qt = (q * jnp.exp(bh)).astype(BF16)
        kT = k.T
        bT = bh.T
        blT = bT[:, L - 1:L]
        kdT = (kT * jnp.exp(blT - bT)).astype(BF16)
        k_scr[...] = k
        b_scr[...] = bh

        def score_column(j, acc, q=q, bh=bh):
            e = jnp.exp(jnp.minimum(bh - b_scr[pl.ds(j, 1), :], 0.0))
            colv = jnp.sum(q * k_scr[pl.ds(j, 1), :] * e, axis=1, keepdims=True)
            return jnp.where(col == j, colv, acc)

        s = lax.fori_loop(0, L, score_column, jnp.zeros((L, L), F32))
        s = jnp.where(causal, s, 0.0).astype(BF16)
        s_old = sg_scr[h]
        o = _dot(s, v) + _dot(qt, s_old.astype(BF16))
        sg_scr[h] = jnp.exp(blT) * s_old + _dot(kdT, v)
        g = z_ref[:, GA + h * DV:GA + (h + 1) * DV].astype(F32)
        ua_ref[:, h * DV:(h + 1) * DV] = (_rms(o, gnw_ref[...]) * (g * jax.nn.sigmoid(g))).astype(BF16)

    cs = cs_ref[...]
    sn = sn_ref[...]
    rel = (row - col).astype(F32)
    t_col = lax.broadcasted_iota(jnp.int32, (L, 1), 0).astype(F32)
    s_row = lax.broadcasted_iota(jnp.int32, (1, L), 1).astype(F32)
    for h in range(HEADS):
        lg = _ret_log_decay(h)
        q = z_ref[:, QB + h * DK:QB + (h + 1) * DK].astype(F32)
        k = z_ref[:, KB + h * DK:KB + (h + 1) * DK].astype(F32)
        v = z_ref[:, VB + h * DV:VB + (h + 1) * DV]
        qr = _rotate(q, cs, sn).astype(BF16)
        kr = _rotate(k, cs, sn) * (DK ** -0.5)
        krT = kr.T
        decay = jnp.exp(jnp.where(causal, lg * rel, -jnp.inf))
        s = (_dot(qr, krT.astype(BF16)) * decay).astype(BF16)
        s_old = sr_scr[h]
        inner = jnp.exp(lg * (t_col + 1.0))
        o = _dot(s, v) + inner * _dot(qr, s_old.astype(BF16))
        kdT = (krT * jnp.exp(lg * (L - 1.0 - s_row))).astype(BF16)
        sr_scr[h] = math.exp(lg * L) * s_old + _dot(kdT, v)
        g = z_ref[:, GB + h * DV:GB + (h + 1) * DV].astype(F32)
        ub_ref[:, h * DV:(h + 1) * DV] = (_rms(o, rnw_ref[...]) * (g * jax.nn.sigmoid(g))).astype(BF16)

    @pl.when(c == n_chunks - 1)
    def _():
        sg_out_ref[0] = sg_scr[...]
        sr_out_ref[0] = sr_scr[...]


def _scan(zs, cs, sn, sg0, sr0, wup, ba, gnw, rnw, batch):
    T = zs.shape[0]
    n_chunks = T // (batch * CHUNK)
    state_spec = pl.BlockSpec((1, HEADS, DK, DV), lambda b, c: (b, 0, 0, 0))
    full = lambda shape: pl.BlockSpec(shape, lambda b, c: (0,) * len(shape))
    return pl.pallas_call(
        functools.partial(_scan_kernel, n_chunks=n_chunks),
        grid=(batch, n_chunks),
        in_specs=[pl.BlockSpec((CHUNK, ZS_W), lambda b, c: (b * n_chunks + c, 0)),
                  pl.BlockSpec((CHUNK, DK), lambda b, c: (c, 0)),
                  pl.BlockSpec((CHUNK, DK), lambda b, c: (c, 0)),
                  state_spec, state_spec,
                  full((LANES, HEADS * DK)), full((1, HEADS * DK)), full((1, DV)), full((1, DV))],
        out_specs=[pl.BlockSpec((CHUNK, HEADS * DV), lambda b, c: (b * n_chunks + c, 0)),
                   pl.BlockSpec((CHUNK, HEADS * DV), lambda b, c: (b * n_chunks + c, 0)),
                   state_spec, state_spec],
        out_shape=[jax.ShapeDtypeStruct((T, HEADS * DV), BF16), jax.ShapeDtypeStruct((T, HEADS * DV), BF16),
                   jax.ShapeDtypeStruct((batch, HEADS, DK, DV), F32),
                   jax.ShapeDtypeStruct((batch, HEADS, DK, DV), F32)],
        scratch_shapes=[pltpu.VMEM((HEADS, DK, DV), F32), pltpu.VMEM((HEADS, DK, DV), F32),
                        pltpu.VMEM((CHUNK, DK), F32), pltpu.VMEM((CHUNK, DK), F32)],
        compiler_params=_params(2),
        name="scan",
    )(zs, cs, sn, sg0, sr0, wup, ba, gnw, rnw)


def _post_kernel(ua_ref, ub_ref, zm_ref, x_ref, wa_ref, wb_ref, wo_ref, nfw_ref, wr_ref, br_ref, cnt0_ref,
                 earlier_ref, x1_ref, e_ref, g_ref, rank_ref, cnt_ref):
    @pl.when(pl.program_id(0) == 0)
    def _():
        cnt_ref[...] = cnt0_ref[...]

    a = _dot(ua_ref[...], wa_ref[...])
    b = _dot(ub_ref[...], wb_ref[...])
    merged = zm_ref[:, :D_MODEL].astype(F32) * a + zm_ref[:, D_MODEL:].astype(F32) * b
    x1 = x_ref[...] + _dot(merged.astype(BF16), wo_ref[...])
    x1_ref[...] = x1

    h2 = _rms(x1, nfw_ref[...])
    h_hi, h_mid, _ = _split3(h2)
    w_hi, w_mid = wr_ref[0], wr_ref[1]
    logits = _dot(h_hi, w_hi) + _dot(h_hi, w_mid) + _dot(h_mid, w_hi) + br_ref[...]

    tm = logits.shape[0]
    work = logits.T[:N_EXPERTS, :]
    expert = lax.broadcasted_iota(jnp.int32, work.shape, 0).astype(F32)
    vals, idxs = [], []
    for _ in range(TOP_K):
        m = jnp.max(work, axis=0, keepdims=True)
        idx = jnp.min(jnp.where(work == m, expert, float(N_EXPERTS)), axis=0, keepdims=True)
        vals.append(m)
        idxs.append(idx)
        work = jnp.where(expert == idx, -jnp.inf, work)
    ps = [jnp.exp(v - vals[0]) for v in vals]
    den = ps[0] + ps[1] + ps[2] + ps[3]

    multi_hot = jnp.zeros(work.shape, F32)
    for k in range(TOP_K):
        multi_hot = multi_hot + (expert == idxs[k]).astype(F32)
    before = _dot(multi_hot.astype(BF16), earlier_ref[...]) + cnt_ref[:, 0:1]
    ranks = [jnp.sum(jnp.where(expert == idxs[k], before, 0.0), axis=0, keepdims=True) for k in range(TOP_K)]
    cnt_ref[...] += jnp.sum(multi_hot, axis=1, keepdims=True)

    slot = lax.broadcasted_iota(jnp.int32, (SUBLANES, tm), 0)
    e_out = jnp.zeros((SUBLANES, tm), F32)
    g_out = jnp.zeros((SUBLANES, tm), F32)
    r_out = jnp.zeros((SUBLANES, tm), F32)
    for k in range(TOP_K):
        e_out = jnp.where(slot == k, idxs[k], e_out)
        g_out = jnp.where(slot == k, ps[k] / den, g_out)
        r_out = jnp.where(slot == k, ranks[k], r_out)
    e_ref[...] = e_out.astype(jnp.int32)
    g_ref[...] = g_out
    rank_ref[...] = r_out.astype(jnp.int32)


def _post(ua, ub, zm, x, wa, wb, wo, nfw, wr, br, cnt0):
    T = x.shape[0]
    tm = min(ROW_TILE, T)
    assert T % tm == 0
    rows = lambda w: pl.BlockSpec((tm, w), lambda i: (i, 0))
    full = lambda shape: pl.BlockSpec(shape, lambda i: (0,) * len(shape))
    earlier = (jnp.arange(tm)[:, None] < jnp.arange(tm)[None, :]).astype(BF16)
    per_token = pl.BlockSpec((SUBLANES, tm), lambda i: (0, i))
    return pl.pallas_call(
        _post_kernel,
        grid=(T // tm,),
        in_specs=[rows(HEADS * DV), rows(HEADS * DV), rows(ZM_W), rows(D_MODEL),
                  full((HEADS * DV, D_MODEL)), full((HEADS * DV, D_MODEL)), full((D_MODEL, D_MODEL)),
                  full((1, D_MODEL)), full((2, D_MODEL, LANES)), full((1, LANES)), full((N_EXPERTS, LANES)),
                  full((tm, tm))],
        out_specs=[rows(D_MODEL), per_token, per_token, per_token, full((N_EXPERTS, LANES))],
        out_shape=[jax.ShapeDtypeStruct((T, D_MODEL), F32), jax.ShapeDtypeStruct((SUBLANES, T), jnp.int32),
                   jax.ShapeDtypeStruct((SUBLANES, T), F32), jax.ShapeDtypeStruct((SUBLANES, T), jnp.int32),
                   jax.ShapeDtypeStruct((N_EXPERTS, LANES), F32)],
        compiler_params=_params(1),
        name="post",
    )(ua, ub, zm, x, wa, wb, wo, nfw, wr, br, cnt0, earlier)


ROW_SUBLANES = D_MODEL // LANES


def _row_tile(r):
    return pl.ds(pl.multiple_of(r * ROW_SUBLANES, ROW_SUBLANES), ROW_SUBLANES)


def _load_row_tiles(ref, n_rows):
    return [ref[pl.ds(c, n_rows, stride=ROW_SUBLANES), :] for c in range(ROW_SUBLANES)]


def _store_row_tiles(ref, x):
    for c in range(ROW_SUBLANES):
        ref[pl.ds(c, x.shape[0], stride=ROW_SUBLANES), :] = x[:, c * LANES:(c + 1) * LANES]


def _tile_indices(dest, tm):
    n = dest.shape[1] // tm
    return dest.reshape(TOP_K, n, tm).transpose(1, 0, 2).reshape(n, 1, TOP_K * tm)


def _row_copy_wait(src_rows, dst_rows, sem):
    pltpu.make_async_copy(src_rows, dst_rows, sem).wait()


def _dispatch_kernel(dest_ref, xp_ref, xs_ref, nfw_ref, rows_ref, hbuf, sem, *, n_prompt_tiles, n_tiles):
    tm = hbuf.shape[1] // ROW_SUBLANES
    i = pl.program_id(0)
    slot = i % 2

    def wait_slot(s):
        for _ in range(TOP_K):
            _row_copy_wait(hbuf.at[s], rows_ref.at[pl.ds(0, tm * ROW_SUBLANES)], sem.at[s])

    @pl.when(i >= 2)
    def _():
        wait_slot(slot)

    @pl.when(i < n_prompt_tiles)
    def _():
        _store_row_tiles(hbuf.at[slot], _rms(xp_ref[...], nfw_ref[...]))

    @pl.when(i >= n_prompt_tiles)
    def _():
        _store_row_tiles(hbuf.at[slot], _rms(xs_ref[...], nfw_ref[...]))

    def issue(r, carry):
        src = hbuf.at[slot, _row_tile(r)]
        for k in range(TOP_K):
            d = dest_ref[0, 0, k * tm + r]
            pltpu.make_async_copy(src, rows_ref.at[_row_tile(d)], sem.at[slot]).start(priority=k % 2)
        return carry

    lax.fori_loop(0, tm, issue, 0, unroll=ISSUE_UNROLL)

    @pl.when(i == n_tiles - 1)
    def _():
        @pl.when(i >= 1)
        def _():
            wait_slot(1 - slot)
        wait_slot(slot)


def _dispatch(dest, x1_p, x1_s, nfw):
    Tp, Ts = x1_p.shape[0], x1_s.shape[0]
    tm = math.gcd(DISPATCH_TILE, Tp, Ts)
    assert tm % 8 == 0
    n_p, n_s = Tp // tm, Ts // tm
    return pl.pallas_call(
        functools.partial(_dispatch_kernel, n_prompt_tiles=n_p, n_tiles=n_p + n_s),
        grid=(n_p + n_s,),
        in_specs=[pl.BlockSpec((1, 1, tm * TOP_K), lambda i: (i, 0, 0), memory_space=pltpu.SMEM),
                  pl.BlockSpec((tm, D_MODEL), lambda i: (jnp.minimum(i, n_p - 1), 0)),
                  pl.BlockSpec((tm, D_MODEL), lambda i: (jnp.maximum(i - n_p, 0), 0)),
                  pl.BlockSpec((1, D_MODEL), lambda i: (0, 0))],
        out_specs=pl.BlockSpec(memory_space=pl.ANY),
        out_shape=jax.ShapeDtypeStruct(((Tp + Ts) * TOP_K * ROW_SUBLANES, LANES), F32),
        scratch_shapes=[pltpu.VMEM((2, tm * ROW_SUBLANES, LANES), F32), pltpu.SemaphoreType.DMA((2,))],
        compiler_params=_params(1),
        name="dispatch",
    )(_tile_indices(dest, tm), x1_p, x1_s, nfw)


def _expert_kernel(blk_ref, exp_ref, lo_ref, hi_ref, xs_ref, wgu_ref, bgu_ref, wd_ref, bd_ref, ys_ref,
                   saved, wgu_bf, wd_bf):
    del blk_ref
    i = pl.program_id(0)
    lo, hi = lo_ref[i], hi_ref[i]

    @pl.when((i == 0) | (exp_ref[i] != exp_ref[jnp.maximum(i - 1, 0)]))
    def _():
        wgu_bf[...] = wgu_ref[0].astype(BF16)
        wd_bf[...] = wd_ref[0].astype(BF16)

    @pl.when(hi > lo)
    def _():
        bm = xs_ref.shape[0] // ROW_SUBLANES

        @pl.when(lo > 0)
        def _():
            saved[...] = ys_ref[...]

        x = jnp.concatenate([t.astype(BF16) for t in _load_row_tiles(xs_ref, bm)], axis=1)
        hgu = _dot(x, wgu_bf[...]) + bgu_ref[0]
        glu = jnp.minimum(hgu[:, :D_FF], SWIGLU_LIMIT)
        lin = jnp.clip(hgu[:, D_FF:], -SWIGLU_LIMIT, SWIGLU_LIMIT)
        act = glu * jax.nn.sigmoid(SWIGLU_ALPHA * glu) * (lin + 1.0)
        _store_row_tiles(ys_ref, _dot(act.astype(BF16), wd_bf[...]) + bd_ref[0])

        @pl.when(lo > 0)
        def _():
            keep = lax.broadcasted_iota(jnp.int32, (bm, 1), 0) < lo
            for c in range(ROW_SUBLANES):
                rows = pl.ds(c, bm, stride=ROW_SUBLANES)
                ys_ref[rows, :] = jnp.where(keep, saved[rows, :], ys_ref[rows, :])


def _experts(items, xs, wgu, bgu, wd, bd):
    P = xs.shape[0] // ROW_SUBLANES
    bm = EXPERT_BLOCK
    n_items = items[0].shape[0]
    blk = lambda i, b, e, lo, hi: (b[i], 0)
    ex3 = lambda i, b, e, lo, hi: (e[i], 0, 0)
    return pl.pallas_call(
        _expert_kernel,
        grid_spec=pltpu.PrefetchScalarGridSpec(
            num_scalar_prefetch=4,
            grid=(n_items,),
            in_specs=[pl.BlockSpec((bm * ROW_SUBLANES, LANES), blk),
                      pl.BlockSpec((1, D_MODEL, 2 * D_FF), ex3),
                      pl.BlockSpec((1, 1, 2 * D_FF), ex3),
                      pl.BlockSpec((1, D_FF, D_MODEL), ex3),
                      pl.BlockSpec((1, 1, D_MODEL), ex3)],
            out_specs=pl.BlockSpec((bm * ROW_SUBLANES, LANES), blk),
            scratch_shapes=[pltpu.VMEM((bm * ROW_SUBLANES, LANES), F32),
                            pltpu.VMEM((D_MODEL, 2 * D_FF), BF16), pltpu.VMEM((D_FF, D_MODEL), BF16)]),
        out_shape=jax.ShapeDtypeStruct((P * ROW_SUBLANES, LANES), F32),
        compiler_params=_params(1),
        name="experts",
    )(*items, xs, wgu, bgu, wd, bd)


def _combine_kernel(dest_ref, dest_next_ref, gate_ref, x1_ref, nw_ref, ys_ref, out_ref, buf, sem, *, n_tiles):
    tm = x1_ref.shape[0]
    i = pl.program_id(0)
    slot = i % 2

    def issue_tile(idx_ref, s):
        def issue(r, carry):
            for k in range(TOP_K):
                d = idx_ref[0, 0, k * tm + r]
                pltpu.make_async_copy(ys_ref.at[_row_tile(d)], buf.at[s, k, _row_tile(r)],
                                      sem.at[s]).start(priority=k % 2)
            return carry

        lax.fori_loop(0, tm, issue, 0, unroll=ISSUE_UNROLL)

    @pl.when(i == 0)
    def _():
        issue_tile(dest_ref, slot)

    @pl.when(i + 1 < n_tiles)
    def _():
        issue_tile(dest_next_ref, 1 - slot)

    for k in range(TOP_K):
        _row_copy_wait(ys_ref.at[pl.ds(0, tm * ROW_SUBLANES)], buf.at[slot, k], sem.at[slot])
    gate = jnp.concatenate([gate_ref[...], jnp.zeros((tm - SUBLANES, tm), F32)], axis=0).T
    chunks = [_load_row_tiles(buf.at[slot, k], tm) for k in range(TOP_K)]
    f = []
    for c in range(ROW_SUBLANES):
        fc = gate[:, 0:1] * chunks[0][c]
        for k in range(1, TOP_K):
            fc = fc + gate[:, k:k + 1] * chunks[k][c]
        f.append(fc)
    out_ref[...] = _rms(x1_ref[...] + jnp.concatenate(f, axis=1), nw_ref[...])


def _combine(dest, gate, x1, nw, ys):
    T = x1.shape[0]
    tm = min(COMBINE_TILE, T)
    assert T % tm == 0
    n = T // tm
    dest3 = _tile_indices(dest, tm)
    return pl.pallas_call(
        functools.partial(_combine_kernel, n_tiles=n),
        grid=(n,),
        in_specs=[pl.BlockSpec((1, 1, tm * TOP_K), lambda i: (i, 0, 0), memory_space=pltpu.SMEM),
                  pl.BlockSpec((1, 1, tm * TOP_K), lambda i: (jnp.minimum(i + 1, n - 1), 0, 0),
                               memory_space=pltpu.SMEM),
                  pl.BlockSpec((SUBLANES, tm), lambda i: (0, i)),
                  pl.BlockSpec((tm, D_MODEL), lambda i: (i, 0)),
                  pl.BlockSpec((1, D_MODEL), lambda i: (0, 0)),
                  pl.BlockSpec(memory_space=pl.ANY)],
        out_specs=pl.BlockSpec((tm, D_MODEL), lambda i: (i, 0)),
        out_shape=jax.ShapeDtypeStruct((T, D_MODEL), F32),
        scratch_shapes=[pltpu.VMEM((2, TOP_K, tm * ROW_SUBLANES, LANES), F32), pltpu.SemaphoreType.DMA((2,))],
        compiler_params=_params(1),
        name="combine",
    )(dest3, dest3, gate, x1, nw, ys)


def _rotary_tables(pos):
    inv = ROPE_BASE ** (-jnp.arange(0, DK, 2, dtype=F32) / DK)
    ang = pos.astype(F32)[:, None] * inv[None, :]
    cos, sin = jnp.cos(ang), jnp.sin(ang)
    return jnp.concatenate([cos, cos], axis=1), jnp.concatenate([-sin, sin], axis=1)


def _routing(top_e, rank, counts):
    bm = EXPERT_BLOCK
    n_rows = top_e.size
    n_blocks = n_rows // bm
    n_items = n_blocks + N_EXPERTS - 1
    start = jnp.cumsum(counts) - counts
    end = start + counts
    expert_ids = jnp.arange(N_EXPERTS, dtype=jnp.int32)
    onehot = top_e[None, :, :] == expert_ids[:, None, None]
    dest = (jnp.sum(jnp.where(onehot, start[:, None, None], 0), axis=0) + rank).astype(jnp.int32)

    first_blk = start // bm
    n_it = jnp.where(counts > 0, (end - 1) // bm - first_blk + 1, 0)
    it_end = jnp.cumsum(n_it)
    it_start = it_end - n_it
    j = jnp.arange(n_items, dtype=jnp.int32)
    e_j = jnp.minimum(jnp.sum(j[:, None] >= it_end[None, :], axis=1), N_EXPERTS - 1).astype(jnp.int32)
    valid = j < it_end[-1]
    last_e = jnp.max(jnp.where(counts > 0, expert_ids, 0))
    e_j = jnp.where(valid, e_j, last_e)
    blk = jnp.where(valid, first_blk[e_j] + (j - it_start[e_j]), n_blocks - 1)
    lo = jnp.maximum(start[e_j], blk * bm) - blk * bm
    hi = jnp.minimum(end[e_j], (blk + 1) * bm) - blk * bm
    lo = jnp.where(valid, lo, 0)
    hi = jnp.where(valid, hi, 0)
    items = tuple(a.astype(jnp.int32) for a in (blk, e_j, lo, hi))
    return dest, items


def kernel(x_prompt, x_sample, state_gla, state_ret, meta_tokens, norm_mix_w, w_in, w_alpha_up, b_alpha, gla_norm_w, ret_norm_w, w_branch_gla, w_branch_ret, w_out, norm_ffn_w, w_router, b_router, w_gate_up, b_gate_up, w_down, b_down, norm_final_w):
    B, S, D = x_prompt.shape
    DB, DS, _ = x_sample.shape
    assert D == D_MODEL and S % CHUNK == 0 and DS == CHUNK and state_gla.shape[0] == 1
    l = 0

    offs = [0]
    for w in IN_WIDTHS:
        offs.append(offs[-1] + w)
    cols = [w_in[l][:, offs[i]:offs[i + 1]] for i in range(len(IN_WIDTHS))]
    qa, ka, va, ga, ra, qb, kb, vb, gb, m = cols
    ra = jnp.pad(ra, ((0, 0), (0, LANES - GLA_RANK)))
    ws = jnp.concatenate([qa, ka, va, ga, qb, kb, vb, gb, ra], axis=1).astype(BF16)
    wm = m.astype(BF16)
    wup = jnp.pad(w_alpha_up[l], ((0, LANES - GLA_RANK), (0, 0))).astype(BF16)
    ba = b_alpha[l][None, :]
    nmw = norm_mix_w[l][None, :]
    gnw = gla_norm_w[l][None, :]
    rnw = ret_norm_w[l][None, :]

    def mixer(x2d, pos, sg0, sr0, batch):
        T = x2d.shape[0]
        cs, sn = _rotary_tables(pos)
        reps = max(1, min(ROW_TILE, T) // pos.shape[0])
        zp, zm, dec, bmin = _prep(x2d, nmw, ws, wm, wup, ba, jnp.tile(cs, (reps, 1)), jnp.tile(sn, (reps, 1)))

        def ratio_form():
            ua, ub, sgt, srt = _scan_fast(zp, dec, jnp.swapaxes(sg0, 2, 3), jnp.swapaxes(sr0, 2, 3), gnw, rnw, batch)
            return ua, ub, jnp.swapaxes(sgt, 2, 3), jnp.swapaxes(srt, 2, 3)

        def exact_form():
            return tuple(_scan(_inproj(x2d, nmw, ws), cs, sn, sg0, sr0, wup, ba, gnw, rnw, batch))

        ua, ub, sg, sr = lax.cond(jnp.min(bmin) >= MIN_SAFE_LOG_FORGET, ratio_form, exact_form)
        return ua, ub, zm, sg, sr

    x_meta = jnp.concatenate([jnp.zeros((CHUNK - N_META, D), F32), meta_tokens.astype(F32)], axis=0)
    pos_meta = jnp.maximum(jnp.arange(CHUNK) - (CHUNK - N_META), 0)
    zero_state = jnp.zeros((1, HEADS, DK, DV), F32)
    _, _, _, sg_m, sr_m = mixer(x_meta, pos_meta, zero_state, zero_state, 1)

    xp = x_prompt.reshape(B * S, D)
    xs_ = x_sample.reshape(DB * DS, D)
    sg0 = jnp.broadcast_to(sg_m, (B, HEADS, DK, DV))
    sr0 = jnp.broadcast_to(sr_m, (B, HEADS, DK, DV))
    ua_p, ub_p, zm_p, sg_p, sr_p = mixer(xp, N_META + jnp.arange(S), sg0, sr0, B)
    ua_s, ub_s, zm_s, sg_s, sr_s = mixer(xs_, N_META + PAST_LEN + jnp.arange(DS),
                                         state_gla[l].astype(F32), state_ret[l].astype(F32), DB)

    wa = w_branch_gla[l].astype(BF16)
    wb = w_branch_ret[l].astype(BF16)
    wo = w_out[l].astype(BF16)
    nfw = norm_ffn_w[l][None, :]
    wr = jnp.pad(w_router[l].astype(F32), ((0, 0), (0, LANES - N_EXPERTS)))
    wr_hi = wr.astype(BF16)
    wr_mid = (wr - wr_hi.astype(F32)).astype(BF16)
    wr2 = jnp.stack([wr_hi, wr_mid])
    br = jnp.pad(b_router[l].astype(F32), (0, LANES - N_EXPERTS))[None, :]
    cnt0 = jnp.zeros((N_EXPERTS, LANES), F32)
    x1_p, e_p, g_p, r_p, cnt_p = _post(ua_p, ub_p, zm_p, xp, wa, wb, wo, nfw, wr2, br, cnt0)
    x1_s, e_s, g_s, r_s, cnt = _post(ua_s, ub_s, zm_s, xs_, wa, wb, wo, nfw, wr2, br, cnt_p)

    Tp, Ts = B * S, DB * DS
    assert ((Tp + Ts) * TOP_K) % EXPERT_BLOCK == 0
    top_e = jnp.concatenate([e_p[:TOP_K], e_s[:TOP_K]], axis=1)
    rank = jnp.concatenate([r_p[:TOP_K], r_s[:TOP_K]], axis=1)
    dest, items = _routing(top_e, rank, cnt[:, 0].astype(jnp.int32))
    dest_p, dest_s = dest[:, :Tp], dest[:, Tp:]

    xs_rows = _dispatch(dest, x1_p, x1_s, nfw)
    ys_rows = _experts(items, xs_rows, w_gate_up[l], b_gate_up[l][:, None, :],
                       w_down[l], b_down[l][:, None, :])
    nw_final = norm_final_w[None, :]
    y_p = _combine(dest_p, g_p, x1_p, nw_final, ys_rows)
    y_s = _combine(dest_s, g_s, x1_s, nw_final, ys_rows)

    dt = state_gla.dtype
    return (y_p.reshape(B, S, D), y_s.reshape(DB, DS, D),
            sg_p[None].astype(dt), sr_p[None].astype(state_ret.dtype),
            sg_s[None].astype(dt), sr_s[None].astype(state_ret.dtype))
```

```python
import functools
import math

import jax
import jax.numpy as jnp
from jax import lax
from jax.experimental import pallas as pl
from jax.experimental.pallas import tpu as pltpu

F32 = jnp.float32
BF16 = jnp.bfloat16

D_MODEL = 1024
CHUNK = 64
N_META = 16
PAST_LEN = 4096
EPS = 1e-5
HEADS = 4
DK = 128
DV = 256
GLA_RANK = 16
GLA_TAU = 16.0
ROPE_BASE = 10000.0
N_EXPERTS = 32
TOP_K = 4
D_FF = 1024
SWIGLU_ALPHA = 1.702
SWIGLU_LIMIT = 7.0
IN_WIDTHS = (HEADS * DK, HEADS * DK, HEADS * DV, HEADS * DV, GLA_RANK,
             HEADS * DK, HEADS * DK, HEADS * DV, HEADS * DV, 2 * D_MODEL)

LANES = 128
SUBLANES = 8


def _offsets(widths):
    offs = [0]
    for w in widths:
        offs.append(offs[-1] + w)
    return offs


_QK, _V = HEADS * DK, HEADS * DV
QA, KA, VA, GA, QB, KB, VB, GB, RA, ZS_W = _offsets([_QK, _QK, _V, _V, _QK, _QK, _V, _V, LANES])
ZM_W = 2 * D_MODEL
PQT, PKT, PKD, PVA, PGA, PQB, PKB, PKDB, PVB, PGB, PZ_W = _offsets(
    [_QK, _QK, _QK, _V, _V, _QK, _QK, _QK, _V, _V])
PROJ_GROUP = 512
MIN_SAFE_LOG_FORGET = -60.0

ROW_TILE = 512
DISPATCH_TILE = 256
COMBINE_TILE = 256
EXPERT_BLOCK = 512
ISSUE_UNROLL = 8
SCAN_CHUNKS_PER_STEP = 8
VMEM_LIMIT = 56 * 1024 * 1024


def _params(n_axes):
    return pltpu.CompilerParams(dimension_semantics=("arbitrary",) * n_axes, vmem_limit_bytes=VMEM_LIMIT)


def _rms(x, w):
    return x * lax.rsqrt(jnp.mean(x * x, axis=-1, keepdims=True) + EPS) * w


def _dot(a, b):
    return jnp.dot(a, b, preferred_element_type=F32)


def _split3(x):
    hi = x.astype(BF16)
    r1 = x - hi.astype(F32)
    mid = r1.astype(BF16)
    lo = (r1 - mid.astype(F32)).astype(BF16)
    return hi, mid, lo


def _log_sigmoid(x):
    return jnp.minimum(x, 0.0) - jnp.log1p(jnp.exp(-jnp.abs(x)))


def _rotate(t, cs, sn):
    return t * cs + pltpu.roll(t, DK // 2, axis=1) * sn


def _prep_kernel(x_ref, nw_ref, ws_ref, wm_ref, wup_ref, ba_ref, tril_ref, cs_ref, sn_ref,
                 zp_ref, zm_ref, dec_ref, bmin_ref):
    tm = x_ref.shape[0]
    n = tm // CHUNK
    h = _rms(x_ref[...], nw_ref[...]).astype(BF16)

    def proj(c0, w):
        return _dot(h, ws_ref[:, c0:c0 + w])

    ra = proj(RA, LANES).astype(BF16)
    la = _log_sigmoid(_dot(ra, wup_ref[...]) + ba_ref[...]) * (1.0 / GLA_TAU)
    parts = _split3(la)[:2]
    tril = tril_ref[...]
    b = jnp.concatenate(
        [sum(_dot(tril, p[j * CHUNK:(j + 1) * CHUNK]) for p in parts) for j in range(n)], axis=0)
    b3 = b.reshape(n, CHUNK, HEADS * DK)
    bl3 = b3[:, CHUNK - 1:CHUNK, :]
    dec_ref[...] = jnp.exp(bl3)
    bl = jnp.broadcast_to(bl3, b3.shape).reshape(tm, HEADS * DK)
    bmin = jnp.min(jnp.min(b, axis=0, keepdims=True), axis=1, keepdims=True)
    bmin_ref[0] = jnp.broadcast_to(bmin, (1, LANES))

    q = proj(QA, HEADS * DK) * (DK ** -0.5)
    zp_ref[:, PQT:PQT + HEADS * DK] = (q * jnp.exp(b)).astype(BF16)
    k = proj(KA, HEADS * DK)
    zp_ref[:, PKT:PKT + HEADS * DK] = (k * jnp.exp(-b)).astype(BF16)
    zp_ref[:, PKD:PKD + HEADS * DK] = (k * jnp.exp(bl - b)).astype(BF16)

    half = HEADS * DV // 2
    for j in range(2):
        zp_ref[:, PVA + j * half:PVA + (j + 1) * half] = proj(VA + j * half, half).astype(BF16)
        g = proj(GA + j * half, half)
        zp_ref[:, PGA + j * half:PGA + (j + 1) * half] = (g * jax.nn.sigmoid(g)).astype(BF16)
        zp_ref[:, PVB + j * half:PVB + (j + 1) * half] = proj(VB + j * half, half).astype(BF16)
        g = proj(GB + j * half, half)
        zp_ref[:, PGB + j * half:PGB + (j + 1) * half] = (g * jax.nn.sigmoid(g)).astype(BF16)

    cs = cs_ref[...]
    sn = sn_ref[...]
    qb = proj(QB, HEADS * DK)
    kb = proj(KB, HEADS * DK)
    steps_left = (CHUNK - 1 - (lax.broadcasted_iota(jnp.int32, (tm, 1), 0) % CHUNK)).astype(F32)
    for hd in range(HEADS):
        sl = slice(hd * DK, (hd + 1) * DK)
        zp_ref[:, PQB + hd * DK:PQB + (hd + 1) * DK] = _rotate(qb[:, sl], cs, sn).astype(BF16)
        kr = _rotate(kb[:, sl], cs, sn) * (DK ** -0.5)
        zp_ref[:, PKB + hd * DK:PKB + (hd + 1) * DK] = kr.astype(BF16)
        zp_ref[:, PKDB + hd * DK:PKDB + (hd + 1) * DK] = (kr * jnp.exp(_ret_log_decay(hd) * steps_left)).astype(BF16)

    for c0 in range(0, ZM_W, PROJ_GROUP):
        cols = slice(c0, c0 + PROJ_GROUP)
        zm_ref[:, cols] = jax.nn.sigmoid(_dot(h, wm_ref[:, cols])).astype(BF16)


def _prep(x, nw, ws, wm, wup, ba, cs, sn):
    T = x.shape[0]
    tm = min(ROW_TILE, T)
    assert T % tm == 0 and cs.shape[0] % tm == 0
    n_tab = cs.shape[0] // tm
    n = tm // CHUNK
    r = jnp.arange(CHUNK)
    tril = (r[:, None] >= r[None, :]).astype(BF16)
    const = dict(pipeline_mode=pl.Buffered(1))
    full = lambda shape, **kw: pl.BlockSpec(shape, lambda i: (0,) * len(shape), **kw)
    return pl.pallas_call(
        _prep_kernel,
        grid=(T // tm,),
        in_specs=[pl.BlockSpec((tm, D_MODEL), lambda i: (i, 0)),
                  full((1, D_MODEL)),
                  full((D_MODEL, ZS_W), **const), full((D_MODEL, ZM_W), **const),
                  full((LANES, HEADS * DK)), full((1, HEADS * DK)), full((CHUNK, CHUNK)),
                  pl.BlockSpec((tm, DK), lambda i: (i % n_tab, 0)),
                  pl.BlockSpec((tm, DK), lambda i: (i % n_tab, 0))],
        out_specs=[pl.BlockSpec((tm, PZ_W), lambda i: (i, 0)),
                   pl.BlockSpec((tm, ZM_W), lambda i: (i, 0)),
                   pl.BlockSpec((n, 1, HEADS * DK), lambda i: (i, 0, 0)),
                   pl.BlockSpec((1, 1, LANES), lambda i: (i, 0, 0))],
        out_shape=[jax.ShapeDtypeStruct((T, PZ_W), BF16), jax.ShapeDtypeStruct((T, ZM_W), BF16),
                   jax.ShapeDtypeStruct((T // CHUNK, 1, HEADS * DK), F32),
                   jax.ShapeDtypeStruct((T // tm, 1, LANES), F32)],
        compiler_params=_params(1),
        name="prep",
    )(x, nw, ws, wm, wup, ba, tril, cs, sn)


def _inproj_kernel(x_ref, nw_ref, ws_ref, zs_ref):
    h = _rms(x_ref[...], nw_ref[...]).astype(BF16)
    for c0 in range(0, ZS_W, PROJ_GROUP):
        c1 = min(c0 + PROJ_GROUP, ZS_W)
        zs_ref[:, c0:c1] = _dot(h, ws_ref[:, c0:c1]).astype(BF16)


def _inproj(x, nw, ws):
    T = x.shape[0]
    tm = min(ROW_TILE, T)
    assert T % tm == 0
    return pl.pallas_call(
        _inproj_kernel,
        grid=(T // tm,),
        in_specs=[pl.BlockSpec((tm, D_MODEL), lambda i: (i, 0)),
                  pl.BlockSpec((1, D_MODEL), lambda i: (0, 0)),
                  pl.BlockSpec((D_MODEL, ZS_W), lambda i: (0, 0), pipeline_mode=pl.Buffered(1))],
        out_specs=pl.BlockSpec((tm, ZS_W), lambda i: (i, 0)),
        out_shape=jax.ShapeDtypeStruct((T, ZS_W), BF16),
        compiler_params=_params(1),
        name="inproj",
    )(x, nw, ws)


def _ret_log_decay(head):
    return math.log(1.0 - 2.0 ** (-5.0 - head))


NT_DIMS = (((1,), (1,)), ((), ()))
TN_DIMS = (((0,), (0,)), ((), ()))


def _scan_fast_kernel(z_ref, dec_ref, sg0_ref, sr0_ref, gnw_ref, rnw_ref,
                      ua_ref, ub_ref, sg_out_ref, sr_out_ref, sg_scr, sr_scr, *, n_steps):
    step = pl.program_id(1)
    L = CHUNK
    chunks_per_step = dec_ref.shape[0]

    @pl.when(step == 0)
    def _():
        sg_scr[...] = sg0_ref[0]
        sr_scr[...] = sr0_ref[0]

    row = lax.broadcasted_iota(jnp.int32, (L, L), 0)
    col = lax.broadcasted_iota(jnp.int32, (L, L), 1)
    causal = row >= col
    rel = (row - col).astype(F32)
    t_col = lax.broadcasted_iota(jnp.int32, (L, 1), 0).astype(F32)

    def nt(a, b):
        return lax.dot_general(a, b, NT_DIMS, preferred_element_type=F32)

    def tn(a, b):
        return lax.dot_general(a, b, TN_DIMS, preferred_element_type=F32)

    units = [(j, gla, h) for j in range(chunks_per_step) for gla in (True, False) for h in range(HEADS)]

    def cols(gla, h):
        q, k, kd, v, g = (PQT, PKT, PKD, PVA, PGA) if gla else (PQB, PKB, PKDB, PVB, PGB)
        return (slice(q + h * DK, q + (h + 1) * DK), slice(k + h * DK, k + (h + 1) * DK),
                slice(kd + h * DK, kd + (h + 1) * DK), slice(v + h * DV, v + (h + 1) * DV),
                slice(g + h * DV, g + (h + 1) * DV))

    ret_decay = [jnp.exp(jnp.where(causal, _ret_log_decay(h) * rel, -jnp.inf)) for h in range(HEADS)]
    ret_inner = [jnp.exp(_ret_log_decay(h) * (t_col + 1.0)) for h in range(HEADS)]

    intra = {}
    for j, gla, h in units:
        rows = pl.ds(j * L, L)
        qc, kc, _, vc, _ = cols(gla, h)
        s = nt(z_ref[rows, qc], z_ref[rows, kc])
        s = jnp.where(causal, s, 0.0) if gla else s * ret_decay[h]
        intra[j, gla, h] = _dot(s.astype(BF16), z_ref[rows, vc])

    for j, gla, h in units:
        rows = pl.ds(j * L, L)
        qc, _, kdc, vc, gc = cols(gla, h)
        scr, out_ref, nw_ref = (sg_scr, ua_ref, gnw_ref) if gla else (sr_scr, ub_ref, rnw_ref)
        st_old = scr[h]
        inter = nt(z_ref[rows, qc], st_old.astype(BF16))
        update = tn(z_ref[rows, vc], z_ref[rows, kdc])
        if gla:
            scr[h] = dec_ref[j][:, h * DK:(h + 1) * DK] * st_old + update
            o = intra[j, gla, h] + inter
        else:
            scr[h] = math.exp(_ret_log_decay(h) * L) * st_old + update
            o = intra[j, gla, h] + ret_inner[h] * inter
        g = z_ref[rows, gc].astype(F32)
        out_ref[rows, h * DV:(h + 1) * DV] = (_rms(o, nw_ref[...]) * g).astype(BF16)

    @pl.when(step == n_steps - 1)
    def _():
        sg_out_ref[0] = sg_scr[...]
        sr_out_ref[0] = sr_scr[...]


def _scan_fast(zp, dec, sg0t, sr0t, gnw, rnw, batch):
    T = zp.shape[0]
    n_chunks = T // (batch * CHUNK)
    cps = math.gcd(SCAN_CHUNKS_PER_STEP, n_chunks)
    n_steps = n_chunks // cps
    state_spec = pl.BlockSpec((1, HEADS, DV, DK), lambda b, c: (b, 0, 0, 0))
    full = lambda shape: pl.BlockSpec(shape, lambda b, c: (0,) * len(shape))
    rows = lambda w: pl.BlockSpec((cps * CHUNK, w), lambda b, c: (b * n_steps + c, 0))
    return pl.pallas_call(
        functools.partial(_scan_fast_kernel, n_steps=n_steps),
        grid=(batch, n_steps),
        in_specs=[rows(PZ_W),
                  pl.BlockSpec((cps, 1, HEADS * DK), lambda b, c: (b * n_steps + c, 0, 0)),
                  state_spec, state_spec, full((1, DV)), full((1, DV))],
        out_specs=[rows(HEADS * DV), rows(HEADS * DV), state_spec, state_spec],
        out_shape=[jax.ShapeDtypeStruct((T, HEADS * DV), BF16), jax.ShapeDtypeStruct((T, HEADS * DV), BF16),
                   jax.ShapeDtypeStruct((batch, HEADS, DV, DK), F32),
                   jax.ShapeDtypeStruct((batch, HEADS, DV, DK), F32)],
        scratch_shapes=[pltpu.VMEM((HEADS, DV, DK), F32), pltpu.VMEM((HEADS, DV, DK), F32)],
        compiler_params=_params(2),
        name="scanfast",
    )(zp, dec, sg0t, sr0t, gnw, rnw)


def _scan_kernel(z_ref, cs_ref, sn_ref, sg0_ref, sr0_ref, wup_ref, ba_ref, gnw_ref, rnw_ref,
                 ua_ref, ub_ref, sg_out_ref, sr_out_ref, sg_scr, sr_scr, k_scr, b_scr, *, n_chunks):
    c = pl.program_id(1)
    L = CHUNK

    @pl.when(c == 0)
    def _():
        sg_scr[...] = sg0_ref[0]
        sr_scr[...] = sr0_ref[0]

    row = lax.broadcasted_iota(jnp.int32, (L, L), 0)
    col = lax.broadcasted_iota(jnp.int32, (L, L), 1)
    causal = row >= col
    tril = causal.astype(BF16)

    la = _log_sigmoid(_dot(z_ref[:, RA:RA + LANES], wup_ref[...]) + ba_ref[...]) * (1.0 / GLA_TAU)
    hi, mid, lo = _split3(la)
    b_all = _dot(tril, hi) + _dot(tril, mid) + _dot(tril, lo)

    for h in range(HEADS):
        bh = b_all[:, h * DK:(h + 1) * DK]
        q = z_ref[:, QA + h * DK:QA + (h + 1) * DK].astype(F32) * (DK ** -0.5)
        k = z_ref[:, KA + h * DK:KA + (h + 1) * DK].astype(F32)
        v = z_ref[:, VA + h * DV:VA + (h + 1) * DV]
        qt = (q * jnp.exp(bh)).astype(BF16)
        kT = k.T
        bT = bh.T
        blT = bT[:, L - 1:L]
        kdT = (kT * jnp.exp(blT - bT)).astype(BF16)
        k_scr[...] = k
        b_scr[...] = bh

        def score_column(j, acc, q=q, bh=bh):
            e = jnp.exp(jnp.minimum(bh - b_scr[pl.ds(j, 1), :], 0.0))
            colv = jnp.sum(q * k_scr[pl.ds(j, 1), :] * e, axis=1, keepdims=True)
            return jnp.where(col == j, colv, acc)

        s = lax.fori_loop(0, L, score_column, jnp.zeros((L, L), F32))
        s = jnp.where(causal, s, 0.0).astype(BF16)
        s_old = sg_scr[h]
        o = _dot(s, v) + _dot(qt, s_old.astype(BF16))
        sg_scr[h] = jnp.exp(blT) * s_old + _dot(kdT, v)
        g = z_ref[:, GA + h * DV:GA + (h + 1) * DV].astype(F32)
        ua_ref[:, h * DV:(h + 1) * DV] = (_rms(o, gnw_ref[...]) * (g * jax.nn.sigmoid(g))).astype(BF16)

    cs = cs_ref[...]
    sn = sn_ref[...]
    rel = (row - col).astype(F32)
    t_col = lax.broadcasted_iota(jnp.int32, (L, 1), 0).astype(F32)
    s_row = lax.broadcasted_iota(jnp.int32, (1, L), 1).astype(F32)
    for h in range(HEADS):
        lg = _ret_log_decay(h)
        q = z_ref[:, QB + h * DK:QB + (h + 1) * DK].astype(F32)
        k = z_ref[:, KB + h * DK:KB + (h + 1) * DK].astype(F32)
        v = z_ref[:, VB + h * DV:VB + (h + 1) * DV]
        qr = _rotate(q, cs, sn).astype(BF16)
        kr = _rotate(k, cs, sn) * (DK ** -0.5)
        krT = kr.T
        decay = jnp.exp(jnp.where(causal, lg * rel, -jnp.inf))
        s = (_dot(qr, krT.astype(BF16)) * decay).astype(BF16)
        s_old = sr_scr[h]
        inner = jnp.exp(lg * (t_col + 1.0))
        o = _dot(s, v) + inner * _dot(qr, s_old.astype(BF16))
        kdT = (krT * jnp.exp(lg * (L - 1.0 - s_row))).astype(BF16)
        sr_scr[h] = math.exp(lg * L) * s_old + _dot(kdT, v)
        g = z_ref[:, GB + h * DV:GB + (h + 1) * DV].astype(F32)
        ub_ref[:, h * DV:(h + 1) * DV] = (_rms(o, rnw_ref[...]) * (g * jax.nn.sigmoid(g))).astype(BF16)

    @pl.when(c == n_chunks - 1)
    def _():
        sg_out_ref[0] = sg_scr[...]
        sr_out_ref[0] = sr_scr[...]


def _scan(zs, cs, sn, sg0, sr0, wup, ba, gnw, rnw, batch):
    T = zs.shape[0]
    n_chunks = T // (batch * CHUNK)
    state_spec = pl.BlockSpec((1, HEADS, DK, DV), lambda b, c: (b, 0, 0, 0))
    full = lambda shape: pl.BlockSpec(shape, lambda b, c: (0,) * len(shape))
    return pl.pallas_call(
        functools.partial(_scan_kernel, n_chunks=n_chunks),
        grid=(batch, n_chunks),
        in_specs=[pl.BlockSpec((CHUNK, ZS_W), lambda b, c: (b * n_chunks + c, 0)),
                  pl.BlockSpec((CHUNK, DK), lambda b, c: (c, 0)),
                  pl.BlockSpec((CHUNK, DK), lambda b, c: (c, 0)),
                  state_spec, state_spec,
                  full((LANES, HEADS * DK)), full((1, HEADS * DK)), full((1, DV)), full((1, DV))],
        out_specs=[pl.BlockSpec((CHUNK, HEADS * DV), lambda b, c: (b * n_chunks + c, 0)),
                   pl.BlockSpec((CHUNK, HEADS * DV), lambda b, c: (b * n_chunks + c, 0)),
                   state_spec, state_spec],
        out_shape=[jax.ShapeDtypeStruct((T, HEADS * DV), BF16), jax.ShapeDtypeStruct((T, HEADS * DV), BF16),
                   jax.ShapeDtypeStruct((batch, HEADS, DK, DV), F32),
                   jax.ShapeDtypeStruct((batch, HEADS, DK, DV), F32)],
        scratch_shapes=[pltpu.VMEM((HEADS, DK, DV), F32), pltpu.VMEM((HEADS, DK, DV), F32),
                        pltpu.VMEM((CHUNK, DK), F32), pltpu.VMEM((CHUNK, DK), F32)],
        compiler_params=_params(2),
        name="scan",
    )(zs, cs, sn, sg0, sr0, wup, ba, gnw, rnw)


def _post_kernel(ua_ref, ub_ref, zm_ref, x_ref, wa_ref, wb_ref, wo_ref, nfw_ref, wr_ref, br_ref, cnt0_ref,
                 earlier_ref, x1_ref, e_ref, g_ref, rank_ref, cnt_ref):
    @pl.when(pl.program_id(0) == 0)
    def _():
        cnt_ref[...] = cnt0_ref[...]

    a = _dot(ua_ref[...], wa_ref[...])
    b = _dot(ub_ref[...], wb_ref[...])
    merged = zm_ref[:, :D_MODEL].astype(F32) * a + zm_ref[:, D_MODEL:].astype(F32) * b
    x1 = x_ref[...] + _dot(merged.astype(BF16), wo_ref[...])
    x1_ref[...] = x1

    h2 = _rms(x1, nfw_ref[...])
    h_hi, h_mid, _ = _split3(h2)
    w_hi, w_mid = wr_ref[0], wr_ref[1]
    logits = _dot(h_hi, w_hi) + _dot(h_hi, w_mid) + _dot(h_mid, w_hi) + br_ref[...]

    tm = logits.shape[0]
    work = logits.T[:N_EXPERTS, :]
    expert = lax.broadcasted_iota(jnp.int32, work.shape, 0).astype(F32)
    vals, idxs = [], []
    for _ in range(TOP_K):
        m = jnp.max(work, axis=0, keepdims=True)
        idx = jnp.min(jnp.where(work == m, expert, float(N_EXPERTS)), axis=0, keepdims=True)
        vals.append(m)
        idxs.append(idx)
        work = jnp.where(expert == idx, -jnp.inf, work)
    ps = [jnp.exp(v - vals[0]) for v in vals]
    den = ps[0] + ps[1] + ps[2] + ps[3]

    multi_hot = jnp.zeros(work.shape, F32)
    for k in range(TOP_K):
        multi_hot = multi_hot + (expert == idxs[k]).astype(F32)
    before = _dot(multi_hot.astype(BF16), earlier_ref[...]) + cnt_ref[:, 0:1]
    ranks = [jnp.sum(jnp.where(expert == idxs[k], before, 0.0), axis=0, keepdims=True) for k in range(TOP_K)]
    cnt_ref[...] += jnp.sum(multi_hot, axis=1, keepdims=True)

    slot = lax.broadcasted_iota(jnp.int32, (SUBLANES, tm), 0)
    e_out = jnp.zeros((SUBLANES, tm), F32)
    g_out = jnp.zeros((SUBLANES, tm), F32)
    r_out = jnp.zeros((SUBLANES, tm), F32)
    for k in range(TOP_K):
        e_out = jnp.where(slot == k, idxs[k], e_out)
        g_out = jnp.where(slot == k, ps[k] / den, g_out)
        r_out = jnp.where(slot == k, ranks[k], r_out)
    e_ref[...] = e_out.astype(jnp.int32)
    g_ref[...] = g_out
    rank_ref[...] = r_out.astype(jnp.int32)


def _post(ua, ub, zm, x, wa, wb, wo, nfw, wr, br, cnt0):
    T = x.shape[0]
    tm = min(ROW_TILE, T)
    assert T % tm == 0
    rows = lambda w: pl.BlockSpec((tm, w), lambda i: (i, 0))
    full = lambda shape: pl.BlockSpec(shape, lambda i: (0,) * len(shape))
    earlier = (jnp.arange(tm)[:, None] < jnp.arange(tm)[None, :]).astype(BF16)
    per_token = pl.BlockSpec((SUBLANES, tm), lambda i: (0, i))
    return pl.pallas_call(
        _post_kernel,
        grid=(T // tm,),
        in_specs=[rows(HEADS * DV), rows(HEADS * DV), rows(ZM_W), rows(D_MODEL),
                  full((HEADS * DV, D_MODEL)), full((HEADS * DV, D_MODEL)), full((D_MODEL, D_MODEL)),
                  full((1, D_MODEL)), full((2, D_MODEL, LANES)), full((1, LANES)), full((N_EXPERTS, LANES)),
                  full((tm, tm))],
        out_specs=[rows(D_MODEL), per_token, per_token, per_token, full((N_EXPERTS, LANES))],
        out_shape=[jax.ShapeDtypeStruct((T, D_MODEL), F32), jax.ShapeDtypeStruct((SUBLANES, T), jnp.int32),
                   jax.ShapeDtypeStruct((SUBLANES, T), F32), jax.ShapeDtypeStruct((SUBLANES, T), jnp.int32),
                   jax.ShapeDtypeStruct((N_EXPERTS, LANES), F32)],
        compiler_params=_params(1),
        name="post",
    )(ua, ub, zm, x, wa, wb, wo, nfw, wr, br, cnt0, earlier)


ROW_SUBLANES = D_MODEL // LANES


def _row_tile(r):
    return pl.ds(pl.multiple_of(r * ROW_SUBLANES, ROW_SUBLANES), ROW_SUBLANES)


def _load_row_tiles(ref, n_rows):
    return [ref[pl.ds(c, n_rows, stride=ROW_SUBLANES), :] for c in range(ROW_SUBLANES)]


def _store_row_tiles(ref, x):
    for c in range(ROW_SUBLANES):
        ref[pl.ds(c, x.shape[0], stride=ROW_SUBLANES), :] = x[:, c * LANES:(c + 1) * LANES]


def _tile_indices(dest, tm):
    n = dest.shape[1] // tm
    return dest.reshape(TOP_K, n, tm).transpose(1, 0, 2).reshape(n, 1, TOP_K * tm)


def _row_copy_wait(src_rows, dst_rows, sem):
    pltpu.make_async_copy(src_rows, dst_rows, sem).wait()


def _dispatch_kernel(dest_ref, xp_ref, xs_ref, nfw_ref, rows_ref, hbuf, sem, *, n_prompt_tiles, n_tiles):
    tm = hbuf.shape[1] // ROW_SUBLANES
    i = pl.program_id(0)
    slot = i % 2

    def wait_slot(s):
        for _ in range(TOP_K):
            _row_copy_wait(hbuf.at[s], rows_ref.at[pl.ds(0, tm * ROW_SUBLANES)], sem.at[s])

    @pl.when(i >= 2)
    def _():
        wait_slot(slot)

    @pl.when(i < n_prompt_tiles)
    def _():
        _store_row_tiles(hbuf.at[slot], _rms(xp_ref[...], nfw_ref[...]))

    @pl.when(i >= n_prompt_tiles)
    def _():
        _store_row_tiles(hbuf.at[slot], _rms(xs_ref[...], nfw_ref[...]))

    def issue(r, carry):
        src = hbuf.at[slot, _row_tile(r)]
        for k in range(TOP_K):
            d = dest_ref[0, 0, k * tm + r]
            pltpu.make_async_copy(src, rows_ref.at[_row_tile(d)], sem.at[slot]).start(priority=k % 2)
        return carry

    lax.fori_loop(0, tm, issue, 0, unroll=ISSUE_UNROLL)

    @pl.when(i == n_tiles - 1)
    def _():
        @pl.when(i >= 1)
        def _():
            wait_slot(1 - slot)
        wait_slot(slot)


def _dispatch(dest, x1_p, x1_s, nfw):
    Tp, Ts = x1_p.shape[0], x1_s.shape[0]
    tm = math.gcd(DISPATCH_TILE, Tp, Ts)
    assert tm % 8 == 0
    n_p, n_s = Tp // tm, Ts // tm
    return pl.pallas_call(
        functools.partial(_dispatch_kernel, n_prompt_tiles=n_p, n_tiles=n_p + n_s),
        grid=(n_p + n_s,),
        in_specs=[pl.BlockSpec((1, 1, tm * TOP_K), lambda i: (i, 0, 0), memory_space=pltpu.SMEM),
                  pl.BlockSpec((tm, D_MODEL), lambda i: (jnp.minimum(i, n_p - 1), 0)),
                  pl.BlockSpec((tm, D_MODEL), lambda i: (jnp.maximum(i - n_p, 0), 0)),
                  pl.BlockSpec((1, D_MODEL), lambda i: (0, 0))],
        out_specs=pl.BlockSpec(memory_space=pl.ANY),
        out_shape=jax.ShapeDtypeStruct(((Tp + Ts) * TOP_K * ROW_SUBLANES, LANES), F32),
        scratch_shapes=[pltpu.VMEM((2, tm * ROW_SUBLANES, LANES), F32), pltpu.SemaphoreType.DMA((2,))],
        compiler_params=_params(1),
        name="dispatch",
    )(_tile_indices(dest, tm), x1_p, x1_s, nfw)


def _expert_kernel(blk_ref, exp_ref, lo_ref, hi_ref, xs_ref, wgu_ref, bgu_ref, wd_ref, bd_ref, ys_ref,
                   saved, wgu_bf, wd_bf):
    del blk_ref
    i = pl.program_id(0)
    lo, hi = lo_ref[i], hi_ref[i]

    @pl.when((i == 0) | (exp_ref[i] != exp_ref[jnp.maximum(i - 1, 0)]))
    def _():
        wgu_bf[...] = wgu_ref[0].astype(BF16)
        wd_bf[...] = wd_ref[0].astype(BF16)

    @pl.when(hi > lo)
    def _():
        bm = xs_ref.shape[0] // ROW_SUBLANES

        @pl.when(lo > 0)
        def _():
            saved[...] = ys_ref[...]

        x = jnp.concatenate([t.astype(BF16) for t in _load_row_tiles(xs_ref, bm)], axis=1)
        hgu = _dot(x, wgu_bf[...]) + bgu_ref[0]
        glu = jnp.minimum(hgu[:, :D_FF], SWIGLU_LIMIT)
        lin = jnp.clip(hgu[:, D_FF:], -SWIGLU_LIMIT, SWIGLU_LIMIT)
        act = glu * jax.nn.sigmoid(SWIGLU_ALPHA * glu) * (lin + 1.0)
        _store_row_tiles(ys_ref, _dot(act.astype(BF16), wd_bf[...]) + bd_ref[0])

        @pl.when(lo > 0)
        def _():
            keep = lax.broadcasted_iota(jnp.int32, (bm, 1), 0) < lo
            for c in range(ROW_SUBLANES):
                rows = pl.ds(c, bm, stride=ROW_SUBLANES)
                ys_ref[rows, :] = jnp.where(keep, saved[rows, :], ys_ref[rows, :])


def _experts(items, xs, wgu, bgu, wd, bd):
    P = xs.shape[0] // ROW_SUBLANES
    bm = EXPERT_BLOCK
    n_items = items[0].shape[0]
    blk = lambda i, b, e, lo, hi: (b[i], 0)
    ex3 = lambda i, b, e, lo, hi: (e[i], 0, 0)
    return pl.pallas_call(
        _expert_kernel,
        grid_spec=pltpu.PrefetchScalarGridSpec(
            num_scalar_prefetch=4,
            grid=(n_items,),
            in_specs=[pl.BlockSpec((bm * ROW_SUBLANES, LANES), blk),
                      pl.BlockSpec((1, D_MODEL, 2 * D_FF), ex3),
                      pl.BlockSpec((1, 1, 2 * D_FF), ex3),
                      pl.BlockSpec((1, D_FF, D_MODEL), ex3),
                      pl.BlockSpec((1, 1, D_MODEL), ex3)],
            out_specs=pl.BlockSpec((bm * ROW_SUBLANES, LANES), blk),
            scratch_shapes=[pltpu.VMEM((bm * ROW_SUBLANES, LANES), F32),
                            pltpu.VMEM((D_MODEL, 2 * D_FF), BF16), pltpu.VMEM((D_FF, D_MODEL), BF16)]),
        out_shape=jax.ShapeDtypeStruct((P * ROW_SUBLANES, LANES), F32),
        compiler_params=_params(1),
        name="experts",
    )(*items, xs, wgu, bgu, wd, bd)


def _combine_kernel(dest_ref, dest_next_ref, gate_ref, x1_ref, nw_ref, ys_ref, out_ref, buf, sem, *, n_tiles):
    tm = x1_ref.shape[0]
    i = pl.program_id(0)
    slot = i % 2

    def issue_tile(idx_ref, s):
        def issue(r, carry):
            for k in range(TOP_K):
                d = idx_ref[0, 0, k * tm + r]
                pltpu.make_async_copy(ys_ref.at[_row_tile(d)], buf.at[s, k, _row_tile(r)],
                                      sem.at[s]).start(priority=k % 2)
            return carry

        lax.fori_loop(0, tm, issue, 0, unroll=ISSUE_UNROLL)

    @pl.when(i == 0)
    def _():
        issue_tile(dest_ref, slot)

    @pl.when(i + 1 < n_tiles)
    def _():
        issue_tile(dest_next_ref, 1 - slot)

    for k in range(TOP_K):
        _row_copy_wait(ys_ref.at[pl.ds(0, tm * ROW_SUBLANES)], buf.at[slot, k], sem.at[slot])
    gate = jnp.concatenate([gate_ref[...], jnp.zeros((tm - SUBLANES, tm), F32)], axis=0).T
    chunks = [_load_row_tiles(buf.at[slot, k], tm) for k in range(TOP_K)]
    f = []
    for c in range(ROW_SUBLANES):
        fc = gate[:, 0:1] * chunks[0][c]
        for k in range(1, TOP_K):
            fc = fc + gate[:, k:k + 1] * chunks[k][c]
        f.append(fc)
    out_ref[...] = _rms(x1_ref[...] + jnp.concatenate(f, axis=1), nw_ref[...])


def _combine(dest, gate, x1, nw, ys):
    T = x1.shape[0]
    tm = min(COMBINE_TILE, T)
    assert T % tm == 0
    n = T // tm
    dest3 = _tile_indices(dest, tm)
    return pl.pallas_call(
        functools.partial(_combine_kernel, n_tiles=n),
        grid=(n,),
        in_specs=[pl.BlockSpec((1, 1, tm * TOP_K), lambda i: (i, 0, 0), memory_space=pltpu.SMEM),
                  pl.BlockSpec((1, 1, tm * TOP_K), lambda i: (jnp.minimum(i + 1, n - 1), 0, 0),
                               memory_space=pltpu.SMEM),
                  pl.BlockSpec((SUBLANES, tm), lambda i: (0, i)),
                  pl.BlockSpec((tm, D_MODEL), lambda i: (i, 0)),
                  pl.BlockSpec((1, D_MODEL), lambda i: (0, 0)),
                  pl.BlockSpec(memory_space=pl.ANY)],
        out_specs=pl.BlockSpec((tm, D_MODEL), lambda i: (i, 0)),
        out_shape=jax.ShapeDtypeStruct((T, D_MODEL), F32),
        scratch_shapes=[pltpu.VMEM((2, TOP_K, tm * ROW_SUBLANES, LANES), F32), pltpu.SemaphoreType.DMA((2,))],
        compiler_params=_params(1),
        name="combine",
    )(dest3, dest3, gate, x1, nw, ys)


def _rotary_tables(pos):
    inv = ROPE_BASE ** (-jnp.arange(0, DK, 2, dtype=F32) / DK)
    ang = pos.astype(F32)[:, None] * inv[None, :]
    cos, sin = jnp.cos(ang), jnp.sin(ang)
    return jnp.concatenate([cos, cos], axis=1), jnp.concatenate([-sin, sin], axis=1)


def _routing(top_e, rank, counts):
    bm = EXPERT_BLOCK
    n_rows = top_e.size
    n_blocks = n_rows // bm
    n_items = n_blocks + N_EXPERTS - 1
    start = jnp.cumsum(counts) - counts
    end = start + counts
    expert_ids = jnp.arange(N_EXPERTS, dtype=jnp.int32)
    onehot = top_e[None, :, :] == expert_ids[:, None, None]
    dest = (jnp.sum(jnp.where(onehot, start[:, None, None], 0), axis=0) + rank).astype(jnp.int32)

    first_blk = start // bm
    n_it = jnp.where(counts > 0, (end - 1) // bm - first_blk + 1, 0)
    it_end = jnp.cumsum(n_it)
    it_start = it_end - n_it
    j = jnp.arange(n_items, dtype=jnp.int32)
    e_j = jnp.minimum(jnp.sum(j[:, None] >= it_end[None, :], axis=1), N_EXPERTS - 1).astype(jnp.int32)
    valid = j < it_end[-1]
    last_e = jnp.max(jnp.where(counts > 0, expert_ids, 0))
    e_j = jnp.where(valid, e_j, last_e)
    blk = jnp.where(valid, first_blk[e_j] + (j - it_start[e_j]), n_blocks - 1)
    lo = jnp.maximum(start[e_j], blk * bm) - blk * bm
    hi = jnp.minimum(end[e_j], (blk + 1) * bm) - blk * bm
    lo = jnp.where(valid, lo, 0)
    hi = jnp.where(valid, hi, 0)
    items = tuple(a.astype(jnp.int32) for a in (blk, e_j, lo, hi))
    return dest, items


def kernel(x_prompt, x_sample, state_gla, state_ret, meta_tokens, norm_mix_w, w_in, w_alpha_up, b_alpha, gla_norm_w, ret_norm_w, w_branch_gla, w_branch_ret, w_out, norm_ffn_w, w_router, b_router, w_gate_up, b_gate_up, w_down, b_down, norm_final_w):
    B, S, D = x_prompt.shape
    DB, DS, _ = x_sample.shape
    assert D == D_MODEL and S % CHUNK == 0 and DS == CHUNK and state_gla.shape[0] == 1
    l = 0

    offs = [0]
    for w in IN_WIDTHS:
        offs.append(offs[-1] + w)
    cols = [w_in[l][:, offs[i]:offs[i + 1]] for i in range(len(IN_WIDTHS))]
    qa, ka, va, ga, ra, qb, kb, vb, gb, m = cols
    ra = jnp.pad(ra, ((0, 0), (0, LANES - GLA_RANK)))
    ws = jnp.concatenate([qa, ka, va, ga, qb, kb, vb, gb, ra], axis=1).astype(BF16)
    wm = m.astype(BF16)
    wup = jnp.pad(w_alpha_up[l], ((0, LANES - GLA_RANK), (0, 0))).astype(BF16)
    ba = b_alpha[l][None, :]
    nmw = norm_mix_w[l][None, :]
    gnw = gla_norm_w[l][None, :]
    rnw = ret_norm_w[l][None, :]

    def mixer(x2d, pos, sg0, sr0, batch):
        T = x2d.shape[0]
        cs, sn = _rotary_tables(pos)
        reps = max(1, min(ROW_TILE, T) // pos.shape[0])
        zp, zm, dec, bmin = _prep(x2d, nmw, ws, wm, wup, ba, jnp.tile(cs, (reps, 1)), jnp.tile(sn, (reps, 1)))

        def ratio_form():
            ua, ub, sgt, srt = _scan_fast(zp, dec, jnp.swapaxes(sg0, 2, 3), jnp.swapaxes(sr0, 2, 3), gnw, rnw, batch)
            return ua, ub, jnp.swapaxes(sgt, 2, 3), jnp.swapaxes(srt, 2, 3)

        def exact_form():
            return tuple(_scan(_inproj(x2d, nmw, ws), cs, sn, sg0, sr0, wup, ba, gnw, rnw, batch))

        ua, ub, sg, sr = lax.cond(jnp.min(bmin) >= MIN_SAFE_LOG_FORGET, ratio_form, exact_form)
        return ua, ub, zm, sg, sr

    x_meta = jnp.concatenate([jnp.zeros((CHUNK - N_META, D), F32), meta_tokens.astype(F32)], axis=0)
    pos_meta = jnp.maximum(jnp.arange(CHUNK) - (CHUNK - N_META), 0)
    zero_state = jnp.zeros((1, HEADS, DK, DV), F32)
    _, _, _, sg_m, sr_m = mixer(x_meta, pos_meta, zero_state, zero_state, 1)

    xp = x_prompt.reshape(B * S, D)
    xs_ = x_sample.reshape(DB * DS, D)
    sg0 = jnp.broadcast_to(sg_m, (B, HEADS, DK, DV))
    sr0 = jnp.broadcast_to(sr_m, (B, HEADS, DK, DV))
    ua_p, ub_p, zm_p, sg_p, sr_p = mixer(xp, N_META + jnp.arange(S), sg0, sr0, B)
    ua_s, ub_s, zm_s, sg_s, sr_s = mixer(xs_, N_META + PAST_LEN + jnp.arange(DS),
                                         state_gla[l].astype(F32), state_ret[l].astype(F32), DB)

    wa = w_branch_gla[l].astype(BF16)
    wb = w_branch_ret[l].astype(BF16)
    wo = w_out[l].astype(BF16)
    nfw = norm_ffn_w[l][None, :]
    wr = jnp.pad(w_router[l].astype(F32), ((0, 0), (0, LANES - N_EXPERTS)))
    wr_hi = wr.astype(BF16)
    wr_mid = (wr - wr_hi.astype(F32)).astype(BF16)
    wr2 = jnp.stack([wr_hi, wr_mid])
    br = jnp.pad(b_router[l].astype(F32), (0, LANES - N_EXPERTS))[None, :]
    cnt0 = jnp.zeros((N_EXPERTS, LANES), F32)
    x1_p, e_p, g_p, r_p, cnt_p = _post(ua_p, ub_p, zm_p, xp, wa, wb, wo, nfw, wr2, br, cnt0)
    x1_s, e_s, g_s, r_s, cnt = _post(ua_s, ub_s, zm_s, xs_, wa, wb, wo, nfw, wr2, br, cnt_p)

    Tp, Ts = B * S, DB * DS
    assert ((Tp + Ts) * TOP_K) % EXPERT_BLOCK == 0
    top_e = jnp.concatenate([e_p[:TOP_K], e_s[:TOP_K]], axis=1)
    rank = jnp.concatenate([r_p[:TOP_K], r_s[:TOP_K]], axis=1)
    dest, items = _routing(top_e, rank, cnt[:, 0].astype(jnp.int32))
    dest_p, dest_s = dest[:, :Tp], dest[:, Tp:]

    xs_rows = _dispatch(dest, x1_p, x1_s, nfw)
    ys_rows = _experts(items, xs_rows, w_gate_up[l], b_gate_up[l][:, None, :],
                       w_down[l], b_down[l][:, None, :])
    nw_final = norm_final_w[None, :]
    y_p = _combine(dest_p, g_p, x1_p, nw_final, ys_rows)
    y_s = _combine(dest_s, g_s, x1_s, nw_final, ys_rows)

    dt = state_gla.dtype
    return (y_p.reshape(B, S, D), y_s.reshape(DB, DS, D),
            sg_p[None].astype(dt), sr_p[None].astype(state_ret.dtype),
            sg_s[None].astype(dt), sr_s[None].astype(state_ret.dtype))
```
